```python
import math
import jax, jax.numpy as jnp
from jax import lax
import numpy as np

D_MODEL = 2048
BATCH = 4
SEQ = 2048
DEPTH = 4
DEC_BATCH = 8
DEC_SEQ = 8
PAST_LEN = 16384
PAGE_SIZE = 128

N_EVEN = (DEPTH + 1) // 2
N_ODD = DEPTH // 2
NORM_EPS = 1e-5
NEG_INF = -1e30

SSD_INNER = D_MODEL
SSD_HEADDIM = 64
SSD_HEADS = SSD_INNER // SSD_HEADDIM
SSD_GROUPS = 4
SSD_STATE = 128
SSD_CONV = 4
SSD_CHUNK = 128
CONV_CH = SSD_INNER + 2 * SSD_GROUPS * SSD_STATE
SC_DIM = D_MODEL
SC_WIDTH = 3
EV_SPLITS = [SSD_INNER, CONV_CH, SSD_HEADS, SC_DIM, SC_DIM, SC_DIM, SC_DIM]
EV_IN = sum(EV_SPLITS)
EV_MIX = SSD_INNER + SC_DIM

ATT_HEADS = 16
KV_HEADS = 4
HEAD_DIM = 128
Q_PER_KV = ATT_HEADS // KV_HEADS
ATT_Q = ATT_HEADS * HEAD_DIM
ATT_KV = KV_HEADS * HEAD_DIM
ATT_IN = 2 * ATT_Q + 2 * ATT_KV
MOBA_BLOCK = 256
MOBA_TOPK = 3
MOBA_QCHUNK = 16
PAGES_PER_BLOCK = MOBA_BLOCK // PAGE_SIZE

N_MEM = 256
CA_HEADS = 4
CA_HEAD_DIM = 128
CA_W = CA_HEADS * CA_HEAD_DIM

kernel_name = 'hybrid_ssd_shortconv_moba_memxattn_step'


def _split(a, sizes):
    return jnp.split(a, [int(s) for s in np.cumsum(sizes)[:-1]], axis=-1)


def _rmsnorm(x, w):
    xf = x.astype(jnp.float32)
    y = xf * lax.rsqrt(jnp.mean(xf * xf, axis=-1, keepdims=True) + NORM_EPS)
    return (y * w.astype(jnp.float32)).astype(x.dtype)


def _alibi_slopes():
    return jnp.asarray(np.array([2.0 ** (-8.0 * (h + 1) / ATT_HEADS) for h in range(ATT_HEADS)], dtype=np.float32))


def _causal_dwconv(x, state, w):
    width = w.shape[0]
    L = x.shape[1]
    xp = jnp.concatenate([state.astype(x.dtype), x], axis=1)
    y = xp[:, 0:L] * w[0]
    for k in range(1, width):
        y = y + xp[:, k:k + L] * w[k]
    return y, xp[:, L:]


def _segsum(a):
    cs = jnp.cumsum(a, axis=-1)
    T = a.shape[-1]
    diff = cs[..., :, None] - cs[..., None, :]
    return jnp.where(jnp.tril(jnp.ones((T, T), bool)), diff, -jnp.inf)


def _ssd_scan(x, dt, a_head, b, c, h0):
    bsz, L, H, P = x.shape
    G, N = b.shape[2], b.shape[3]
    E = H // G
    Q = min(SSD_CHUNK, L)
    Lp = -(-L // Q) * Q
    if Lp != L:
        pw = Lp - L
        x = jnp.pad(x, ((0, 0), (0, pw), (0, 0), (0, 0)))
        dt = jnp.pad(dt, ((0, 0), (0, pw), (0, 0)))
        b = jnp.pad(b, ((0, 0), (0, pw), (0, 0), (0, 0)))
        c = jnp.pad(c, ((0, 0), (0, pw), (0, 0), (0, 0)))
    nc = Lp // Q
    xr = (x * dt[..., None]).reshape(bsz, nc, Q, G, E, P)
    a_t = jnp.moveaxis((dt * a_head).reshape(bsz, nc, Q, G, E), 2, -1)
    br = b.reshape(bsz, nc, Q, G, N)
    cr = c.reshape(bsz, nc, Q, G, N)
    a_cs = jnp.cumsum(a_t, axis=-1)
    lmat = jnp.exp(_segsum(a_t))
    cb = jnp.einsum('bclgn,bcsgn->bcgls', cr, br)
    y_diag = jnp.einsum('bcgls,bcgels,bcsgep->bclgep', cb, lmat, xr)
    decay_states = jnp.exp(a_cs[..., -1:] - a_cs)
    states = jnp.einsum('bclgn,bcgel,bclgep->bcgepn', br, decay_states, xr)
    states = jnp.concatenate([h0.reshape(bsz, 1, G, E, P, N), states], axis=1)
    chunk_a = jnp.pad(a_cs[..., -1], ((0, 0), (1, 0), (0, 0), (0, 0)))
    decay_chunk = jnp.exp(_segsum(jnp.moveaxis(chunk_a, 1, -1)))
    new_states = jnp.einsum('bgezc,bcgepn->bzgepn', decay_chunk, states)
    states_in, h_final = new_states[:, :-1], new_states[:, -1]
    y_off = jnp.einsum('bclgn,bcgepn,bcgel->bclgep', cr, states_in, jnp.exp(a_cs))
    y = (y_diag + y_off).reshape(bsz, Lp, H, P)[:, :L]
    return y, h_final.reshape(bsz, H, P, N)


def _even_mixer(h, w_in, conv_w, conv_b, dt_bias, a_log, d_skip, norm_w, sc_w, w_out, conv_state, ssm_state, sc_state):
    bsz, L, _ = h.shape
    f32 = jnp.float32
    z, xbc, dt_raw, sc_b, sc_c, sc_x, sc_g = _split(h @ w_in, EV_SPLITS)
    xbc, new_conv = _causal_dwconv(xbc, conv_state, conv_w)
    xbc = jax.nn.silu(xbc + conv_b)
    xs, bs, cs = _split(xbc, [SSD_INNER, SSD_GROUPS * SSD_STATE, SSD_GROUPS * SSD_STATE])
    xs = xs.reshape(bsz, L, SSD_HEADS, SSD_HEADDIM).astype(f32)
    bs = bs.reshape(bsz, L, SSD_GROUPS, SSD_STATE).astype(f32)
    cs = cs.reshape(bsz, L, SSD_GROUPS, SSD_STATE).astype(f32)
    dt = jax.nn.softplus(dt_raw.astype(f32) + dt_bias.astype(f32))
    a_head = -jnp.exp(a_log.astype(f32))
    y, new_ssm = _ssd_scan(xs, dt, a_head, bs, cs, ssm_state.astype(f32))
    y = y + d_skip.astype(f32)[:, None] * xs
    y = y.reshape(bsz, L, SSD_INNER).astype(h.dtype) * jax.nn.silu(z)
    y = _rmsnorm(y.reshape(bsz, L, SSD_GROUPS, SSD_INNER // SSD_GROUPS),
                 norm_w.reshape(SSD_GROUPS, SSD_INNER // SSD_GROUPS)).reshape(bsz, L, SSD_INNER)
    u, new_sc = _causal_dwconv(sc_c * sc_x, sc_state, sc_w)
    y_sc = sc_b * u * jax.nn.silu(sc_g)
    out = jnp.concatenate([y, y_sc], axis=-1) @ w_out
    return out, new_conv, new_ssm.astype(ssm_state.dtype), new_sc


def _attn_proj(h, w_in):
    bsz, L, _ = h.shape
    q, k, v, g = _split(h @ w_in, [ATT_Q, ATT_KV, ATT_KV, ATT_Q])
    q = q.reshape(bsz, L, ATT_HEADS, HEAD_DIM) * (HEAD_DIM ** -0.5)
    k = k.reshape(bsz, L, KV_HEADS, HEAD_DIM)
    v = v.reshape(bsz, L, KV_HEADS, HEAD_DIM)
    return q, k, v, g


def _attn_out(o, g, w_out):
    bsz, L = o.shape[0], o.shape[1]
    return (o.reshape(bsz, L, ATT_Q) * jax.nn.silu(g)) @ w_out


def _moba_attend(qg, k_own, v_own, bias_own, k_sel, v_sel, bias_sel):
    s_own = jnp.einsum('bkgqd,bknd->bkgqn', qg, k_own).astype(jnp.float32) + bias_own
    if k_sel is None:
        p = jax.nn.softmax(s_own, axis=-1).astype(v_own.dtype)
        return jnp.einsum('bkgqn,bknd->bkgqd', p, v_own)
    s_sel = jnp.einsum('bkgqd,bkgqjnd->bkgqjn', qg, k_sel).astype(jnp.float32) + bias_sel
    n_own = s_own.shape[-1]
    j, n_blk = s_sel.shape[-2], s_sel.shape[-1]
    s = jnp.concatenate([s_own, s_sel.reshape(s_sel.shape[:-2] + (j * n_blk,))], axis=-1)
    p = jax.nn.softmax(s, axis=-1).astype(v_own.dtype)
    p_sel = p[..., n_own:].reshape(s_sel.shape)
    return (jnp.einsum('bkgqn,bknd->bkgqd', p[..., :n_own], v_own)
            + jnp.einsum('bkgqjn,bkgqjnd->bkgqd', p_sel, v_sel))


def _moba_prompt(q, k, v, slopes):
    bsz, seq = q.shape[0], q.shape[1]
    nb = -(-seq // MOBA_BLOCK)
    sp = nb * MOBA_BLOCK
    padw = ((0, 0), (0, sp - seq), (0, 0), (0, 0))
    q, k, v = jnp.pad(q, padw), jnp.pad(k, padw), jnp.pad(v, padw)
    qg = q.reshape(bsz, sp, KV_HEADS, Q_PER_KV, HEAD_DIM).transpose(0, 2, 3, 1, 4)
    kb = k.reshape(bsz, nb, MOBA_BLOCK, KV_HEADS, HEAD_DIM).transpose(0, 3, 1, 2, 4)
    vb = v.reshape(bsz, nb, MOBA_BLOCK, KV_HEADS, HEAD_DIM).transpose(0, 3, 1, 2, 4)
    slope = slopes.reshape(1, KV_HEADS, Q_PER_KV, 1, 1)
    n_ch = sp // MOBA_QCHUNK

    def to_chunks(a):
        return jnp.moveaxis(a.reshape(a.shape[:3] + (n_ch, MOBA_QCHUNK) + a.shape[4:]), 3, 0)

    xs = [jnp.arange(n_ch), to_chunks(qg)]
    ksel = min(MOBA_TOPK, nb - 1)
    if ksel > 0:
        kmean = jnp.mean(kb.astype(jnp.float32), axis=3)
        gate = jnp.einsum('bkgsd,bknd->bkgsn', qg.astype(jnp.float32), kmean)
        valid = jnp.arange(nb)[None, :] < (jnp.arange(sp) // MOBA_BLOCK)[:, None]
        gate = jnp.where(valid, gate, NEG_INF)
        gval, gidx = lax.top_k(gate, ksel)
        xs += [to_chunks(gidx), to_chunks(gval > 0.5 * NEG_INF)]
    bi = jnp.arange(bsz)[:, None, None, None, None]
    ki = jnp.arange(KV_HEADS)[None, :, None, None, None]

    def chunk_fn(args):
        c, q_c = args[0], args[1]
        qpos = c * MOBA_QCHUNK + jnp.arange(MOBA_QCHUNK)
        ob = (c * MOBA_QCHUNK) // MOBA_BLOCK
        k_own = lax.dynamic_index_in_dim(kb, ob, axis=2, keepdims=False)
        v_own = lax.dynamic_index_in_dim(vb, ob, axis=2, keepdims=False)
        kpos = ob * MOBA_BLOCK + jnp.arange(MOBA_BLOCK)
        dist = (qpos[:, None] - kpos[None, :]).astype(jnp.float32)
        bias_own = jnp.where(dist >= 0, -slope * dist, NEG_INF)
        if ksel > 0:
            idx_c, ok_c = args[2], args[3]
            k_sel = kb[bi, ki, idx_c]
            v_sel = vb[bi, ki, idx_c]
            kpos_sel = idx_c[..., None] * MOBA_BLOCK + jnp.arange(MOBA_BLOCK)
            dist_sel = (qpos[:, None, None] - kpos_sel).astype(jnp.float32)
            bias_sel = jnp.where(ok_c[..., None], -slope[..., None] * dist_sel, NEG_INF)
            return _moba_attend(q_c, k_own, v_own, bias_own, k_sel, v_sel, bias_sel)
        return _moba_attend(q_c, k_own, v_own, bias_own, None, None, None)

    out = lax.map(chunk_fn, tuple(xs))
    out = jnp.moveaxis(out, 0, 3).reshape(bsz, KV_HEADS, Q_PER_KV, sp, HEAD_DIM)
    return out.transpose(0, 3, 1, 2, 4).reshape(bsz, sp, ATT_HEADS, HEAD_DIM)[:, :seq]


def _moba_sample(q, k_new, v_new, k_pool, v_pool, page_table, slopes):
    db, ds = q.shape[0], q.shape[1]
    qg = q.reshape(db, ds, KV_HEADS, Q_PER_KV, HEAD_DIM).transpose(0, 2, 3, 1, 4)
    qpos = PAST_LEN + jnp.arange(ds)
    slope = slopes.reshape(1, KV_HEADS, Q_PER_KV, 1, 1)
    k_past = k_pool[page_table].reshape(db, PAST_LEN, KV_HEADS, HEAD_DIM)
    nbf = PAST_LEN // MOBA_BLOCK
    own_start = nbf * MOBA_BLOCK
    v_own_past = v_pool[page_table[:, own_start // PAGE_SIZE:]].reshape(db, PAST_LEN - own_start, KV_HEADS, HEAD_DIM)
    k_own = jnp.concatenate([k_past[:, own_start:], k_new.astype(k_past.dtype)], axis=1).transpose(0, 2, 1, 3)
    v_own = jnp.concatenate([v_own_past, v_new.astype(v_own_past.dtype)], axis=1).transpose(0, 2, 1, 3)
    kpos = own_start + jnp.arange(k_own.shape[2])
    dist_i = qpos[:, None] - kpos[None, :]
    own_ok = (dist_i >= 0) & ((kpos[None, :] // MOBA_BLOCK) == (qpos[:, None] // MOBA_BLOCK))
    dist = dist_i.astype(jnp.float32)
    bias_own = jnp.where(own_ok, -slope * dist, NEG_INF)
    ksel = min(MOBA_TOPK, nbf)
    if ksel == 0:
        o = _moba_attend(qg, k_own, v_own, bias_own, None, None, None)
    else:
        kb = k_past[:, :own_start].reshape(db, nbf, MOBA_BLOCK, KV_HEADS, HEAD_DIM).transpose(0, 3, 1, 2, 4)
        kmean = jnp.mean(kb.astype(jnp.float32), axis=3)
        gate = jnp.einsum('bkgqd,bknd->bkgqn', qg.astype(jnp.float32), kmean)
        valid = jnp.arange(nbf)[None, :] < (qpos // MOBA_BLOCK)[:, None]
        gate = jnp.where(valid, gate, NEG_INF)
        gval, gidx = lax.top_k(gate, ksel)
        ok = gval > 0.5 * NEG_INF
        bi = jnp.arange(db)[:, None, None, None, None]
        ki = jnp.arange(KV_HEADS)[None, :, None, None, None]
        k_sel = kb[bi, ki, gidx]
        lpage = gidx[..., None] * PAGES_PER_BLOCK + jnp.arange(PAGES_PER_BLOCK)
        phys = page_table[bi[..., None], lpage]
        v_sel = v_pool[phys, :, ki[..., None], :]
        v_sel = v_sel.reshape(gidx.shape + (MOBA_BLOCK, HEAD_DIM))
        kpos_sel = gidx[..., None] * MOBA_BLOCK + jnp.arange(MOBA_BLOCK)
        dist_sel = (qpos[:, None, None] - kpos_sel).astype(jnp.float32)
        bias_sel = jnp.where(ok[..., None], -slope[..., None] * dist_sel, NEG_INF)
        o = _moba_attend(qg, k_own, v_own, bias_own, k_sel, v_sel, bias_sel)
    return o.transpose(0, 3, 1, 2, 4).reshape(db, ds, ATT_HEADS, HEAD_DIM)


def _mem_kv(mem, norm_w, w_kv):
    bsz = mem.shape[0]
    k, v = _split(_rmsnorm(mem, norm_w) @ w_kv, [CA_W, CA_W])
    return (k.reshape(bsz, N_MEM, CA_HEADS, CA_HEAD_DIM), v.reshape(bsz, N_MEM, CA_HEADS, CA_HEAD_DIM))


def _cross_attn(h, mem_k, mem_v, w_q, w_out):
    bsz, L, _ = h.shape
    q, g = _split(h @ w_q, [CA_W, CA_W])
    q = q.reshape(bsz, L, CA_HEADS, CA_HEAD_DIM) * (CA_HEAD_DIM ** -0.5)
    s = jnp.einsum('blhd,bmhd->bhlm', q, mem_k.astype(q.dtype)).astype(jnp.float32)
    p = jax.nn.softmax(s, axis=-1).astype(mem_v.dtype)
    o = jnp.einsum('bhlm,bmhd->blhd', p, mem_v).reshape(bsz, L, CA_W)
    return (o.astype(h.dtype) * jax.nn.silu(g)) @ w_out


def setup_inputs(seed: int = 0) -> dict:
    key = jax.random.key(seed)
    ks = jax.random.split(key, 32)
    f32 = jnp.float32

    def nrm(k, shape, scale=1.0):
        return jax.random.normal(k, shape, f32) * scale

    n_pages = PAST_LEN // PAGE_SIZE
    n_used = DEC_BATCH * n_pages
    n_pool = n_used + max(1, n_used // 4)
    page_table = jax.random.permutation(ks[5], n_pool)[:n_used].reshape(DEC_BATCH, n_pages).astype(jnp.int32)
    dt0 = jnp.exp(jax.random.uniform(ks[18], (N_EVEN, SSD_HEADS), f32) * (math.log(0.1) - math.log(0.001)) + math.log(0.001))
    return {
        'x_prompt': nrm(ks[0], (BATCH, SEQ, D_MODEL)),
        'x_sample': nrm(ks[1], (DEC_BATCH, DEC_SEQ, D_MODEL)),
        'mem_prompt': nrm(ks[2], (BATCH, N_MEM, D_MODEL)),
        'cache_k': nrm(ks[3], (N_ODD, n_pool, PAGE_SIZE, KV_HEADS, HEAD_DIM)),
        'cache_v': nrm(ks[4], (N_ODD, n_pool, PAGE_SIZE, KV_HEADS, HEAD_DIM)),
        'page_table': page_table,
        'state_conv': nrm(ks[6], (N_EVEN, DEC_BATCH, SSD_CONV - 1, CONV_CH)),
        'state_ssm': nrm(ks[7], (N_EVEN, DEC_BATCH, SSD_HEADS, SSD_HEADDIM, SSD_STATE), 0.5),
        'state_sc': nrm(ks[8], (N_EVEN, DEC_BATCH, SC_WIDTH - 1, SC_DIM)),
        'cache_mem_k': nrm(ks[9], (DEPTH, DEC_BATCH, N_MEM, CA_HEADS, CA_HEAD_DIM)),
        'cache_mem_v': nrm(ks[10], (DEPTH, DEC_BATCH, N_MEM, CA_HEADS, CA_HEAD_DIM)),
        'norm_mix_w': 1.0 + nrm(ks[11], (DEPTH, D_MODEL), 0.02),
        'norm_cross_w': 1.0 + nrm(ks[12], (DEPTH, D_MODEL), 0.02),
        'norm_mem_w': 1.0 + nrm(ks[13], (DEPTH, D_MODEL), 0.02),
        'final_norm_w': 1.0 + nrm(ks[14], (D_MODEL,), 0.02),
        'ev_w_in': nrm(ks[15], (N_EVEN, D_MODEL, EV_IN), D_MODEL ** -0.5),
        'ev_conv_w': nrm(ks[16], (N_EVEN, SSD_CONV, CONV_CH), SSD_CONV ** -0.5),
        'ev_conv_b': nrm(ks[17], (N_EVEN, CONV_CH), 0.01),
        'ev_dt_bias': dt0 + jnp.log(-jnp.expm1(-dt0)),
        'ev_a_log': jnp.log(jax.random.uniform(ks[19], (N_EVEN, SSD_HEADS), f32, 1.0, 16.0)),
        'ev_d_skip': 1.0 + nrm(ks[20], (N_EVEN, SSD_HEADS), 0.1),
        'ev_norm_w': 1.0 + nrm(ks[21], (N_EVEN, SSD_INNER), 0.02),
        'ev_sc_w': nrm(ks[22], (N_EVEN, SC_WIDTH, SC_DIM), SC_WIDTH ** -0.5),
        'ev_w_out': nrm(ks[23], (N_EVEN, EV_MIX, D_MODEL), EV_MIX ** -0.5),
        'od_w_in': nrm(ks[24], (N_ODD, D_MODEL, ATT_IN), D_MODEL ** -0.5),
        'od_w_out': nrm(ks[25], (N_ODD, ATT_Q, D_MODEL), ATT_Q ** -0.5),
        'ca_w_q': nrm(ks[26], (DEPTH, D_MODEL, 2 * CA_W), D_MODEL ** -0.5),
        'ca_w_kv': nrm(ks[27], (DEPTH, D_MODEL, 2 * CA_W), D_MODEL ** -0.5),
        'ca_w_out': nrm(ks[28], (DEPTH, CA_W, D_MODEL), CA_W ** -0.5),
    }


def reference(x_prompt, x_sample, mem_prompt, cache_k, cache_v, page_table, state_conv, state_ssm, state_sc,
              cache_mem_k, cache_mem_v, norm_mix_w, norm_cross_w, norm_mem_w, final_norm_w,
              ev_w_in, ev_conv_w, ev_conv_b, ev_dt_bias, ev_a_log, ev_d_skip, ev_norm_w, ev_sc_w, ev_w_out,
              od_w_in, od_w_out, ca_w_q, ca_w_kv, ca_w_out):
    slopes = _alibi_slopes()
    bp = x_prompt.shape[0]
    xp, xs = x_prompt, x_sample
    pk, pv, sk, sv = [], [], [], []
    pconv, pssm, psc, sconv, sssm, ssc = [], [], [], [], [], []
    pmk, pmv = [], []
    for l in range(DEPTH):
        i = l // 2
        hp = _rmsnorm(xp, norm_mix_w[l])
        hs = _rmsnorm(xs, norm_mix_w[l])
        if l % 2 == 0:
            w = (ev_w_in[i], ev_conv_w[i], ev_conv_b[i], ev_dt_bias[i], ev_a_log[i], ev_d_skip[i], ev_norm_w[i], ev_sc_w[i], ev_w_out[i])
            z_conv = jnp.zeros((bp, SSD_CONV - 1, CONV_CH), x_prompt.dtype)
            z_ssm = jnp.zeros((bp, SSD_HEADS, SSD_HEADDIM, SSD_STATE), state_ssm.dtype)
            z_sc = jnp.zeros((bp, SC_WIDTH - 1, SC_DIM), x_prompt.dtype)
            op, c1, s1, q1 = _even_mixer(hp, *w, z_conv, z_ssm, z_sc)
            os_, c2, s2, q2 = _even_mixer(hs, *w, state_conv[i], state_ssm[i], state_sc[i])
            pconv.append(c1); pssm.append(s1); psc.append(q1)
            sconv.append(c2); sssm.append(s2); ssc.append(q2)
        else:
            q1, k1, v1, g1 = _attn_proj(hp, od_w_in[i])
            op = _attn_out(_moba_prompt(q1, k1, v1, slopes), g1, od_w_out[i])
            q2, k2, v2, g2 = _attn_proj(hs, od_w_in[i])
            os_ = _attn_out(_moba_sample(q2, k2, v2, cache_k[i], cache_v[i], page_table, slopes), g2, od_w_out[i])
            pk.append(k1); pv.append(v1); sk.append(k2); sv.append(v2)
        xp = xp + op
        xs = xs + os_
        mk, mv = _mem_kv(mem_prompt, norm_mem_w[l], ca_w_kv[l])
        pmk.append(mk); pmv.append(mv)
        xp = xp + _cross_attn(_rmsnorm(xp, norm_cross_w[l]), mk, mv, ca_w_q[l], ca_w_out[l])
        xs = xs + _cross_attn(_rmsnorm(xs, norm_cross_w[l]), cache_mem_k[l], cache_mem_v[l], ca_w_q[l], ca_w_out[l])
    y_prompt = _rmsnorm(xp, final_norm_w)
    y_sample = _rmsnorm(xs, final_norm_w)
    return (y_prompt, y_sample,
            jnp.stack(pk), jnp.stack(pv), jnp.stack(pconv), jnp.stack(pssm), jnp.stack(psc),
            jnp.stack(pmk), jnp.stack(pmv),
            jnp.stack(sk), jnp.stack(sv), jnp.stack(sconv), jnp.stack(sssm), jnp.stack(ssc))
```

```python
import functools
import math

import numpy as np
import jax
import jax.numpy as jnp
from jax import lax
from jax.experimental import pallas as pl
from jax.experimental.pallas import tpu as pltpu

F32 = jnp.float32
BF16 = jnp.bfloat16
HIGHEST = lax.Precision.HIGHEST

NORM_EPS = 1e-5
NEG_INF = -1e30

LANES = 128
SUBLANES = 8
VMEM_LIMIT = 48 * 1024 * 1024

SSD_HEADDIM = 64
SSD_GROUPS = 4
SSD_STATE = 128
SSD_CONV = 4
SSD_CHUNK = 128
SC_WIDTH = 3
ATT_HEADS = 16
KV_HEADS = 4
HEAD_DIM = 128
Q_PER_KV = ATT_HEADS // KV_HEADS
MOBA_BLOCK = 256
MOBA_TOPK = 3
PAGE_SIZE = 128
CA_HEADS = 4
CA_HEAD_DIM = 128


def _cparams(*sem):
    return pltpu.CompilerParams(dimension_semantics=sem, vmem_limit_bytes=VMEM_LIMIT)


def _tile(n, pref):
    if n <= pref:
        return n
    t = pref
    while n % t:
        t //= 2
    return t


def _silu(x):
    return x / (1.0 + jnp.exp(-x))


def _dot(a, b, **kw):
    return jnp.dot(a, b, preferred_element_type=F32, **kw)


def _dot_nt(a, b, **kw):
    return lax.dot_general(a, b, (((1,), (1,)), ((), ())), preferred_element_type=F32, **kw)


def _norm_mm_kernel(x_ref, nw_ref, w_ref, o_ref, xn_ref):
    @pl.when(pl.program_id(1) == 0)
    def _():
        x = x_ref[...]
        ms = jnp.mean(x * x, axis=-1, keepdims=True)
        xn_ref[...] = ((x * lax.rsqrt(ms + NORM_EPS)) * nw_ref[...]).astype(BF16)

    o_ref[...] = _dot(xn_ref[...], w_ref[...]).astype(o_ref.dtype)


def norm_matmul(x, norm_w, w, *, tm=512, tn=1024, out_dtype=F32):
    m, k = x.shape
    n = w.shape[1]
    tm, tn = _tile(m, tm), _tile(n, tn)
    return pl.pallas_call(
        _norm_mm_kernel,
        out_shape=jax.ShapeDtypeStruct((m, n), out_dtype),
        grid=(m // tm, n // tn),
        in_specs=[pl.BlockSpec((tm, k), lambda i, j: (i, 0)),
                  pl.BlockSpec((1, k), lambda i, j: (0, 0)),
                  pl.BlockSpec((k, tn), lambda i, j: (0, j))],
        out_specs=pl.BlockSpec((tm, tn), lambda i, j: (i, j)),
        scratch_shapes=[pltpu.VMEM((tm, k), BF16)],
        compiler_params=_cparams("parallel", "arbitrary"),
        name="norm_matmul",
    )(x, norm_w.reshape(1, k), w)


def _mm_res_kernel(*refs, n_pairs):
    res_ref, o_ref = refs[2 * n_pairs], refs[2 * n_pairs + 1]
    acc = res_ref[...]
    for p in range(n_pairs):
        acc = acc + _dot(refs[2 * p][...], refs[2 * p + 1][...])
    o_ref[...] = acc


def matmul_residual(pairs, res, *, tm=512, tn=1024):
    m, n = res.shape
    tm, tn = _tile(m, tm), _tile(n, tn)
    in_specs, args = [], []
    for a, w in pairs:
        k = a.shape[1]
        in_specs += [pl.BlockSpec((tm, k), lambda i, j: (i, 0)),
                     pl.BlockSpec((k, tn), lambda i, j: (0, j))]
        args += [a, w]
    in_specs.append(pl.BlockSpec((tm, tn), lambda i, j: (i, j)))
    return pl.pallas_call(
        functools.partial(_mm_res_kernel, n_pairs=len(pairs)),
        out_shape=jax.ShapeDtypeStruct((m, n), F32),
        grid=(m // tm, n // tn),
        in_specs=in_specs,
        out_specs=pl.BlockSpec((tm, tn), lambda i, j: (i, j)),
        compiler_params=_cparams("parallel", "parallel"),
        name="matmul_residual",
    )(*args, res)


def _rmsnorm_kernel(x_ref, nw_ref, o_ref):
    x = x_ref[...]
    ms = jnp.mean(x * x, axis=-1, keepdims=True)
    o_ref[...] = (x * lax.rsqrt(ms + NORM_EPS)) * nw_ref[...]


def rmsnorm(x, norm_w, *, tm=512):
    m, k = x.shape
    tm = _tile(m, tm)
    return pl.pallas_call(
        _rmsnorm_kernel,
        out_shape=jax.ShapeDtypeStruct((m, k), F32),
        grid=(m // tm,),
        in_specs=[pl.BlockSpec((tm, k), lambda i: (i, 0)),
                  pl.BlockSpec((1, k), lambda i: (0, 0))],
        out_specs=pl.BlockSpec((tm, k), lambda i: (i, 0)),
        compiler_params=_cparams("parallel"),
        name="rmsnorm",
    )(x, norm_w.reshape(1, k))


def _causal_conv_chunk(x_ref, buf_ref, cw_ref, cb_ref, q):
    x = x_ref[...]
    buf_ref[SUBLANES:SUBLANES + q, :] = x
    y = cw_ref[SSD_CONV - 1:SSD_CONV, :] * x
    for k in range(1, SSD_CONV):
        y = y + cw_ref[SSD_CONV - 1 - k:SSD_CONV - k, :] * buf_ref[SUBLANES - k:SUBLANES - k + q, :]
    if cb_ref is not None:
        y = y + cb_ref[...]
    buf_ref[0:SUBLANES, :] = x[q - SUBLANES:q, :]
    return y


def _ssd_kernel(z_ref, xs_ref, b_ref, c_ref, dt_ref, cs_xs_ref, cs_b_ref, cs_c_ref, h0_ref,
                cw_xs_ref, cw_b_ref, cw_c_ref, cb_xs_ref, cb_b_ref, cb_c_ref,
                dtb_ref, alog_ref, dskip_ref, nw_ref, e_ref,
                y_ref, hfin_ref,
                ht_ref, buf_xs, buf_b, buf_c, *, q, n_valid, n_chunks):
    c = pl.program_id(1)
    n_heads = e_ref.shape[1] // SSD_HEADDIM
    gw = (n_heads // SSD_GROUPS) * SSD_HEADDIM

    @pl.when(c == 0)
    def _():
        ht_ref[...] = h0_ref[...].T
        buf_xs[0:SUBLANES, :] = cs_xs_ref[...]
        buf_b[0:SUBLANES, :] = cs_b_ref[...]
        buf_c[0:SUBLANES, :] = cs_c_ref[...]

    xs = _silu(_causal_conv_chunk(xs_ref, buf_xs, cw_xs_ref, cb_xs_ref, q))
    bm = _silu(_causal_conv_chunk(b_ref, buf_b, cw_b_ref, cb_b_ref, q))
    cm = _silu(_causal_conv_chunk(c_ref, buf_c, cw_c_ref, cb_c_ref, q))

    row = lax.broadcasted_iota(jnp.int32, (q, q), 0)
    col = lax.broadcasted_iota(jnp.int32, (q, q), 1)
    tril = row >= col

    dtv = dt_ref[...] + dtb_ref[...]
    dt = jnp.maximum(dtv, 0.0) + jnp.log(1.0 + jnp.exp(-jnp.abs(dtv)))
    if n_valid < q:
        dt = jnp.where(lax.broadcasted_iota(jnp.int32, dt.shape, 0) < n_valid, dt, 0.0)
    a = dt * (-jnp.exp(alog_ref[...]))
    a_cs = _dot(tril.astype(F32), a, precision=HIGHEST)
    a_cs_t = a_cs.T
    expand = e_ref[...]
    dt_full = _dot(dt, expand, precision=HIGHEST)
    acs_full = _dot(a_cs, expand, precision=HIGHEST)
    tot_full = acs_full[q - 1:q, :]
    xr = xs * dt_full
    xr_dec = (xr * jnp.exp(tot_full - acs_full)).astype(BF16)
    exp_acs = jnp.exp(acs_full)
    exp_tot = jnp.exp(tot_full)

    lane = lax.broadcasted_iota(jnp.int32, (q, LANES), 1)
    lo_half = lane < SSD_HEADDIM
    heads_per_tile = LANES // SSD_HEADDIM

    y = dskip_ref[...] * xs
    y_parts = []
    for g in range(SSD_GROUPS):
        bg = bm[:, g * SSD_STATE:(g + 1) * SSD_STATE]
        cg = cm[:, g * SSD_STATE:(g + 1) * SSD_STATE].astype(BF16)
        bg_t = bg.T.astype(BF16)
        cb = _dot(cg, bg_t)
        ht_g = ht_ref[:, g * gw:(g + 1) * gw]
        y_g = _dot(cg, ht_g.astype(BF16)) * exp_acs[:, g * gw:(g + 1) * gw]
        tiles = []
        for t in range(gw // LANES):
            base = g * gw + t * LANES
            xr_t = xr[:, base:base + LANES]
            acc = None
            for e in range(heads_per_tile):
                h = base // SSD_HEADDIM + e
                seg = a_cs[:, h:h + 1] - a_cs_t[h:h + 1, :]
                lmat = jnp.where(tril, jnp.exp(seg), 0.0)
                in_head = lo_half if e == 0 else jnp.logical_not(lo_half)
                part = _dot((cb * lmat).astype(BF16), jnp.where(in_head, xr_t, 0.0).astype(BF16))
                acc = part if acc is None else acc + part
            tiles.append(acc)
        y_g = y_g + jnp.concatenate(tiles, axis=1)
        states = _dot(bg_t, xr_dec[:, g * gw:(g + 1) * gw])
        ht_ref[:, g * gw:(g + 1) * gw] = exp_tot[:, g * gw:(g + 1) * gw] * ht_g + states
        y_parts.append(y_g)
    y = y + jnp.concatenate(y_parts, axis=1)

    y = y * _silu(z_ref[...])
    for g in range(SSD_GROUPS):
        yg = y[:, g * gw:(g + 1) * gw]
        ms = jnp.mean(yg * yg, axis=-1, keepdims=True)
        y_ref[:, g * gw:(g + 1) * gw] = ((yg * lax.rsqrt(ms + NORM_EPS)) * nw_ref[:, g * gw:(g + 1) * gw]).astype(y_ref.dtype)

    @pl.when(c == n_chunks - 1)
    def _():
        hfin_ref[...] = ht_ref[...].T


def ssd_mixer(proj, dt_raw, conv_state8, h0, wts, *, batch, seq, n_valid):
    q = min(SSD_CHUNK, seq)
    assert seq % q == 0 and q % SUBLANES == 0
    nc = seq // q
    assert n_valid == q or nc == 1
    inner = h0.shape[1]
    gn = SSD_GROUPS * SSD_STATE
    assert inner % gn == 0 and (inner // gn) * gn == inner
    kb = inner // gn
    rows = lambda b, c: b * nc + c
    full = lambda b, c: (0, 0)
    in_specs = [
        pl.BlockSpec((q, inner), lambda b, c: (rows(b, c), 0)),
        pl.BlockSpec((q, inner), lambda b, c: (rows(b, c), 1)),
        pl.BlockSpec((q, gn), lambda b, c: (rows(b, c), 2 * kb)),
        pl.BlockSpec((q, gn), lambda b, c: (rows(b, c), 2 * kb + 1)),
        pl.BlockSpec((q, LANES), lambda b, c: (rows(b, c), 0)),
        pl.BlockSpec((None, SUBLANES, inner), lambda b, c: (b, 0, 0)),
        pl.BlockSpec((None, SUBLANES, gn), lambda b, c: (b, 0, kb)),
        pl.BlockSpec((None, SUBLANES, gn), lambda b, c: (b, 0, kb + 1)),
        pl.BlockSpec((None, inner, SSD_STATE), lambda b, c: (b, 0, 0)),
        pl.BlockSpec((SSD_CONV, inner), full),
        pl.BlockSpec((SSD_CONV, gn), lambda b, c: (0, kb)),
        pl.BlockSpec((SSD_CONV, gn), lambda b, c: (0, kb + 1)),
        pl.BlockSpec((1, inner), full),
        pl.BlockSpec((1, gn), lambda b, c: (0, kb)),
        pl.BlockSpec((1, gn), lambda b, c: (0, kb + 1)),
        pl.BlockSpec((1, LANES), full),
        pl.BlockSpec((1, LANES), full),
        pl.BlockSpec((1, inner), full),
        pl.BlockSpec((1, inner), full),
        pl.BlockSpec((LANES, inner), full),
    ]
    cw, cb = wts["conv_w"], wts["conv_b"]
    y, hfin = pl.pallas_call(
        functools.partial(_ssd_kernel, q=q, n_valid=n_valid, n_chunks=nc),
        out_shape=(jax.ShapeDtypeStruct((batch * seq, inner), BF16),
                   jax.ShapeDtypeStruct((batch, inner, SSD_STATE), F32)),
        grid=(batch, nc),
        in_specs=in_specs,
        out_specs=(pl.BlockSpec((q, inner), lambda b, c: (rows(b, c), 0)),
                   pl.BlockSpec((None, inner, SSD_STATE), lambda b, c: (b, 0, 0))),
        scratch_shapes=[pltpu.VMEM((SSD_STATE, inner), F32),
                        pltpu.VMEM((q + SUBLANES, inner), F32),
                        pltpu.VMEM((q + SUBLANES, gn), F32),
                        pltpu.VMEM((q + SUBLANES, gn), F32)],
        compiler_params=_cparams("parallel", "arbitrary"),
        name="ssd_mixer",
    )(proj, proj, proj, proj, dt_raw, conv_state8, conv_state8, conv_state8, h0,
      cw, cw, cw, cb, cb, cb, wts["dt_bias"], wts["a_log"], wts["d_skip"], wts["norm_w"], wts["expand"])
    return y, hfin


def _short_conv_kernel(b_ref, c_ref, x_ref, g_ref, st_ref, w_ref, y_ref, last_ref, buf_ref, *, tq):
    @pl.when(pl.program_id(2) == 0)
    def _():
        buf_ref[0:SUBLANES, :] = st_ref[...]

    prod = c_ref[...] * x_ref[...]
    buf_ref[SUBLANES:SUBLANES + tq, :] = prod
    u = w_ref[SC_WIDTH - 1:SC_WIDTH, :] * prod
    for k in range(1, SC_WIDTH):
        u = u + w_ref[SC_WIDTH - 1 - k:SC_WIDTH - k, :] * buf_ref[SUBLANES - k:SUBLANES - k + tq, :]
    tail = prod[tq - SUBLANES:tq, :]
    buf_ref[0:SUBLANES, :] = tail
    last_ref[...] = tail
    y_ref[...] = (b_ref[...] * u * _silu(g_ref[...])).astype(y_ref.dtype)


def short_conv(proj, state8, w, *, batch, seq, dim, col0, tq=256, tc=1024):
    tq = _tile(seq, tq)
    nt = seq // tq
    ncb = dim // tc
    assert col0 % tc == 0 and dim % tc == 0 and tq % SUBLANES == 0
    cb0 = col0 // tc

    def part(k):
        return pl.BlockSpec((tq, tc), lambda b, j, t: (b * nt + t, cb0 + k * ncb + j))

    return pl.pallas_call(
        functools.partial(_short_conv_kernel, tq=tq),
        out_shape=(jax.ShapeDtypeStruct((batch * seq, dim), BF16),
                   jax.ShapeDtypeStruct((batch, SUBLANES, dim), F32)),
        grid=(batch, ncb, nt),
        in_specs=[part(0), part(1), part(2), part(3),
                  pl.BlockSpec((None, SUBLANES, tc), lambda b, j, t: (b, 0, j)),
                  pl.BlockSpec((SC_WIDTH, tc), lambda b, j, t: (0, j))],
        out_specs=(pl.BlockSpec((tq, tc), lambda b, j, t: (b * nt + t, j)),
                   pl.BlockSpec((None, SUBLANES, tc), lambda b, j, t: (b, 0, j))),
        scratch_shapes=[pltpu.VMEM((tq + SUBLANES, tc), F32)],
        compiler_params=_cparams("parallel", "parallel", "arbitrary"),
        name="short_conv",
    )(proj, proj, proj, proj, state8, w)


def _top_blocks(gate, n_valid_lanes):
    lane = lax.broadcasted_iota(jnp.int32, gate.shape, 1).astype(F32)
    g = jnp.where(lane < jnp.asarray(n_valid_lanes, F32), gate, NEG_INF)
    sel = jnp.zeros(gate.shape, F32)
    for _ in range(MOBA_TOPK):
        m = jnp.max(g, axis=1, keepdims=True)
        first = jnp.min(jnp.where(g == m, lane, float(gate.shape[1])), axis=1, keepdims=True)
        pick = lane == first
        sel = jnp.where(pick & (m > 0.5 * NEG_INF), 1.0, sel)
        g = jnp.where(pick, -jnp.inf, g)
    return sel


def _moba_prompt_kernel(slopes_ref, q_ref, k_ref, v_ref, g_ref, o_ref, kmean_ref, *, n_blocks):
    h = pl.program_id(1)
    i = pl.program_id(2)
    blk = MOBA_BLOCK

    @pl.when((i == 0) & (h % Q_PER_KV == 0))
    def _():
        kmean_ref[...] = jnp.zeros(kmean_ref.shape, F32)
        for j in range(n_blocks):
            kmean_ref[j:j + 1, :] = jnp.sum(k_ref[j * blk:(j + 1) * blk, :], axis=0, keepdims=True) * (1.0 / blk)

    slope = slopes_ref[h]
    q = q_ref[...] * (HEAD_DIM ** -0.5)
    gate = _dot_nt(q, kmean_ref[...], precision=HIGHEST)
    sel = _top_blocks(gate, i)
    lane = lax.broadcasted_iota(jnp.int32, sel.shape, 1)
    qb = q.astype(BF16)
    rel = (lax.broadcasted_iota(jnp.int32, (blk, blk), 0)
           - lax.broadcasted_iota(jnp.int32, (blk, blk), 1)).astype(F32)

    own = pl.multiple_of(i * blk, blk)
    s = _dot_nt(qb, k_ref[pl.ds(own, blk), :].astype(BF16))
    s = jnp.where(rel >= 0, s - slope * rel, NEG_INF)
    m0 = jnp.max(s, axis=1, keepdims=True)
    p = jnp.exp(s - m0)
    l0 = jnp.sum(p, axis=1, keepdims=True)
    acc0 = _dot(p.astype(BF16), v_ref[pl.ds(own, blk), :].astype(BF16))

    def past_block(j, carry):
        m, l, acc = carry
        start = pl.multiple_of(j * blk, blk)
        sj = _dot_nt(qb, k_ref[pl.ds(start, blk), :].astype(BF16))
        dist = rel + ((i - j) * blk).astype(F32)
        chosen = jnp.max(jnp.where(lane == j, sel, 0.0), axis=1, keepdims=True) > 0.0
        sj = jnp.where(chosen, sj - slope * dist, NEG_INF)
        m_new = jnp.maximum(m, jnp.max(sj, axis=1, keepdims=True))
        alpha = jnp.exp(m - m_new)
        pj = jnp.exp(sj - m_new)
        l = alpha * l + jnp.sum(pj, axis=1, keepdims=True)
        acc = alpha * acc + _dot(pj.astype(BF16), v_ref[pl.ds(start, blk), :].astype(BF16))
        return m_new, l, acc

    _, l, acc = lax.fori_loop(0, i, past_block, (m0, l0, acc0))
    o_ref[...] = ((acc / l) * _silu(g_ref[...])).astype(o_ref.dtype)


def moba_prompt(proj, slopes, *, batch, seq):
    assert seq % MOBA_BLOCK == 0
    nb = seq // MOBA_BLOCK
    assert nb <= LANES
    kcol = ATT_HEADS
    vcol = ATT_HEADS + KV_HEADS
    gcol = ATT_HEADS + 2 * KV_HEADS
    return pl.pallas_call(
        functools.partial(_moba_prompt_kernel, n_blocks=nb),
        out_shape=jax.ShapeDtypeStruct((batch * seq, ATT_HEADS * HEAD_DIM), BF16),
        grid=(batch, ATT_HEADS, nb),
        in_specs=[pl.BlockSpec(memory_space=pltpu.SMEM),
                  pl.BlockSpec((MOBA_BLOCK, HEAD_DIM), lambda b, h, i: (b * nb + i, h)),
                  pl.BlockSpec((seq, HEAD_DIM), lambda b, h, i: (b, kcol + h // Q_PER_KV)),
                  pl.BlockSpec((seq, HEAD_DIM), lambda b, h, i: (b, vcol + h // Q_PER_KV)),
                  pl.BlockSpec((MOBA_BLOCK, HEAD_DIM), lambda b, h, i: (b * nb + i, gcol + h))],
        out_specs=pl.BlockSpec((MOBA_BLOCK, HEAD_DIM), lambda b, h, i: (b * nb + i, h)),
        scratch_shapes=[pltpu.VMEM((LANES, HEAD_DIM), F32)],
        compiler_params=_cparams("arbitrary", "arbitrary", "arbitrary"),
        name="moba_prompt",
    )(slopes, proj, proj, proj, proj)


def _kmean_kernel(pt_ref, k0_ref, k1_ref, o_ref):
    del pt_ref
    o_ref[...] = (jnp.sum(k0_ref[...], axis=0, keepdims=True)
                  + jnp.sum(k1_ref[...], axis=0, keepdims=True)) * (1.0 / MOBA_BLOCK)


def paged_block_means(k_pool, page_table, *, n_blocks):
    db = page_table.shape[0]
    width = k_pool.shape[2]
    assert MOBA_BLOCK == 2 * PAGE_SIZE
    out = pl.pallas_call(
        _kmean_kernel,
        out_shape=jax.ShapeDtypeStruct((db, n_blocks, 1, width), F32),
        grid_spec=pltpu.PrefetchScalarGridSpec(
            num_scalar_prefetch=1,
            grid=(db, n_blocks),
            in_specs=[pl.BlockSpec((None, PAGE_SIZE, width), lambda b, j, pt: (pt[b, 2 * j], 0, 0)),
                      pl.BlockSpec((None, PAGE_SIZE, width), lambda b, j, pt: (pt[b, 2 * j + 1], 0, 0))],
            out_specs=pl.BlockSpec((None, None, 1, width), lambda b, j, pt: (b, j, 0, 0))),
        compiler_params=_cparams("parallel", "arbitrary"),
        name="paged_block_means",
    )(page_table, k_pool, k_pool)
    return out.reshape(db, n_blocks, width)


def _moba_sample_kernel(pt_ref, q_ref, kmean_ref, slope_ref, knew_ref, vnew_ref,
                        k0_ref, k1_ref, v0_ref, v1_ref, o_ref,
                        m_ref, l_ref, acc_ref, sel_ref, *, n_blocks, past_len, dec_seq):
    del pt_ref
    j = pl.program_id(1)
    rows = q_ref.shape[0]
    rpk = rows // KV_HEADS
    q = q_ref[...] * (HEAD_DIM ** -0.5)
    qb = q.astype(BF16)
    slope = slope_ref[...][:, 0:1]
    t_row = lax.broadcasted_iota(jnp.int32, (rows, LANES), 0) % dec_seq
    key = lax.broadcasted_iota(jnp.int32, (rows, LANES), 1)

    def per_kv(fn):
        return jnp.concatenate([fn(kv, slice(kv * rpk, (kv + 1) * rpk), slice(kv * HEAD_DIM, (kv + 1) * HEAD_DIM))
                                for kv in range(KV_HEADS)], axis=0)

    @pl.when(j == 0)
    def _():
        gate = per_kv(lambda kv, r, c: _dot_nt(q[r, :], kmean_ref[:, c], precision=HIGHEST))
        sel_ref[...] = _top_blocks(gate, n_blocks)
        s = per_kv(lambda kv, r, c: _dot_nt(qb[r, :], knew_ref[:, c].astype(BF16)))
        dist = (t_row - key).astype(F32)
        s = jnp.where(dist >= 0, s - slope * dist, NEG_INF)
        m = jnp.max(s, axis=1, keepdims=True)
        p = jnp.exp(s - m)
        pb = p.astype(BF16)
        m_ref[...] = jnp.broadcast_to(m, m_ref.shape)
        l_ref[...] = jnp.broadcast_to(jnp.sum(p, axis=1, keepdims=True), l_ref.shape)
        acc_ref[...] = per_kv(lambda kv, r, c: _dot(pb[r, :], vnew_ref[:, c].astype(BF16)))

    chosen = jnp.max(jnp.where(key == j, sel_ref[...], 0.0), axis=1, keepdims=True) > 0.0
    s_pages = []
    for pg, k_ref in enumerate((k0_ref, k1_ref)):
        s = per_kv(lambda kv, r, c: _dot_nt(qb[r, :], k_ref[:, c].astype(BF16)))
        dist = (past_len + t_row - (j * MOBA_BLOCK + pg * PAGE_SIZE) - key).astype(F32)
        s_pages.append(jnp.where(chosen, s - slope * dist, NEG_INF))
    s = jnp.concatenate(s_pages, axis=1)
    m_old = m_ref[...][:, 0:1]
    m_new = jnp.maximum(m_old, jnp.max(s, axis=1, keepdims=True))
    alpha = jnp.exp(m_old - m_new)
    p = jnp.exp(s - m_new)
    pb = p.astype(BF16)
    pv = per_kv(lambda kv, r, c: _dot(pb[r, 0:PAGE_SIZE], v0_ref[:, c].astype(BF16))
                + _dot(pb[r, PAGE_SIZE:2 * PAGE_SIZE], v1_ref[:, c].astype(BF16)))
    l_new = alpha * l_ref[...][:, 0:1] + jnp.sum(p, axis=1, keepdims=True)
    acc_new = alpha * acc_ref[...] + pv
    m_ref[...] = jnp.broadcast_to(m_new, m_ref.shape)
    l_ref[...] = jnp.broadcast_to(l_new, l_ref.shape)
    acc_ref[...] = acc_new

    @pl.when(j == n_blocks - 1)
    def _():
        o_ref[...] = acc_new / l_new


def moba_sample(q_rows, kmean_pad, slope_rows, k_new_pad, v_new_pad, k_pool, v_pool, page_table,
                *, n_blocks, past_len, dec_seq):
    db, rows, _ = q_rows.shape
    width = k_pool.shape[2]
    page = lambda pg: pl.BlockSpec((None, PAGE_SIZE, width), lambda b, j, pt: (pt[b, 2 * j + pg], 0, 0))
    per_b = lambda shape: pl.BlockSpec((None,) + shape, lambda b, j, pt: (b, 0, 0))
    return pl.pallas_call(
        functools.partial(_moba_sample_kernel, n_blocks=n_blocks, past_len=past_len, dec_seq=dec_seq),
        out_shape=jax.ShapeDtypeStruct((db, rows, HEAD_DIM), F32),
        grid_spec=pltpu.PrefetchScalarGridSpec(
            num_scalar_prefetch=1,
            grid=(db, n_blocks),
            in_specs=[per_b((rows, HEAD_DIM)),
                      per_b((LANES, width)),
                      pl.BlockSpec((rows, LANES), lambda b, j, pt: (0, 0)),
                      per_b((LANES, width)),
                      per_b((LANES, width)),
                      page(0), page(1), page(0), page(1)],
            out_specs=per_b((rows, HEAD_DIM)),
            scratch_shapes=[pltpu.VMEM((rows, LANES), F32), pltpu.VMEM((rows, LANES), F32),
                            pltpu.VMEM((rows, HEAD_DIM), F32), pltpu.VMEM((rows, LANES), F32)]),
        compiler_params=_cparams("parallel", "arbitrary"),
        name="moba_sample",
    )(page_table, q_rows, kmean_pad, slope_rows, k_new_pad, v_new_pad, k_pool, k_pool, v_pool, v_pool)


def _gate_kernel(o_ref, g_ref, y_ref):
    y_ref[...] = (o_ref[...] * _silu(g_ref[...])).astype(y_ref.dtype)


def silu_gate(o, proj, *, col0, tn=1024):
    m, n = o.shape
    assert col0 % tn == 0 and n % tn == 0
    cb = col0 // tn
    return pl.pallas_call(
        _gate_kernel,
        out_shape=jax.ShapeDtypeStruct((m, n), BF16),
        grid=(n // tn,),
        in_specs=[pl.BlockSpec((m, tn), lambda j: (0, j)),
                  pl.BlockSpec((m, tn), lambda j: (0, cb + j))],
        out_specs=pl.BlockSpec((m, tn), lambda j: (0, j)),
        compiler_params=_cparams("parallel"),
        name="silu_gate",
    )(o, proj)


def _cross_attn_kernel(qg_ref, k_ref, v_ref, o_ref):
    width = CA_HEADS * CA_HEAD_DIM
    for h in range(CA_HEADS):
        c = slice(h * CA_HEAD_DIM, (h + 1) * CA_HEAD_DIM)
        q = (qg_ref[:, c] * (CA_HEAD_DIM ** -0.5)).astype(BF16)
        s = _dot_nt(q, k_ref[:, c].astype(BF16))
        m = jnp.max(s, axis=1, keepdims=True)
        p = jnp.exp(s - m)
        l = jnp.sum(p, axis=1, keepdims=True)
        o = _dot(p.astype(BF16), v_ref[:, c].astype(BF16)) / l
        g = qg_ref[:, width + h * CA_HEAD_DIM:width + (h + 1) * CA_HEAD_DIM]
        o_ref[:, c] = (o * _silu(g)).astype(o_ref.dtype)


def cross_attention(qg, mem_k, mem_v, *, batch, seq, tm=256):
    width = CA_HEADS * CA_HEAD_DIM
    n_mem = mem_k.shape[1]
    tm = _tile(seq, tm)
    nt = seq // tm
    return pl.pallas_call(
        _cross_attn_kernel,
        out_shape=jax.ShapeDtypeStruct((batch * seq, width), BF16),
        grid=(batch * nt,),
        in_specs=[pl.BlockSpec((tm, 2 * width), lambda i: (i, 0)),
                  pl.BlockSpec((None, n_mem, width), lambda i: (i // nt, 0, 0)),
                  pl.BlockSpec((None, n_mem, width), lambda i: (i // nt, 0, 0))],
        out_specs=pl.BlockSpec((tm, width), lambda i: (i, 0)),
        compiler_params=_cparams("parallel"),
        name="cross_attention",
    )(qg, mem_k, mem_v)


def _alibi_slopes():
    return np.array([2.0 ** (-8.0 * (h + 1) / ATT_HEADS) for h in range(ATT_HEADS)], dtype=np.float32)


def _pad_rows(a, n, front=False):
    extra = n - a.shape[-2]
    pad = [(0, 0)] * a.ndim
    pad[-2] = (extra, 0) if front else (0, extra)
    return jnp.pad(a, pad)


def _pad_lanes(a):
    pad = [(0, 0)] * a.ndim
    pad[-1] = (0, LANES - a.shape[-1])
    return jnp.pad(a, pad)


def _even_layer(x, w, conv_state, ssm_state, sc_state, *, batch, seq):
    d = x.shape[1]
    inner = d
    conv_ch = inner + 2 * SSD_GROUPS * SSD_STATE
    proj = norm_matmul(x, w["norm_mix"], w["w_in"])
    dt_raw = norm_matmul(x, w["norm_mix"], w["w_dt"])
    new_conv = proj.reshape(batch, seq, -1)[:, seq - (SSD_CONV - 1):, inner:inner + conv_ch]

    lpad = -(-seq // SSD_CHUNK) * SSD_CHUNK if seq > SSD_CHUNK else max(seq, SSD_CHUNK)
    if lpad != seq:
        assert seq < SSD_CHUNK
        ssd_in = _pad_rows(proj.reshape(batch, seq, -1)[:, :, :inner + conv_ch], lpad).reshape(batch * lpad, -1)
        dt_in = _pad_rows(dt_raw.reshape(batch, seq, -1), lpad).reshape(batch * lpad, -1)
    else:
        ssd_in, dt_in = proj, dt_raw
    y, h_fin = ssd_mixer(ssd_in, dt_in, _pad_rows(conv_state, SUBLANES, front=True),
                         ssm_state.reshape(batch, inner, SSD_STATE), w,
                         batch=batch, seq=lpad, n_valid=min(seq, SSD_CHUNK))
    if lpad != seq:
        y = y.reshape(batch, lpad, inner)[:, :seq].reshape(batch * seq, inner)

    y_sc, sc_tail = short_conv(proj, _pad_rows(sc_state, SUBLANES, front=True), w["sc_w"],
                               batch=batch, seq=seq, dim=d, col0=inner + conv_ch)
    x = matmul_residual([(y, w["w_out_ssd"]), (y_sc, w["w_out_sc"])], x)
    return (x, new_conv, h_fin.reshape(batch, inner // SSD_HEADDIM, SSD_HEADDIM, SSD_STATE),
            sc_tail[:, SUBLANES - (SC_WIDTH - 1):, :])


def _odd_prompt(x, w, slopes, *, batch, seq):
    proj = norm_matmul(x, w["norm_mix"], w["w_in"])
    att_q, att_kv = ATT_HEADS * HEAD_DIM, KV_HEADS * HEAD_DIM
    k = proj[:, att_q:att_q + att_kv].reshape(batch, seq, KV_HEADS, HEAD_DIM)
    v = proj[:, att_q + att_kv:att_q + 2 * att_kv].reshape(batch, seq, KV_HEADS, HEAD_DIM)
    og = moba_prompt(proj, slopes, batch=batch, seq=seq)
    return matmul_residual([(og, w["w_out"])], x), k, v


def _odd_sample(x, w, k_pool, v_pool, page_table, slope_rows, *, batch, seq, past_len):
    assert past_len % MOBA_BLOCK == 0 and seq <= MOBA_BLOCK and seq <= LANES
    n_blocks = past_len // MOBA_BLOCK
    assert MOBA_TOPK <= n_blocks <= LANES
    proj = norm_matmul(x, w["norm_mix"], w["w_in"])
    att_q, att_kv = ATT_HEADS * HEAD_DIM, KV_HEADS * HEAD_DIM
    k_new = proj[:, att_q:att_q + att_kv]
    v_new = proj[:, att_q + att_kv:att_q + 2 * att_kv]
    q_rows = proj[:, :att_q].reshape(batch, seq, KV_HEADS, Q_PER_KV, HEAD_DIM).transpose(0, 2, 3, 1, 4)
    q_rows = q_rows.reshape(batch, ATT_HEADS * seq, HEAD_DIM)
    kmean = paged_block_means(k_pool, page_table, n_blocks=n_blocks)
    o = moba_sample(q_rows, _pad_rows(kmean, LANES), slope_rows,
                    _pad_rows(k_new.reshape(batch, seq, att_kv), LANES),
                    _pad_rows(v_new.reshape(batch, seq, att_kv), LANES),
                    k_pool, v_pool, page_table, n_blocks=n_blocks, past_len=past_len, dec_seq=seq)
    o = o.reshape(batch, KV_HEADS, Q_PER_KV, seq, HEAD_DIM).transpose(0, 3, 1, 2, 4).reshape(batch * seq, att_q)
    og = silu_gate(o, proj, col0=att_q + 2 * att_kv)
    return (matmul_residual([(og, w["w_out"])], x),
            k_new.reshape(batch, seq, KV_HEADS, HEAD_DIM), v_new.reshape(batch, seq, KV_HEADS, HEAD_DIM))


def _cross_layer(x, w, mem_k, mem_v, *, batch, seq):
    qg = norm_matmul(x, w["norm_cross"], w["ca_w_q"])
    og = cross_attention(qg, mem_k, mem_v, batch=batch, seq=seq)
    return matmul_residual([(og, w["ca_w_out"])], x)


def kernel(x_prompt, x_sample, mem_prompt, cache_k, cache_v, page_table, state_conv, state_ssm, state_sc,
           cache_mem_k, cache_mem_v, norm_mix_w, norm_cross_w, norm_mem_w, final_norm_w,
           ev_w_in, ev_conv_w, ev_conv_b, ev_dt_bias, ev_a_log, ev_d_skip, ev_norm_w, ev_sc_w, ev_w_out,
           od_w_in, od_w_out, ca_w_q, ca_w_kv, ca_w_out):
    bp, sp, d = x_prompt.shape
    bs, ss, _ = x_sample.shape
    depth = norm_mix_w.shape[0]
    n_mem = mem_prompt.shape[1]
    n_heads = ev_dt_bias.shape[1]
    inner = n_heads * SSD_HEADDIM
    conv_ch = inner + 2 * SSD_GROUPS * SSD_STATE
    past_len = page_table.shape[1] * PAGE_SIZE
    ca_w = CA_HEADS * CA_HEAD_DIM
    assert inner == d and n_heads <= LANES

    slopes = jnp.asarray(_alibi_slopes())
    slope_rows = jnp.asarray(np.repeat(_alibi_slopes(), ss)[:, None] * np.ones((1, LANES), np.float32))
    expand_np = np.zeros((LANES, inner), np.float32)
    for hd in range(n_heads):
        expand_np[hd, hd * SSD_HEADDIM:(hd + 1) * SSD_HEADDIM] = 1.0
    expand = jnp.asarray(expand_np)

    xp = x_prompt.reshape(bp * sp, d)
    xs = x_sample.reshape(bs * ss, d)
    mem = mem_prompt.reshape(bp * n_mem, d)
    k_pools = cache_k.reshape(cache_k.shape[0], cache_k.shape[1], PAGE_SIZE, KV_HEADS * HEAD_DIM)
    v_pools = cache_v.reshape(cache_v.shape[0], cache_v.shape[1], PAGE_SIZE, KV_HEADS * HEAD_DIM)

    pk, pv, sk, sv = [], [], [], []
    pconv, pssm, psc, sconv, sssm, ssc = [], [], [], [], [], []
    pmk, pmv = [], []
    for l in range(depth):
        i = l // 2
        if l % 2 == 0:
            w_in = ev_w_in[i]
            dt0 = inner + conv_ch
            w = {
                "norm_mix": norm_mix_w[l],
                "w_in": jnp.concatenate([w_in[:, :dt0], w_in[:, dt0 + n_heads:]], axis=1).astype(BF16),
                "w_dt": _pad_lanes(w_in[:, dt0:dt0 + n_heads]).astype(BF16),
                "conv_w": ev_conv_w[i],
                "conv_b": ev_conv_b[i].reshape(1, conv_ch),
                "dt_bias": _pad_lanes(ev_dt_bias[i].reshape(1, n_heads)),
                "a_log": _pad_lanes(ev_a_log[i].reshape(1, n_heads)),
                "d_skip": jnp.repeat(ev_d_skip[i], SSD_HEADDIM).reshape(1, inner),
                "norm_w": ev_norm_w[i].reshape(1, inner),
                "expand": expand,
                "sc_w": ev_sc_w[i],
                "w_out_ssd": ev_w_out[i][:inner].astype(BF16),
                "w_out_sc": ev_w_out[i][inner:].astype(BF16),
            }
            xp, c1, s1, q1 = _even_layer(xp, w, jnp.zeros((bp, SSD_CONV - 1, conv_ch), F32),
                                         jnp.zeros((bp, n_heads, SSD_HEADDIM, SSD_STATE), F32),
                                         jnp.zeros((bp, SC_WIDTH - 1, d), F32), batch=bp, seq=sp)
            xs, c2, s2, q2 = _even_layer(xs, w, state_conv[i], state_ssm[i], state_sc[i], batch=bs, seq=ss)
            pconv.append(c1); pssm.append(s1); psc.append(q1)
            sconv.append(c2); sssm.append(s2); ssc.append(q2)
        else:
            w = {"norm_mix": norm_mix_w[l], "w_in": od_w_in[i].astype(BF16), "w_out": od_w_out[i].astype(BF16)}
            xp, k1, v1 = _odd_prompt(xp, w, slopes, batch=bp, seq=sp)
            xs, k2, v2 = _odd_sample(xs, w, k_pools[i], v_pools[i], page_table, slope_rows,
                                     batch=bs, seq=ss, past_len=past_len)
            pk.append(k1); pv.append(v1); sk.append(k2); sv.append(v2)
        wc = {"norm_cross": norm_cross_w[l], "ca_w_q": ca_w_q[l].astype(BF16), "ca_w_out": ca_w_out[l].astype(BF16)}
        mkv = norm_matmul(mem, norm_mem_w[l], ca_w_kv[l].astype(BF16))
        mk = mkv[:, :ca_w].reshape(bp, n_mem, ca_w)
        mv = mkv[:, ca_w:].reshape(bp, n_mem, ca_w)
        pmk.append(mk.reshape(bp, n_mem, CA_HEADS, CA_HEAD_DIM))
        pmv.append(mv.reshape(bp, n_mem, CA_HEADS, CA_HEAD_DIM))
        xp = _cross_layer(xp, wc, mk, mv, batch=bp, seq=sp)
        xs = _cross_layer(xs, wc, cache_mem_k[l].reshape(bs, n_mem, ca_w), cache_mem_v[l].reshape(bs, n_mem, ca_w),
                          batch=bs, seq=ss)
    y_prompt = rmsnorm(xp, final_norm_w).reshape(bp, sp, d)
    y_sample = rmsnorm(xs, final_norm_w).reshape(bs, ss, d)
    return (y_prompt, y_sample,
            jnp.stack(pk), jnp.stack(pv), jnp.stack(pconv), jnp.stack(pssm), jnp.stack(psc),
            jnp.stack(pmk), jnp.stack(pmv),
            jnp.stack(sk), jnp.stack(sv), jnp.stack(sconv), jnp.stack(sssm), jnp.stack(ssc))
```

```python
import functools
import math

import numpy as np
import jax
import jax.numpy as jnp
from jax import lax
from jax.experimental import pallas as pl
from jax.experimental.pallas import tpu as pltpu

F32 = jnp.float32
BF16 = jnp.bfloat16
HIGHEST = lax.Precision.HIGHEST

NORM_EPS = 1e-5
NEG_INF = -1e30

LANES = 128
SUBLANES = 8
VMEM_LIMIT = 48 * 1024 * 1024

SSD_HEADDIM = 64
SSD_GROUPS = 4
SSD_STATE = 128
SSD_CONV = 4
SSD_CHUNK = 128
SC_WIDTH = 3
ATT_HEADS = 16
KV_HEADS = 4
HEAD_DIM = 128
Q_PER_KV = ATT_HEADS // KV_HEADS
MOBA_BLOCK = 256
MOBA_TOPK = 3
PAGE_SIZE = 128
PAGES_PER_BLOCK = MOBA_BLOCK // PAGE_SIZE
PAGE_ROWS = PAGE_SIZE * KV_HEADS
BLOCKS_PER_STEP = 4
PAGES_PER_STEP = BLOCKS_PER_STEP * PAGES_PER_BLOCK
CA_HEADS = 4
CA_HEAD_DIM = 128


def _cparams(*sem):
    return pltpu.CompilerParams(dimension_semantics=sem, vmem_limit_bytes=VMEM_LIMIT)


def _tile(n, pref):
    if n <= pref:
        return n
    t = pref
    while n % t:
        t //= 2
    return t


def _silu(x):
    return x / (1.0 + jnp.exp(-x))


def _dot(a, b, **kw):
    return jnp.dot(a, b, preferred_element_type=F32, **kw)


def _dot_nt(a, b, **kw):
    return lax.dot_general(a, b, (((1,), (1,)), ((), ())), preferred_element_type=F32, **kw)


def _norm_mm_kernel(x_ref, nw_ref, w_ref, o_ref, xn_ref):
    @pl.when(pl.program_id(1) == 0)
    def _():
        x = x_ref[...]
        ms = jnp.mean(x * x, axis=-1, keepdims=True)
        xn_ref[...] = ((x * lax.rsqrt(ms + NORM_EPS)) * nw_ref[...]).astype(BF16)

    o_ref[...] = _dot(xn_ref[...], w_ref[...]).astype(o_ref.dtype)


def norm_matmul(x, norm_w, w, *, tm=1024, tn=1024, out_dtype=F32):
    m, k = x.shape
    n = w.shape[1]
    tm, tn = _tile(m, tm), _tile(n, tn)
    return pl.pallas_call(
        _norm_mm_kernel,
        out_shape=jax.ShapeDtypeStruct((m, n), out_dtype),
        grid=(m // tm, n // tn),
        in_specs=[pl.BlockSpec((tm, k), lambda i, j: (i, 0)),
                  pl.BlockSpec((1, k), lambda i, j: (0, 0)),
                  pl.BlockSpec((k, tn), lambda i, j: (0, j))],
        out_specs=pl.BlockSpec((tm, tn), lambda i, j: (i, j)),
        scratch_shapes=[pltpu.VMEM((tm, k), BF16)],
        compiler_params=_cparams("parallel", "arbitrary"),
        name="norm_matmul",
    )(x, norm_w.reshape(1, k), w)


def _mm_res_kernel(*refs, n_pairs):
    res_ref, o_ref = refs[2 * n_pairs], refs[2 * n_pairs + 1]
    acc = res_ref[...]
    for p in range(n_pairs):
        acc = acc + _dot(refs[2 * p][...], refs[2 * p + 1][...])
    o_ref[...] = acc


def matmul_residual(pairs, res, *, tm=512, tn=1024):
    m, n = res.shape
    tm, tn = _tile(m, tm), _tile(n, tn)
    in_specs, args = [], []
    for a, w in pairs:
        k = a.shape[1]
        in_specs += [pl.BlockSpec((tm, k), lambda i, j: (i, 0)),
                     pl.BlockSpec((k, tn), lambda i, j: (0, j))]
        args += [a, w]
    in_specs.append(pl.BlockSpec((tm, tn), lambda i, j: (i, j)))
    return pl.pallas_call(
        functools.partial(_mm_res_kernel, n_pairs=len(pairs)),
        out_shape=jax.ShapeDtypeStruct((m, n), F32),
        grid=(m // tm, n // tn),
        in_specs=in_specs,
        out_specs=pl.BlockSpec((tm, tn), lambda i, j: (i, j)),
        compiler_params=_cparams("parallel", "parallel"),
        name="matmul_residual",
    )(*args, res)


def _rmsnorm_kernel(x_ref, nw_ref, o_ref):
    x = x_ref[...]
    ms = jnp.mean(x * x, axis=-1, keepdims=True)
    o_ref[...] = (x * lax.rsqrt(ms + NORM_EPS)) * nw_ref[...]


def rmsnorm(x, norm_w, *, tm=512):
    m, k = x.shape
    tm = _tile(m, tm)
    return pl.pallas_call(
        _rmsnorm_kernel,
        out_shape=jax.ShapeDtypeStruct((m, k), F32),
        grid=(m // tm,),
        in_specs=[pl.BlockSpec((tm, k), lambda i: (i, 0)),
                  pl.BlockSpec((1, k), lambda i: (0, 0))],
        out_specs=pl.BlockSpec((tm, k), lambda i: (i, 0)),
        compiler_params=_cparams("parallel"),
        name="rmsnorm",
    )(x, norm_w.reshape(1, k))


def _causal_conv_chunk(x_ref, buf_ref, cw_ref, cb_ref, q):
    x = x_ref[...]
    buf_ref[SUBLANES:SUBLANES + q, :] = x
    y = cw_ref[SSD_CONV - 1:SSD_CONV, :] * x
    for k in range(1, SSD_CONV):
        y = y + cw_ref[SSD_CONV - 1 - k:SSD_CONV - k, :] * buf_ref[SUBLANES - k:SUBLANES - k + q, :]
    if cb_ref is not None:
        y = y + cb_ref[...]
    buf_ref[0:SUBLANES, :] = x[q - SUBLANES:q, :]
    return y


def _ssd_kernel(z_ref, xs_ref, b_ref, c_ref, dt_ref, cs_xs_ref, cs_b_ref, cs_c_ref, h0_ref,
                cw_xs_ref, cw_b_ref, cw_c_ref, cb_xs_ref, cb_b_ref, cb_c_ref,
                dtb_ref, alog_ref, dskip_ref, nw_ref, e_ref,
                y_ref, hfin_ref,
                ht_ref, buf_xs, buf_b, buf_c, *, q, n_valid, n_chunks):
    c = pl.program_id(1)
    n_heads = e_ref.shape[1] // SSD_HEADDIM
    gw = (n_heads // SSD_GROUPS) * SSD_HEADDIM

    @pl.when(c == 0)
    def _():
        ht_ref[...] = h0_ref[...].T
        buf_xs[0:SUBLANES, :] = cs_xs_ref[...]
        buf_b[0:SUBLANES, :] = cs_b_ref[...]
        buf_c[0:SUBLANES, :] = cs_c_ref[...]

    xs = _silu(_causal_conv_chunk(xs_ref, buf_xs, cw_xs_ref, cb_xs_ref, q))
    bm = _silu(_causal_conv_chunk(b_ref, buf_b, cw_b_ref, cb_b_ref, q))
    cm = _silu(_causal_conv_chunk(c_ref, buf_c, cw_c_ref, cb_c_ref, q))

    row = lax.broadcasted_iota(jnp.int32, (q, q), 0)
    col = lax.broadcasted_iota(jnp.int32, (q, q), 1)
    tril = row >= col

    dtv = dt_ref[...] + dtb_ref[...]
    dt = jnp.maximum(dtv, 0.0) + jnp.log(1.0 + jnp.exp(-jnp.abs(dtv)))
    if n_valid < q:
        dt = jnp.where(lax.broadcasted_iota(jnp.int32, dt.shape, 0) < n_valid, dt, 0.0)
    a = dt * (-jnp.exp(alog_ref[...]))
    a_cs = _dot(tril.astype(F32), a, precision=HIGHEST)
    a_cs_t = a_cs.T
    expand = e_ref[...]
    dt_full = _dot(dt, expand, precision=HIGHEST)
    acs_full = _dot(a_cs, expand, precision=HIGHEST)
    tot_full = acs_full[q - 1:q, :]
    xr = xs * dt_full
    xr_dec = (xr * jnp.exp(tot_full - acs_full)).astype(BF16)
    exp_acs = jnp.exp(acs_full)
    exp_tot = jnp.exp(tot_full)

    lane = lax.broadcasted_iota(jnp.int32, (q, LANES), 1)
    lo_half = lane < SSD_HEADDIM
    heads_per_tile = LANES // SSD_HEADDIM

    y = dskip_ref[...] * xs
    y_parts = []
    for g in range(SSD_GROUPS):
        bg = bm[:, g * SSD_STATE:(g + 1) * SSD_STATE]
        cg = cm[:, g * SSD_STATE:(g + 1) * SSD_STATE].astype(BF16)
        bg_t = bg.T.astype(BF16)
        cb = _dot(cg, bg_t)
        ht_g = ht_ref[:, g * gw:(g + 1) * gw]
        y_g = _dot(cg, ht_g.astype(BF16)) * exp_acs[:, g * gw:(g + 1) * gw]
        tiles = []
        for t in range(gw // LANES):
            base = g * gw + t * LANES
            xr_t = xr[:, base:base + LANES]
            acc = None
            for e in range(heads_per_tile):
                h = base // SSD_HEADDIM + e
                seg = a_cs[:, h:h + 1] - a_cs_t[h:h + 1, :]
                lmat = jnp.where(tril, jnp.exp(seg), 0.0)
                in_head = lo_half if e == 0 else jnp.logical_not(lo_half)
                part = _dot((cb * lmat).astype(BF16), jnp.where(in_head, xr_t, 0.0).astype(BF16))
                acc = part if acc is None else acc + part
            tiles.append(acc)
        y_g = y_g + jnp.concatenate(tiles, axis=1)
        states = _dot(bg_t, xr_dec[:, g * gw:(g + 1) * gw])
        ht_ref[:, g * gw:(g + 1) * gw] = exp_tot[:, g * gw:(g + 1) * gw] * ht_g + states
        y_parts.append(y_g)
    y = y + jnp.concatenate(y_parts, axis=1)

    y = y * _silu(z_ref[...])
    for g in range(SSD_GROUPS):
        yg = y[:, g * gw:(g + 1) * gw]
        ms = jnp.mean(yg * yg, axis=-1, keepdims=True)
        y_ref[:, g * gw:(g + 1) * gw] = ((yg * lax.rsqrt(ms + NORM_EPS)) * nw_ref[:, g * gw:(g + 1) * gw]).astype(y_ref.dtype)

    @pl.when(c == n_chunks - 1)
    def _():
        hfin_ref[...] = ht_ref[...].T


def ssd_mixer(proj, dt_raw, conv_state8, h0, wts, *, batch, seq, n_valid):
    q = min(SSD_CHUNK, seq)
    assert seq % q == 0 and q % SUBLANES == 0
    nc = seq // q
    assert n_valid == q or nc == 1
    inner = h0.shape[1]
    gn = SSD_GROUPS * SSD_STATE
    assert inner % gn == 0 and (inner // gn) * gn == inner
    kb = inner // gn
    rows = lambda b, c: b * nc + c
    full = lambda b, c: (0, 0)
    in_specs = [
        pl.BlockSpec((q, inner), lambda b, c: (rows(b, c), 0)),
        pl.BlockSpec((q, inner), lambda b, c: (rows(b, c), 1)),
        pl.BlockSpec((q, gn), lambda b, c: (rows(b, c), 2 * kb)),
        pl.BlockSpec((q, gn), lambda b, c: (rows(b, c), 2 * kb + 1)),
        pl.BlockSpec((q, LANES), lambda b, c: (rows(b, c), 0)),
        pl.BlockSpec((None, SUBLANES, inner), lambda b, c: (b, 0, 0)),
        pl.BlockSpec((None, SUBLANES, gn), lambda b, c: (b, 0, kb)),
        pl.BlockSpec((None, SUBLANES, gn), lambda b, c: (b, 0, kb + 1)),
        pl.BlockSpec((None, inner, SSD_STATE), lambda b, c: (b, 0, 0)),
        pl.BlockSpec((SSD_CONV, inner), full),
        pl.BlockSpec((SSD_CONV, gn), lambda b, c: (0, kb)),
        pl.BlockSpec((SSD_CONV, gn), lambda b, c: (0, kb + 1)),
        pl.BlockSpec((1, inner), full),
        pl.BlockSpec((1, gn), lambda b, c: (0, kb)),
        pl.BlockSpec((1, gn), lambda b, c: (0, kb + 1)),
        pl.BlockSpec((1, LANES), full),
        pl.BlockSpec((1, LANES), full),
        pl.BlockSpec((1, inner), full),
        pl.BlockSpec((1, inner), full),
        pl.BlockSpec((LANES, inner), full),
    ]
    cw, cb = wts["conv_w"], wts["conv_b"]
    y, hfin = pl.pallas_call(
        functools.partial(_ssd_kernel, q=q, n_valid=n_valid, n_chunks=nc),
        out_shape=(jax.ShapeDtypeStruct((batch * seq, inner), BF16),
                   jax.ShapeDtypeStruct((batch, inner, SSD_STATE), F32)),
        grid=(batch, nc),
        in_specs=in_specs,
        out_specs=(pl.BlockSpec((q, inner), lambda b, c: (rows(b, c), 0)),
                   pl.BlockSpec((None, inner, SSD_STATE), lambda b, c: (b, 0, 0))),
        scratch_shapes=[pltpu.VMEM((SSD_STATE, inner), F32),
                        pltpu.VMEM((q + SUBLANES, inner), F32),
                        pltpu.VMEM((q + SUBLANES, gn), F32),
                        pltpu.VMEM((q + SUBLANES, gn), F32)],
        compiler_params=_cparams("parallel", "arbitrary"),
        name="ssd_mixer",
    )(proj, proj, proj, proj, dt_raw, conv_state8, conv_state8, conv_state8, h0,
      cw, cw, cw, cb, cb, cb, wts["dt_bias"], wts["a_log"], wts["d_skip"], wts["norm_w"], wts["expand"])
    return y, hfin


def _short_conv_kernel(b_ref, c_ref, x_ref, g_ref, st_ref, w_ref, y_ref, last_ref, buf_ref, *, tq):
    @pl.when(pl.program_id(2) == 0)
    def _():
        buf_ref[0:SUBLANES, :] = st_ref[...]

    prod = c_ref[...] * x_ref[...]
    buf_ref[SUBLANES:SUBLANES + tq, :] = prod
    u = w_ref[SC_WIDTH - 1:SC_WIDTH, :] * prod
    for k in range(1, SC_WIDTH):
        u = u + w_ref[SC_WIDTH - 1 - k:SC_WIDTH - k, :] * buf_ref[SUBLANES - k:SUBLANES - k + tq, :]
    tail = prod[tq - SUBLANES:tq, :]
    buf_ref[0:SUBLANES, :] = tail
    last_ref[...] = tail
    y_ref[...] = (b_ref[...] * u * _silu(g_ref[...])).astype(y_ref.dtype)


def short_conv(proj, state8, w, *, batch, seq, dim, col0, tq=256, tc=1024):
    tq = _tile(seq, tq)
    nt = seq // tq
    ncb = dim // tc
    assert col0 % tc == 0 and dim % tc == 0 and tq % SUBLANES == 0
    cb0 = col0 // tc

    def part(k):
        return pl.BlockSpec((tq, tc), lambda b, j, t: (b * nt + t, cb0 + k * ncb + j))

    return pl.pallas_call(
        functools.partial(_short_conv_kernel, tq=tq),
        out_shape=(jax.ShapeDtypeStruct((batch * seq, dim), BF16),
                   jax.ShapeDtypeStruct((batch, SUBLANES, dim), F32)),
        grid=(batch, ncb, nt),
        in_specs=[part(0), part(1), part(2), part(3),
                  pl.BlockSpec((None, SUBLANES, tc), lambda b, j, t: (b, 0, j)),
                  pl.BlockSpec((SC_WIDTH, tc), lambda b, j, t: (0, j))],
        out_specs=(pl.BlockSpec((tq, tc), lambda b, j, t: (b * nt + t, j)),
                   pl.BlockSpec((None, SUBLANES, tc), lambda b, j, t: (b, 0, j))),
        scratch_shapes=[pltpu.VMEM((tq + SUBLANES, tc), F32)],
        compiler_params=_cparams("parallel", "parallel", "arbitrary"),
        name="short_conv",
    )(proj, proj, proj, proj, state8, w)


def _top_blocks(gate, n_valid, axis=1):
    idx = lax.broadcasted_iota(jnp.int32, gate.shape, axis).astype(F32)
    g = jnp.where(idx < jnp.asarray(n_valid, F32), gate, NEG_INF)
    sel = jnp.zeros(gate.shape, F32)
    for _ in range(MOBA_TOPK):
        m = jnp.max(g, axis=axis, keepdims=True)
        first = jnp.min(jnp.where(g == m, idx, float(gate.shape[axis])), axis=axis, keepdims=True)
        pick = idx == first
        sel = jnp.where(pick & (m > 0.5 * NEG_INF), 1.0, sel)
        g = jnp.where(pick, -jnp.inf, g)
    return sel


def _moba_prompt_kernel(slopes_ref, q_ref, k_ref, v_ref, g_ref, o_ref, kmean_ref, kb_ref, vt_ref, mval_ref, *, n_blocks):
    kv = pl.program_id(1)
    i = pl.program_id(2)
    blk = MOBA_BLOCK
    heads = range(Q_PER_KV)

    @pl.when(i == 0)
    def _():
        kmean_ref[...] = jnp.zeros(kmean_ref.shape, F32)
        for j in range(n_blocks):
            kj = k_ref[j * blk:(j + 1) * blk, :]
            kmean_ref[j:j + 1, :] = jnp.sum(kj, axis=0, keepdims=True) * (1.0 / blk)
            kb_ref[j] = kj.astype(BF16)
            vt_ref[j] = v_ref[j * blk:(j + 1) * blk, :].T.astype(BF16)

    rel = (lax.broadcasted_iota(jnp.int32, (blk, blk), 1)
           - lax.broadcasted_iota(jnp.int32, (blk, blk), 0)).astype(F32)
    blk_idx = lax.broadcasted_iota(jnp.int32, (kmean_ref.shape[0], blk), 0)
    back = (i - blk_idx).astype(F32) * float(blk)
    kmean = kmean_ref[...]

    qt, bias, diag = [], [], []
    for g in heads:
        slope = slopes_ref[kv * Q_PER_KV + g]
        q_t = (q_ref[:, g * HEAD_DIM:(g + 1) * HEAD_DIM] * (HEAD_DIM ** -0.5)).T
        gate = _dot(kmean, q_t, precision=HIGHEST)
        sel = _top_blocks(gate, i, axis=0)
        mval_ref[:, g * blk:(g + 1) * blk] = jnp.where(sel > 0.0, -slope * back, NEG_INF)
        bias.append(-slope * rel)
        diag.append(jnp.where(rel >= 0, -slope * rel, NEG_INF))
        qt.append(q_t.astype(BF16))
    qt = jnp.concatenate(qt, axis=1)
    bias = jnp.concatenate(bias, axis=1)

    s = _dot(kb_ref[i], qt) + jnp.concatenate(diag, axis=1)
    m0 = jnp.max(s, axis=0, keepdims=True)
    p = jnp.exp(s - m0)
    l0 = jnp.sum(p, axis=0, keepdims=True)
    acc0 = _dot(vt_ref[i], p.astype(BF16))

    def past_block(j, carry):
        m, l, acc = carry
        s = _dot(kb_ref[j], qt) + bias + mval_ref[pl.ds(j, 1), :]
        m_new = jnp.maximum(m, jnp.max(s, axis=0, keepdims=True))
        alpha = jnp.exp(m - m_new)
        p = jnp.exp(s - m_new)
        l = alpha * l + jnp.sum(p, axis=0, keepdims=True)
        acc = alpha * acc + _dot(vt_ref[j], p.astype(BF16))
        return m_new, l, acc

    _, l, acc = lax.fori_loop(0, i, past_block, (m0, l0, acc0))
    o_t = acc / l
    for g in heads:
        c = slice(g * HEAD_DIM, (g + 1) * HEAD_DIM)
        o_ref[:, c] = (o_t[:, g * blk:(g + 1) * blk].T * _silu(g_ref[:, c])).astype(o_ref.dtype)


def moba_prompt(proj, slopes, *, batch, seq):
    assert seq % MOBA_BLOCK == 0
    nb = seq // MOBA_BLOCK
    nbp = -(-nb // SUBLANES) * SUBLANES
    gw = Q_PER_KV * HEAD_DIM
    kcol = ATT_HEADS
    vcol = ATT_HEADS + KV_HEADS
    gcol = (ATT_HEADS + 2 * KV_HEADS) * HEAD_DIM // gw
    assert gcol * gw == (ATT_HEADS + 2 * KV_HEADS) * HEAD_DIM
    return pl.pallas_call(
        functools.partial(_moba_prompt_kernel, n_blocks=nb),
        out_shape=jax.ShapeDtypeStruct((batch * seq, ATT_HEADS * HEAD_DIM), BF16),
        grid=(batch, KV_HEADS, nb),
        in_specs=[pl.BlockSpec(memory_space=pltpu.SMEM),
                  pl.BlockSpec((MOBA_BLOCK, gw), lambda b, kv, i: (b * nb + i, kv)),
                  pl.BlockSpec((seq, HEAD_DIM), lambda b, kv, i: (b, kcol + kv)),
                  pl.BlockSpec((seq, HEAD_DIM), lambda b, kv, i: (b, vcol + kv)),
                  pl.BlockSpec((MOBA_BLOCK, gw), lambda b, kv, i: (b * nb + i, gcol + kv))],
        out_specs=pl.BlockSpec((MOBA_BLOCK, gw), lambda b, kv, i: (b * nb + i, kv)),
        scratch_shapes=[pltpu.VMEM((nbp, HEAD_DIM), F32),
                        pltpu.VMEM((nb, MOBA_BLOCK, HEAD_DIM), BF16),
                        pltpu.VMEM((nb, HEAD_DIM, MOBA_BLOCK), BF16),
                        pltpu.VMEM((nbp, Q_PER_KV * MOBA_BLOCK), F32)],
        compiler_params=_cparams("parallel", "parallel", "arbitrary"),
        name="moba_prompt",
    )(slopes, proj, proj, proj, proj)


def _page_specs(layer):
    return [pl.BlockSpec((None, None, PAGE_ROWS, HEAD_DIM),
                         lambda b, s, pt, p=p: (layer, pt[b, s * PAGES_PER_STEP + p], 0, 0))
            for p in range(PAGES_PER_STEP)]


def _kmean_kernel(pt_ref, *refs):
    del pt_ref
    pages, o_ref = refs[:PAGES_PER_STEP], refs[PAGES_PER_STEP]
    for bb in range(BLOCKS_PER_STEP):
        tot = None
        for pg in range(PAGES_PER_BLOCK):
            page = pages[bb * PAGES_PER_BLOCK + pg][...]
            part = jnp.sum(page.reshape(PAGE_ROWS // SUBLANES, SUBLANES, HEAD_DIM), axis=0)
            tot = part if tot is None else tot + part
        per_kv = tot[0:KV_HEADS]
        for par in range(1, SUBLANES // KV_HEADS):
            per_kv = per_kv + tot[par * KV_HEADS:(par + 1) * KV_HEADS]
        o_ref[bb] = per_kv * (1.0 / MOBA_BLOCK)


def paged_block_means(k_pools, page_table, *, layer, n_blocks):
    db = page_table.shape[0]
    assert n_blocks % BLOCKS_PER_STEP == 0 and SUBLANES % KV_HEADS == 0
    return pl.pallas_call(
        _kmean_kernel,
        out_shape=jax.ShapeDtypeStruct((db, n_blocks, KV_HEADS, HEAD_DIM), F32),
        grid_spec=pltpu.PrefetchScalarGridSpec(
            num_scalar_prefetch=1,
            grid=(db, n_blocks // BLOCKS_PER_STEP),
            in_specs=_page_specs(layer),
            out_specs=pl.BlockSpec((None, BLOCKS_PER_STEP, KV_HEADS, HEAD_DIM), lambda b, s, pt: (b, s, 0, 0))),
        compiler_params=_cparams("parallel", "arbitrary"),
        name="paged_block_means",
    )(page_table, *([k_pools] * PAGES_PER_STEP))


def _moba_sample_kernel(pt_ref, q_ref, kmean_ref, slope_ref, knew_ref, vnew_ref, *refs,
                        n_blocks, past_len, dec_seq):
    del pt_ref
    k_pages, v_pages = refs[:PAGES_PER_STEP], refs[PAGES_PER_STEP:2 * PAGES_PER_STEP]
    o_ref, m_ref, l_ref, acc_ref, sel_ref, base_ref = refs[2 * PAGES_PER_STEP:]
    step = pl.program_id(1)
    rows = q_ref.shape[0]
    rpk = rows // KV_HEADS
    kv_shift = KV_HEADS.bit_length() - 1
    q = q_ref[...] * (HEAD_DIM ** -0.5)
    qb = q.astype(BF16)
    slope = slope_ref[:, 0:1]
    lane = lax.broadcasted_iota(jnp.int32, (rows, LANES), 1)

    def key_bias(width, q_pos0, causal):
        r = lax.broadcasted_iota(jnp.int32, (rows, width), 0)
        c = lax.broadcasted_iota(jnp.int32, (rows, width), 1)
        tok = jnp.right_shift(c, kv_shift)
        t_row = jnp.bitwise_and(r, dec_seq - 1)
        same_head = jnp.bitwise_and(c, KV_HEADS - 1) == jnp.right_shift(r, rpk.bit_length() - 1)
        dist = (q_pos0 + t_row - tok).astype(F32)
        ok = same_head & (dist >= 0) if causal else same_head
        return jnp.where(ok, -slope * dist, NEG_INF)

    @pl.when(step == 0)
    def _():
        gate = jnp.concatenate([_dot_nt(q[kv * rpk:(kv + 1) * rpk, :], kmean_ref[kv], precision=HIGHEST)
                                for kv in range(KV_HEADS)], axis=0)
        sel_ref[...] = _top_blocks(gate, n_blocks)
        base_ref[...] = key_bias(PAGE_ROWS, past_len, causal=False)
        s = _dot_nt(qb, knew_ref[...].astype(BF16)) + key_bias(LANES, 0, causal=True)
        m = jnp.max(s, axis=1, keepdims=True)
        p = jnp.exp(s - m)
        m_ref[...] = jnp.broadcast_to(m, m_ref.shape)
        l_ref[...] = jnp.broadcast_to(jnp.sum(p, axis=1, keepdims=True), l_ref.shape)
        acc_ref[...] = _dot(p.astype(BF16), vnew_ref[...].astype(BF16))

    sel = sel_ref[...]
    base = base_ref[...]
    m, l, acc = m_ref[:, 0:1], l_ref[:, 0:1], acc_ref[...]
    for bb in range(BLOCKS_PER_STEP):
        blk_id = step * BLOCKS_PER_STEP + bb
        chosen = jnp.sum(jnp.where(lane == blk_id, sel, 0.0), axis=1, keepdims=True) > 0.0
        s_pages = []
        for pg in range(PAGES_PER_BLOCK):
            page_pos = (blk_id * MOBA_BLOCK + pg * PAGE_SIZE).astype(F32)
            mcol = jnp.where(chosen, slope * page_pos, NEG_INF)
            s_pages.append(_dot_nt(qb, k_pages[bb * PAGES_PER_BLOCK + pg][...].astype(BF16)) + base + mcol)
        m_new = m
        for s in s_pages:
            m_new = jnp.maximum(m_new, jnp.max(s, axis=1, keepdims=True))
        alpha = jnp.exp(m - m_new)
        l, acc, m = alpha * l, alpha * acc, m_new
        for pg, s in enumerate(s_pages):
            p = jnp.exp(s - m)
            l = l + jnp.sum(p, axis=1, keepdims=True)
            acc = acc + _dot(p.astype(BF16), v_pages[bb * PAGES_PER_BLOCK + pg][...].astype(BF16))
    m_ref[...] = jnp.broadcast_to(m, m_ref.shape)
    l_ref[...] = jnp.broadcast_to(l, l_ref.shape)
    acc_ref[...] = acc

    @pl.when(step == n_blocks // BLOCKS_PER_STEP - 1)
    def _():
        o_ref[...] = acc / l


def moba_sample(q_rows, kmean_t, slope_rows, k_new_pad, v_new_pad, k_pools, v_pools, page_table,
                *, layer, n_blocks, past_len, dec_seq):
    db, rows, _ = q_rows.shape
    rpk = rows // KV_HEADS
    for n in (KV_HEADS, dec_seq, rpk):
        assert n & (n - 1) == 0
    assert n_blocks % BLOCKS_PER_STEP == 0 and dec_seq * KV_HEADS <= LANES
    per_b = lambda *shape: pl.BlockSpec((None,) + shape, lambda b, s, pt: (b,) + (0,) * len(shape))
    return pl.pallas_call(
        functools.partial(_moba_sample_kernel, n_blocks=n_blocks, past_len=past_len, dec_seq=dec_seq),
        out_shape=jax.ShapeDtypeStruct((db, rows, HEAD_DIM), F32),
        grid_spec=pltpu.PrefetchScalarGridSpec(
            num_scalar_prefetch=1,
            grid=(db, n_blocks // BLOCKS_PER_STEP),
            in_specs=[per_b(rows, HEAD_DIM),
                      per_b(KV_HEADS, LANES, HEAD_DIM),
                      pl.BlockSpec((rows, LANES), lambda b, s, pt: (0, 0)),
                      per_b(LANES, HEAD_DIM),
                      per_b(LANES, HEAD_DIM)] + _page_specs(layer) + _page_specs(layer),
            out_specs=per_b(rows, HEAD_DIM),
            scratch_shapes=[pltpu.VMEM((rows, LANES), F32), pltpu.VMEM((rows, LANES), F32),
                            pltpu.VMEM((rows, HEAD_DIM), F32), pltpu.VMEM((rows, LANES), F32),
                            pltpu.VMEM((rows, PAGE_ROWS), F32)]),
        compiler_params=_cparams("parallel", "arbitrary"),
        name="moba_sample",
    )(page_table, q_rows, kmean_t, slope_rows, k_new_pad, v_new_pad,
      *([k_pools] * PAGES_PER_STEP), *([v_pools] * PAGES_PER_STEP))


def _gate_kernel(o_ref, g_ref, y_ref):
    y_ref[...] = (o_ref[...] * _silu(g_ref[...])).astype(y_ref.dtype)


def silu_gate(o, proj, *, col0, tn=1024):
    m, n = o.shape
    assert col0 % tn == 0 and n % tn == 0
    cb = col0 // tn
    return pl.pallas_call(
        _gate_kernel,
        out_shape=jax.ShapeDtypeStruct((m, n), BF16),
        grid=(n // tn,),
        in_specs=[pl.BlockSpec((m, tn), lambda j: (0, j)),
                  pl.BlockSpec((m, tn), lambda j: (0, cb + j))],
        out_specs=pl.BlockSpec((m, tn), lambda j: (0, j)),
        compiler_params=_cparams("parallel"),
        name="silu_gate",
    )(o, proj)


def _cross_attn_kernel(qg_ref, k_ref, v_ref, o_ref):
    width = CA_HEADS * CA_HEAD_DIM
    for h in range(CA_HEADS):
        c = slice(h * CA_HEAD_DIM, (h + 1) * CA_HEAD_DIM)
        q = (qg_ref[:, c] * (CA_HEAD_DIM ** -0.5)).astype(BF16)
        s = _dot_nt(q, k_ref[:, c].astype(BF16))
        m = jnp.max(s, axis=1, keepdims=True)
        p = jnp.exp(s - m)
        l = jnp.sum(p, axis=1, keepdims=True)
        o = _dot(p.astype(BF16), v_ref[:, c].astype(BF16)) / l
        g = qg_ref[:, width + h * CA_HEAD_DIM:width + (h + 1) * CA_HEAD_DIM]
        o_ref[:, c] = (o * _silu(g)).astype(o_ref.dtype)


def cross_attention(qg, mem_k, mem_v, *, batch, seq, tm=256):
    width = CA_HEADS * CA_HEAD_DIM
    n_mem = mem_k.shape[1]
    tm = _tile(seq, tm)
    nt = seq // tm
    return pl.pallas_call(
        _cross_attn_kernel,
        out_shape=jax.ShapeDtypeStruct((batch * seq, width), BF16),
        grid=(batch * nt,),
        in_specs=[pl.BlockSpec((tm, 2 * width), lambda i: (i, 0)),
                  pl.BlockSpec((None, n_mem, width), lambda i: (i // nt, 0, 0)),
                  pl.BlockSpec((None, n_mem, width), lambda i: (i // nt, 0, 0))],
        out_specs=pl.BlockSpec((tm, width), lambda i: (i, 0)),
        compiler_params=_cparams("parallel"),
        name="cross_attention",
    )(qg, mem_k, mem_v)


def _alibi_slopes():
    return np.array([2.0 ** (-8.0 * (h + 1) / ATT_HEADS) for h in range(ATT_HEADS)], dtype=np.float32)


def _pad_rows(a, n, front=False):
    extra = n - a.shape[-2]
    pad = [(0, 0)] * a.ndim
    pad[-2] = (extra, 0) if front else (0, extra)
    return jnp.pad(a, pad)


def _pad_lanes(a):
    pad = [(0, 0)] * a.ndim
    pad[-1] = (0, LANES - a.shape[-1])
    return jnp.pad(a, pad)


def _even_layer(x, w, conv_state, ssm_state, sc_state, *, batch, seq):
    d = x.shape[1]
    inner = d
    conv_ch = inner + 2 * SSD_GROUPS * SSD_STATE
    proj = norm_matmul(x, w["norm_mix"], w["w_in"])
    dt_raw = norm_matmul(x, w["norm_mix"], w["w_dt"])
    new_conv = proj.reshape(batch, seq, -1)[:, seq - (SSD_CONV - 1):, inner:inner + conv_ch]

    lpad = -(-seq // SSD_CHUNK) * SSD_CHUNK if seq > SSD_CHUNK else max(seq, SSD_CHUNK)
    if lpad != seq:
        assert seq < SSD_CHUNK
        ssd_in = _pad_rows(proj.reshape(batch, seq, -1)[:, :, :inner + conv_ch], lpad).reshape(batch * lpad, -1)
        dt_in = _pad_rows(dt_raw.reshape(batch, seq, -1), lpad).reshape(batch * lpad, -1)
    else:
        ssd_in, dt_in = proj, dt_raw
    y, h_fin = ssd_mixer(ssd_in, dt_in, _pad_rows(conv_state, SUBLANES, front=True),
                         ssm_state.reshape(batch, inner, SSD_STATE), w,
                         batch=batch, seq=lpad, n_valid=min(seq, SSD_CHUNK))
    if lpad != seq:
        y = y.reshape(batch, lpad, inner)[:, :seq].reshape(batch * seq, inner)

    y_sc, sc_tail = short_conv(proj, _pad_rows(sc_state, SUBLANES, front=True), w["sc_w"],
                               batch=batch, seq=seq, dim=d, col0=inner + conv_ch)
    x = matmul_residual([(y, w["w_out_ssd"]), (y_sc, w["w_out_sc"])], x)
    return (x, new_conv, h_fin.reshape(batch, inner // SSD_HEADDIM, SSD_HEADDIM, SSD_STATE),
            sc_tail[:, SUBLANES - (SC_WIDTH - 1):, :])


def _odd_prompt(x, w, slopes, *, batch, seq):
    proj = norm_matmul(x, w["norm_mix"], w["w_in"])
    att_q, att_kv = ATT_HEADS * HEAD_DIM, KV_HEADS * HEAD_DIM
    k = proj[:, att_q:att_q + att_kv].reshape(batch, seq, KV_HEADS, HEAD_DIM)
    v = proj[:, att_q + att_kv:att_q + 2 * att_kv].reshape(batch, seq, KV_HEADS, HEAD_DIM)
    og = moba_prompt(proj, slopes, batch=batch, seq=seq)
    return matmul_residual([(og, w["w_out"])], x), k, v


def _odd_sample(x, w, k_pools, v_pools, page_table, slope_rows, *, layer, batch, seq, past_len):
    assert past_len % MOBA_BLOCK == 0 and seq <= MOBA_BLOCK and seq <= LANES
    n_blocks = past_len // MOBA_BLOCK
    assert MOBA_TOPK <= n_blocks <= LANES
    proj = norm_matmul(x, w["norm_mix"], w["w_in"])
    att_q, att_kv = ATT_HEADS * HEAD_DIM, KV_HEADS * HEAD_DIM
    k_new = proj[:, att_q:att_q + att_kv]
    v_new = proj[:, att_q + att_kv:att_q + 2 * att_kv]
    q_rows = proj[:, :att_q].reshape(batch, seq, KV_HEADS, Q_PER_KV, HEAD_DIM).transpose(0, 2, 3, 1, 4)
    q_rows = q_rows.reshape(batch, ATT_HEADS * seq, HEAD_DIM)
    kmean = paged_block_means(k_pools, page_table, layer=layer, n_blocks=n_blocks)
    o = moba_sample(q_rows, _pad_rows(kmean.transpose(0, 2, 1, 3), LANES), slope_rows,
                    _pad_rows(k_new.reshape(batch, seq * KV_HEADS, HEAD_DIM), LANES),
                    _pad_rows(v_new.reshape(batch, seq * KV_HEADS, HEAD_DIM), LANES),
                    k_pools, v_pools, page_table, layer=layer, n_blocks=n_blocks, past_len=past_len, dec_seq=seq)
    o = o.reshape(batch, KV_HEADS, Q_PER_KV, seq, HEAD_DIM).transpose(0, 3, 1, 2, 4).reshape(batch * seq, att_q)
    og = silu_gate(o, proj, col0=att_q + 2 * att_kv)
    return (matmul_residual([(og, w["w_out"])], x),
            k_new.reshape(batch, seq, KV_HEADS, HEAD_DIM), v_new.reshape(batch, seq, KV_HEADS, HEAD_DIM))


def _cross_layer(x, w, mem_k, mem_v, *, batch, seq):
    qg = norm_matmul(x, w["norm_cross"], w["ca_w_q"])
    og = cross_attention(qg, mem_k, mem_v, batch=batch, seq=seq)
    return matmul_residual([(og, w["ca_w_out"])], x)


def kernel(x_prompt, x_sample, mem_prompt, cache_k, cache_v, page_table, state_conv, state_ssm, state_sc,
           cache_mem_k, cache_mem_v, norm_mix_w, norm_cross_w, norm_mem_w, final_norm_w,
           ev_w_in, ev_conv_w, ev_conv_b, ev_dt_bias, ev_a_log, ev_d_skip, ev_norm_w, ev_sc_w, ev_w_out,
           od_w_in, od_w_out, ca_w_q, ca_w_kv, ca_w_out):
    bp, sp, d = x_prompt.shape
    bs, ss, _ = x_sample.shape
    depth = norm_mix_w.shape[0]
    n_mem = mem_prompt.shape[1]
    n_heads = ev_dt_bias.shape[1]
    inner = n_heads * SSD_HEADDIM
    conv_ch = inner + 2 * SSD_GROUPS * SSD_STATE
    past_len = page_table.shape[1] * PAGE_SIZE
    ca_w = CA_HEADS * CA_HEAD_DIM
    assert inner == d and n_heads <= LANES

    slopes = jnp.asarray(_alibi_slopes())
    slope_rows = jnp.asarray(np.repeat(_alibi_slopes(), ss)[:, None] * np.ones((1, LANES), np.float32))
    expand_np = np.zeros((LANES, inner), np.float32)
    for hd in range(n_heads):
        expand_np[hd, hd * SSD_HEADDIM:(hd + 1) * SSD_HEADDIM] = 1.0
    expand = jnp.asarray(expand_np)

    xp = x_prompt.reshape(bp * sp, d)
    xs = x_sample.reshape(bs * ss, d)
    mem = mem_prompt.reshape(bp * n_mem, d)
    k_pools = cache_k.reshape(cache_k.shape[0], cache_k.shape[1], PAGE_ROWS, HEAD_DIM)
    v_pools = cache_v.reshape(cache_v.shape[0], cache_v.shape[1], PAGE_ROWS, HEAD_DIM)

    pk, pv, sk, sv = [], [], [], []
    pconv, pssm, psc, sconv, sssm, ssc = [], [], [], [], [], []
    pmk, pmv = [], []
    for l in range(depth):
        i = l // 2
        if l % 2 == 0:
            w_in = ev_w_in[i]
            dt0 = inner + conv_ch
            w = {
                "norm_mix": norm_mix_w[l],
                "w_in": jnp.concatenate([w_in[:, :dt0], w_in[:, dt0 + n_heads:]], axis=1).astype(BF16),
                "w_dt": _pad_lanes(w_in[:, dt0:dt0 + n_heads]).astype(BF16),
                "conv_w": ev_conv_w[i],
                "conv_b": ev_conv_b[i].reshape(1, conv_ch),
                "dt_bias": _pad_lanes(ev_dt_bias[i].reshape(1, n_heads)),
                "a_log": _pad_lanes(ev_a_log[i].reshape(1, n_heads)),
                "d_skip": jnp.repeat(ev_d_skip[i], SSD_HEADDIM).reshape(1, inner),
                "norm_w": ev_norm_w[i].reshape(1, inner),
                "expand": expand,
                "sc_w": ev_sc_w[i],
                "w_out_ssd": ev_w_out[i][:inner].astype(BF16),
                "w_out_sc": ev_w_out[i][inner:].astype(BF16),
            }
            xp, c1, s1, q1 = _even_layer(xp, w, jnp.zeros((bp, SSD_CONV - 1, conv_ch), F32),
                                         jnp.zeros((bp, n_heads, SSD_HEADDIM, SSD_STATE), F32),
                                         jnp.zeros((bp, SC_WIDTH - 1, d), F32), batch=bp, seq=sp)
            xs, c2, s2, q2 = _even_layer(xs, w, state_conv[i], state_ssm[i], state_sc[i], batch=bs, seq=ss)
            pconv.append(c1); pssm.append(s1); psc.append(q1)
            sconv.append(c2); sssm.append(s2); ssc.append(q2)
        else:
            w = {"norm_mix": norm_mix_w[l], "w_in": od_w_in[i].astype(BF16), "w_out": od_w_out[i].astype(BF16)}
            xp, k1, v1 = _odd_prompt(xp, w, slopes, batch=bp, seq=sp)
            xs, k2, v2 = _odd_sample(xs, w, k_pools, v_pools, page_table, slope_rows,
                                     layer=i, batch=bs, seq=ss, past_len=past_len)
            pk.append(k1); pv.append(v1); sk.append(k2); sv.append(v2)
        wc = {"norm_cross": norm_cross_w[l], "ca_w_q": ca_w_q[l].astype(BF16), "ca_w_out": ca_w_out[l].astype(BF16)}
        mkv = norm_matmul(mem, norm_mem_w[l], ca_w_kv[l].astype(BF16))
        mk = mkv[:, :ca_w].reshape(bp, n_mem, ca_w)
        mv = mkv[:, ca_w:].reshape(bp, n_mem, ca_w)
        pmk.append(mk.reshape(bp, n_mem, CA_HEADS, CA_HEAD_DIM))
        pmv.append(mv.reshape(bp, n_mem, CA_HEADS, CA_HEAD_DIM))
        xp = _cross_layer(xp, wc, mk, mv, batch=bp, seq=sp)
        xs = _cross_layer(xs, wc, cache_mem_k[l].reshape(bs, n_mem, ca_w), cache_mem_v[l].reshape(bs, n_mem, ca_w),
                          batch=bs, seq=ss)
    y_prompt = rmsnorm(xp, final_norm_w).reshape(bp, sp, d)
    y_sample = rmsnorm(xs, final_norm_w).reshape(bs, ss, d)
    return (y_prompt, y_sample,
            jnp.stack(pk), jnp.stack(pv), jnp.stack(pconv), jnp.stack(pssm), jnp.stack(psc),
            jnp.stack(pmk), jnp.stack(pmv),
            jnp.stack(sk), jnp.stack(sv), jnp.stack(sconv), jnp.stack(sssm), jnp.stack(ssc))
```

```python
import functools
import math

import numpy as np
import jax
import jax.numpy as jnp
from jax import lax
from jax.experimental import pallas as pl
from jax.experimental.pallas import tpu as pltpu

F32 = jnp.float32
BF16 = jnp.bfloat16
HIGHEST = lax.Precision.HIGHEST

NORM_EPS = 1e-5
NEG_INF = -1e30

LANES = 128
SUBLANES = 8
VMEM_LIMIT = 48 * 1024 * 1024

SSD_HEADDIM = 64
SSD_GROUPS = 4
SSD_STATE = 128
SSD_CONV = 4
SSD_CHUNK = 128
SC_WIDTH = 3
ATT_HEADS = 16
KV_HEADS = 4
HEAD_DIM = 128
Q_PER_KV = ATT_HEADS // KV_HEADS
MOBA_BLOCK = 256
MOBA_TOPK = 3
PAGE_SIZE = 128
PAGES_PER_BLOCK = MOBA_BLOCK // PAGE_SIZE
PAGE_ROWS = PAGE_SIZE * KV_HEADS
BLOCKS_PER_STEP = 8
PAGES_PER_STEP = BLOCKS_PER_STEP * PAGES_PER_BLOCK
CA_HEADS = 4
CA_HEAD_DIM = 128


def _cparams(*sem):
    return pltpu.CompilerParams(dimension_semantics=sem, vmem_limit_bytes=VMEM_LIMIT)


def _tile(n, pref):
    if n <= pref:
        return n
    t = pref
    while n % t:
        t //= 2
    return t


def _silu(x):
    return x / (1.0 + jnp.exp(-x))


def _dot(a, b, **kw):
    return jnp.dot(a, b, preferred_element_type=F32, **kw)


def _dot_nt(a, b, **kw):
    return lax.dot_general(a, b, (((1,), (1,)), ((), ())), preferred_element_type=F32, **kw)


def _norm_mm_kernel(x_ref, nw_ref, w_ref, o_ref, xn_ref):
    @pl.when(pl.program_id(1) == 0)
    def _():
        x = x_ref[...]
        ms = jnp.mean(x * x, axis=-1, keepdims=True)
        xn_ref[...] = ((x * lax.rsqrt(ms + NORM_EPS)) * nw_ref[...]).astype(BF16)

    o_ref[...] = _dot(xn_ref[...], w_ref[...]).astype(o_ref.dtype)


def norm_matmul(x, norm_w, w, *, tm=1024, tn=1024, out_dtype=F32):
    m, k = x.shape
    n = w.shape[1]
    tm, tn = _tile(m, tm), _tile(n, tn)
    return pl.pallas_call(
        _norm_mm_kernel,
        out_shape=jax.ShapeDtypeStruct((m, n), out_dtype),
        grid=(m // tm, n // tn),
        in_specs=[pl.BlockSpec((tm, k), lambda i, j: (i, 0)),
                  pl.BlockSpec((1, k), lambda i, j: (0, 0)),
                  pl.BlockSpec((k, tn), lambda i, j: (0, j))],
        out_specs=pl.BlockSpec((tm, tn), lambda i, j: (i, j)),
        scratch_shapes=[pltpu.VMEM((tm, k), BF16)],
        compiler_params=_cparams("parallel", "arbitrary"),
        name="norm_matmul",
    )(x, norm_w.reshape(1, k), w)


def _mm_res_kernel(*refs, n_pairs):
    res_ref, o_ref = refs[2 * n_pairs], refs[2 * n_pairs + 1]
    acc = res_ref[...]
    for p in range(n_pairs):
        acc = acc + _dot(refs[2 * p][...], refs[2 * p + 1][...])
    o_ref[...] = acc


def matmul_residual(pairs, res, *, tm=512, tn=1024):
    m, n = res.shape
    tm, tn = _tile(m, tm), _tile(n, tn)
    in_specs, args = [], []
    for a, w in pairs:
        k = a.shape[1]
        in_specs += [pl.BlockSpec((tm, k), lambda i, j: (i, 0)),
                     pl.BlockSpec((k, tn), lambda i, j: (0, j))]
        args += [a, w]
    in_specs.append(pl.BlockSpec((tm, tn), lambda i, j: (i, j)))
    return pl.pallas_call(
        functools.partial(_mm_res_kernel, n_pairs=len(pairs)),
        out_shape=jax.ShapeDtypeStruct((m, n), F32),
        grid=(m // tm, n // tn),
        in_specs=in_specs,
        out_specs=pl.BlockSpec((tm, tn), lambda i, j: (i, j)),
        compiler_params=_cparams("parallel", "parallel"),
        name="matmul_residual",
    )(*args, res)


def _rmsnorm_kernel(x_ref, nw_ref, o_ref):
    x = x_ref[...]
    ms = jnp.mean(x * x, axis=-1, keepdims=True)
    o_ref[...] = (x * lax.rsqrt(ms + NORM_EPS)) * nw_ref[...]


def rmsnorm(x, norm_w, *, tm=512):
    m, k = x.shape
    tm = _tile(m, tm)
    return pl.pallas_call(
        _rmsnorm_kernel,
        out_shape=jax.ShapeDtypeStruct((m, k), F32),
        grid=(m // tm,),
        in_specs=[pl.BlockSpec((tm, k), lambda i: (i, 0)),
                  pl.BlockSpec((1, k), lambda i: (0, 0))],
        out_specs=pl.BlockSpec((tm, k), lambda i: (i, 0)),
        compiler_params=_cparams("parallel"),
        name="rmsnorm",
    )(x, norm_w.reshape(1, k))


def _causal_conv_chunk(x_ref, buf_ref, cw_ref, cb_ref, q):
    x = x_ref[...]
    buf_ref[SUBLANES:SUBLANES + q, :] = x
    y = cw_ref[SSD_CONV - 1:SSD_CONV, :] * x
    for k in range(1, SSD_CONV):
        y = y + cw_ref[SSD_CONV - 1 - k:SSD_CONV - k, :] * buf_ref[SUBLANES - k:SUBLANES - k + q, :]
    if cb_ref is not None:
        y = y + cb_ref[...]
    buf_ref[0:SUBLANES, :] = x[q - SUBLANES:q, :]
    return y


def _ssd_kernel(z_ref, xs_ref, b_ref, c_ref, dt_ref, cs_xs_ref, cs_b_ref, cs_c_ref, h0_ref,
                cw_xs_ref, cw_b_ref, cw_c_ref, cb_xs_ref, cb_b_ref, cb_c_ref,
                dtb_ref, alog_ref, dskip_ref, nw_ref, e_ref,
                y_ref, hfin_ref,
                ht_ref, buf_xs, buf_b, buf_c, *, q, n_valid, n_chunks):
    c = pl.program_id(1)
    n_heads = e_ref.shape[1] // SSD_HEADDIM
    gw = (n_heads // SSD_GROUPS) * SSD_HEADDIM

    @pl.when(c == 0)
    def _():
        ht_ref[...] = h0_ref[...].T
        buf_xs[0:SUBLANES, :] = cs_xs_ref[...]
        buf_b[0:SUBLANES, :] = cs_b_ref[...]
        buf_c[0:SUBLANES, :] = cs_c_ref[...]

    xs = _silu(_causal_conv_chunk(xs_ref, buf_xs, cw_xs_ref, cb_xs_ref, q))
    bm = _silu(_causal_conv_chunk(b_ref, buf_b, cw_b_ref, cb_b_ref, q))
    cm = _silu(_causal_conv_chunk(c_ref, buf_c, cw_c_ref, cb_c_ref, q))

    row = lax.broadcasted_iota(jnp.int32, (q, q), 0)
    col = lax.broadcasted_iota(jnp.int32, (q, q), 1)
    tril = row >= col

    dtv = dt_ref[...] + dtb_ref[...]
    dt = jnp.maximum(dtv, 0.0) + jnp.log(1.0 + jnp.exp(-jnp.abs(dtv)))
    if n_valid < q:
        dt = jnp.where(lax.broadcasted_iota(jnp.int32, dt.shape, 0) < n_valid, dt, 0.0)
    a = dt * (-jnp.exp(alog_ref[...]))
    a_cs = _dot(tril.astype(F32), a, precision=HIGHEST)
    a_cs_t = a_cs.T
    expand = e_ref[...]
    dt_full = _dot(dt, expand, precision=HIGHEST)
    acs_full = _dot(a_cs, expand, precision=HIGHEST)
    tot_full = acs_full[q - 1:q, :]
    xr = xs * dt_full
    xr_dec = (xr * jnp.exp(tot_full - acs_full)).astype(BF16)
    exp_acs = jnp.exp(acs_full)
    exp_tot = jnp.exp(tot_full)

    lane = lax.broadcasted_iota(jnp.int32, (q, LANES), 1)
    lo_half = lane < SSD_HEADDIM
    heads_per_tile = LANES // SSD_HEADDIM

    y = dskip_ref[...] * xs
    y_parts = []
    for g in range(SSD_GROUPS):
        bg = bm[:, g * SSD_STATE:(g + 1) * SSD_STATE]
        cg = cm[:, g * SSD_STATE:(g + 1) * SSD_STATE].astype(BF16)
        bg_t = bg.T.astype(BF16)
        cb = _dot(cg, bg_t)
        ht_g = ht_ref[:, g * gw:(g + 1) * gw]
        y_g = _dot(cg, ht_g.astype(BF16)) * exp_acs[:, g * gw:(g + 1) * gw]
        tiles = []
        for t in range(gw // LANES):
            base = g * gw + t * LANES
            xr_t = xr[:, base:base + LANES]
            acc = None
            for e in range(heads_per_tile):
                h = base // SSD_HEADDIM + e
                seg = a_cs[:, h:h + 1] - a_cs_t[h:h + 1, :]
                lmat = jnp.where(tril, jnp.exp(seg), 0.0)
                in_head = lo_half if e == 0 else jnp.logical_not(lo_half)
                part = _dot((cb * lmat).astype(BF16), jnp.where(in_head, xr_t, 0.0).astype(BF16))
                acc = part if acc is None else acc + part
            tiles.append(acc)
        y_g = y_g + jnp.concatenate(tiles, axis=1)
        states = _dot(bg_t, xr_dec[:, g * gw:(g + 1) * gw])
        ht_ref[:, g * gw:(g + 1) * gw] = exp_tot[:, g * gw:(g + 1) * gw] * ht_g + states
        y_parts.append(y_g)
    y = y + jnp.concatenate(y_parts, axis=1)

    y = y * _silu(z_ref[...])
    for g in range(SSD_GROUPS):
        yg = y[:, g * gw:(g + 1) * gw]
        ms = jnp.mean(yg * yg, axis=-1, keepdims=True)
        y_ref[:, g * gw:(g + 1) * gw] = ((yg * lax.rsqrt(ms + NORM_EPS)) * nw_ref[:, g * gw:(g + 1) * gw]).astype(y_ref.dtype)

    @pl.when(c == n_chunks - 1)
    def _():
        hfin_ref[...] = ht_ref[...].T


def ssd_mixer(proj, dt_raw, conv_state8, h0, wts, *, batch, seq, n_valid):
    q = min(SSD_CHUNK, seq)
    assert seq % q == 0 and q % SUBLANES == 0
    nc = seq // q
    assert n_valid == q or nc == 1
    inner = h0.shape[1]
    gn = SSD_GROUPS * SSD_STATE
    assert inner % gn == 0 and (inner // gn) * gn == inner
    kb = inner // gn
    rows = lambda b, c: b * nc + c
    full = lambda b, c: (0, 0)
    in_specs = [
        pl.BlockSpec((q, inner), lambda b, c: (rows(b, c), 0)),
        pl.BlockSpec((q, inner), lambda b, c: (rows(b, c), 1)),
        pl.BlockSpec((q, gn), lambda b, c: (rows(b, c), 2 * kb)),
        pl.BlockSpec((q, gn), lambda b, c: (rows(b, c), 2 * kb + 1)),
        pl.BlockSpec((q, LANES), lambda b, c: (rows(b, c), 0)),
        pl.BlockSpec((None, SUBLANES, inner), lambda b, c: (b, 0, 0)),
        pl.BlockSpec((None, SUBLANES, gn), lambda b, c: (b, 0, kb)),
        pl.BlockSpec((None, SUBLANES, gn), lambda b, c: (b, 0, kb + 1)),
        pl.BlockSpec((None, inner, SSD_STATE), lambda b, c: (b, 0, 0)),
        pl.BlockSpec((SSD_CONV, inner), full),
        pl.BlockSpec((SSD_CONV, gn), lambda b, c: (0, kb)),
        pl.BlockSpec((SSD_CONV, gn), lambda b, c: (0, kb + 1)),
        pl.BlockSpec((1, inner), full),
        pl.BlockSpec((1, gn), lambda b, c: (0, kb)),
        pl.BlockSpec((1, gn), lambda b, c: (0, kb + 1)),
        pl.BlockSpec((1, LANES), full),
        pl.BlockSpec((1, LANES), full),
        pl.BlockSpec((1, inner), full),
        pl.BlockSpec((1, inner), full),
        pl.BlockSpec((LANES, inner), full),
    ]
    cw, cb = wts["conv_w"], wts["conv_b"]
    y, hfin = pl.pallas_call(
        functools.partial(_ssd_kernel, q=q, n_valid=n_valid, n_chunks=nc),
        out_shape=(jax.ShapeDtypeStruct((batch * seq, inner), BF16),
                   jax.ShapeDtypeStruct((batch, inner, SSD_STATE), F32)),
        grid=(batch, nc),
        in_specs=in_specs,
        out_specs=(pl.BlockSpec((q, inner), lambda b, c: (rows(b, c), 0)),
                   pl.BlockSpec((None, inner, SSD_STATE), lambda b, c: (b, 0, 0))),
        scratch_shapes=[pltpu.VMEM((SSD_STATE, inner), F32),
                        pltpu.VMEM((q + SUBLANES, inner), F32),
                        pltpu.VMEM((q + SUBLANES, gn), F32),
                        pltpu.VMEM((q + SUBLANES, gn), F32)],
        compiler_params=_cparams("parallel", "arbitrary"),
        name="ssd_mixer",
    )(proj, proj, proj, proj, dt_raw, conv_state8, conv_state8, conv_state8, h0,
      cw, cw, cw, cb, cb, cb, wts["dt_bias"], wts["a_log"], wts["d_skip"], wts["norm_w"], wts["expand"])
    return y, hfin


def _short_conv_kernel(b_ref, c_ref, x_ref, g_ref, st_ref, w_ref, y_ref, last_ref, buf_ref, *, tq):
    @pl.when(pl.program_id(2) == 0)
    def _():
        buf_ref[0:SUBLANES, :] = st_ref[...]

    prod = c_ref[...] * x_ref[...]
    buf_ref[SUBLANES:SUBLANES + tq, :] = prod
    u = w_ref[SC_WIDTH - 1:SC_WIDTH, :] * prod
    for k in range(1, SC_WIDTH):
        u = u + w_ref[SC_WIDTH - 1 - k:SC_WIDTH - k, :] * buf_ref[SUBLANES - k:SUBLANES - k + tq, :]
    tail = prod[tq - SUBLANES:tq, :]
    buf_ref[0:SUBLANES, :] = tail
    last_ref[...] = tail
    y_ref[...] = (b_ref[...] * u * _silu(g_ref[...])).astype(y_ref.dtype)


def short_conv(proj, state8, w, *, batch, seq, dim, col0, tq=256, tc=1024):
    tq = _tile(seq, tq)
    nt = seq // tq
    ncb = dim // tc
    assert col0 % tc == 0 and dim % tc == 0 and tq % SUBLANES == 0
    cb0 = col0 // tc

    def part(k):
        return pl.BlockSpec((tq, tc), lambda b, j, t: (b * nt + t, cb0 + k * ncb + j))

    return pl.pallas_call(
        functools.partial(_short_conv_kernel, tq=tq),
        out_shape=(jax.ShapeDtypeStruct((batch * seq, dim), BF16),
                   jax.ShapeDtypeStruct((batch, SUBLANES, dim), F32)),
        grid=(batch, ncb, nt),
        in_specs=[part(0), part(1), part(2), part(3),
                  pl.BlockSpec((None, SUBLANES, tc), lambda b, j, t: (b, 0, j)),
                  pl.BlockSpec((SC_WIDTH, tc), lambda b, j, t: (0, j))],
        out_specs=(pl.BlockSpec((tq, tc), lambda b, j, t: (b * nt + t, j)),
                   pl.BlockSpec((None, SUBLANES, tc), lambda b, j, t: (b, 0, j))),
        scratch_shapes=[pltpu.VMEM((tq + SUBLANES, tc), F32)],
        compiler_params=_cparams("parallel", "parallel", "arbitrary"),
        name="short_conv",
    )(proj, proj, proj, proj, state8, w)


def _top_blocks(gate, n_valid, axis=1):
    idx = lax.broadcasted_iota(jnp.int32, gate.shape, axis).astype(F32)
    g = jnp.where(idx < jnp.asarray(n_valid, F32), gate, NEG_INF)
    sel = jnp.zeros(gate.shape, F32)
    for _ in range(MOBA_TOPK):
        m = jnp.max(g, axis=axis, keepdims=True)
        first = jnp.min(jnp.where(g == m, idx, float(gate.shape[axis])), axis=axis, keepdims=True)
        pick = idx == first
        sel = jnp.where(pick & (m > 0.5 * NEG_INF), 1.0, sel)
        g = jnp.where(pick, -jnp.inf, g)
    return sel


def _moba_prompt_kernel(slopes_ref, q_ref, k_ref, v_ref, g_ref, o_ref, kmean_ref, kb_ref, vt_ref, mval_ref, *, n_blocks):
    kv = pl.program_id(1)
    i = pl.program_id(2)
    blk = MOBA_BLOCK
    heads = range(Q_PER_KV)

    @pl.when(i == 0)
    def _():
        kmean_ref[...] = jnp.zeros(kmean_ref.shape, F32)
        for j in range(n_blocks):
            kj = k_ref[j * blk:(j + 1) * blk, :]
            kmean_ref[j:j + 1, :] = jnp.sum(kj, axis=0, keepdims=True) * (1.0 / blk)
            kb_ref[j] = kj.astype(BF16)
            vt_ref[j] = v_ref[j * blk:(j + 1) * blk, :].T.astype(BF16)

    rel = (lax.broadcasted_iota(jnp.int32, (blk, blk), 1)
           - lax.broadcasted_iota(jnp.int32, (blk, blk), 0)).astype(F32)
    blk_idx = lax.broadcasted_iota(jnp.int32, (kmean_ref.shape[0], blk), 0)
    back = (i - blk_idx).astype(F32) * float(blk)
    kmean = kmean_ref[...]

    qt, bias, diag = [], [], []
    for g in heads:
        slope = slopes_ref[kv * Q_PER_KV + g]
        q_t = (q_ref[:, g * HEAD_DIM:(g + 1) * HEAD_DIM] * (HEAD_DIM ** -0.5)).T
        gate = _dot(kmean, q_t, precision=HIGHEST)
        sel = _top_blocks(gate, i, axis=0)
        mval_ref[:, g * blk:(g + 1) * blk] = jnp.where(sel > 0.0, -slope * back, NEG_INF)
        bias.append(-slope * rel)
        diag.append(jnp.where(rel >= 0, -slope * rel, NEG_INF))
        qt.append(q_t.astype(BF16))
    qt = jnp.concatenate(qt, axis=1)
    bias = jnp.concatenate(bias, axis=1)

    s = _dot(kb_ref[i], qt) + jnp.concatenate(diag, axis=1)
    m0 = jnp.max(s, axis=0, keepdims=True)
    p = jnp.exp(s - m0)
    l0 = jnp.sum(p, axis=0, keepdims=True)
    acc0 = _dot(vt_ref[i], p.astype(BF16))

    def past_pair(t, carry):
        m, l, acc = carry
        js = (2 * t, 2 * t + 1)
        ss = [_dot(kb_ref[j], qt) + bias for j in js]
        masks = [mval_ref[pl.ds(j, 1), :] for j in js]
        m_new = m
        for s, mask in zip(ss, masks):
            m_new = jnp.maximum(m_new, jnp.max(s, axis=0, keepdims=True) + mask)
        l = jnp.exp(m - m_new) * l
        acc = jnp.exp(m - m_new) * acc
        for j, s, mask in zip(js, ss, masks):
            p = jnp.exp(s - (m_new - mask))
            l = l + jnp.sum(p, axis=0, keepdims=True)
            acc = acc + _dot(vt_ref[j], p.astype(BF16))
        return m_new, l, acc

    _, l, acc = lax.fori_loop(0, (i + 1) // 2, past_pair, (m0, l0, acc0))
    o_t = acc / l
    for g in heads:
        c = slice(g * HEAD_DIM, (g + 1) * HEAD_DIM)
        o_ref[:, c] = (o_t[:, g * blk:(g + 1) * blk].T * _silu(g_ref[:, c])).astype(o_ref.dtype)


def moba_prompt(proj, slopes, *, batch, seq):
    assert seq % MOBA_BLOCK == 0
    nb = seq // MOBA_BLOCK
    nbp = -(-nb // SUBLANES) * SUBLANES
    gw = Q_PER_KV * HEAD_DIM
    kcol = ATT_HEADS
    vcol = ATT_HEADS + KV_HEADS
    gcol = (ATT_HEADS + 2 * KV_HEADS) * HEAD_DIM // gw
    assert gcol * gw == (ATT_HEADS + 2 * KV_HEADS) * HEAD_DIM
    return pl.pallas_call(
        functools.partial(_moba_prompt_kernel, n_blocks=nb),
        out_shape=jax.ShapeDtypeStruct((batch * seq, ATT_HEADS * HEAD_DIM), BF16),
        grid=(batch, KV_HEADS, nb),
        in_specs=[pl.BlockSpec(memory_space=pltpu.SMEM),
                  pl.BlockSpec((MOBA_BLOCK, gw), lambda b, kv, i: (b * nb + i, kv)),
                  pl.BlockSpec((seq, HEAD_DIM), lambda b, kv, i: (b, kcol + kv)),
                  pl.BlockSpec((seq, HEAD_DIM), lambda b, kv, i: (b, vcol + kv)),
                  pl.BlockSpec((MOBA_BLOCK, gw), lambda b, kv, i: (b * nb + i, gcol + kv))],
        out_specs=pl.BlockSpec((MOBA_BLOCK, gw), lambda b, kv, i: (b * nb + i, kv)),
        scratch_shapes=[pltpu.VMEM((nbp, HEAD_DIM), F32),
                        pltpu.VMEM((nb, MOBA_BLOCK, HEAD_DIM), BF16),
                        pltpu.VMEM((nb, HEAD_DIM, MOBA_BLOCK), BF16),
                        pltpu.VMEM((nbp, Q_PER_KV * MOBA_BLOCK), F32)],
        compiler_params=_cparams("parallel", "parallel", "arbitrary"),
        name="moba_prompt",
    )(slopes, proj, proj, proj, proj)


def _page_specs(layer):
    return [pl.BlockSpec((None, None, PAGE_ROWS, HEAD_DIM),
                         lambda b, s, pt, p=p: (layer, pt[b, s * PAGES_PER_STEP + p], 0, 0))
            for p in range(PAGES_PER_STEP)]


def _kmean_kernel(pt_ref, *refs):
    del pt_ref
    pages, o_ref = refs[:PAGES_PER_STEP], refs[PAGES_PER_STEP]
    for bb in range(BLOCKS_PER_STEP):
        tot = None
        for pg in range(PAGES_PER_BLOCK):
            page = pages[bb * PAGES_PER_BLOCK + pg][...]
            part = jnp.sum(page.reshape(PAGE_ROWS // SUBLANES, SUBLANES, HEAD_DIM), axis=0)
            tot = part if tot is None else tot + part
        per_kv = tot[0:KV_HEADS]
        for par in range(1, SUBLANES // KV_HEADS):
            per_kv = per_kv + tot[par * KV_HEADS:(par + 1) * KV_HEADS]
        o_ref[bb] = per_kv * (1.0 / MOBA_BLOCK)


def paged_block_means(k_pools, page_table, *, layer, n_blocks):
    db = page_table.shape[0]
    assert n_blocks % BLOCKS_PER_STEP == 0 and SUBLANES % KV_HEADS == 0
    return pl.pallas_call(
        _kmean_kernel,
        out_shape=jax.ShapeDtypeStruct((db, n_blocks, KV_HEADS, HEAD_DIM), F32),
        grid_spec=pltpu.PrefetchScalarGridSpec(
            num_scalar_prefetch=1,
            grid=(db, n_blocks // BLOCKS_PER_STEP),
            in_specs=_page_specs(layer),
            out_specs=pl.BlockSpec((None, BLOCKS_PER_STEP, KV_HEADS, HEAD_DIM), lambda b, s, pt: (b, s, 0, 0))),
        compiler_params=_cparams("parallel", "arbitrary"),
        name="paged_block_means",
    )(page_table, *([k_pools] * PAGES_PER_STEP))


def _moba_sample_kernel(pt_ref, q_ref, kmean_ref, slope_ref, knew_ref, vnew_ref, *refs,
                        n_blocks, past_len, dec_seq):
    del pt_ref
    k_pages, v_pages = refs[:PAGES_PER_STEP], refs[PAGES_PER_STEP:2 * PAGES_PER_STEP]
    o_ref, m_ref, l_ref, acc_ref, sel_ref, base_ref = refs[2 * PAGES_PER_STEP:]
    step = pl.program_id(1)
    rows = q_ref.shape[0]
    rpk = rows // KV_HEADS
    kv_shift = KV_HEADS.bit_length() - 1
    q = q_ref[...] * (HEAD_DIM ** -0.5)
    qb = q.astype(BF16)
    slope = slope_ref[:, 0:1]
    lane = lax.broadcasted_iota(jnp.int32, (rows, LANES), 1)

    def key_bias(width, q_pos0, causal):
        r = lax.broadcasted_iota(jnp.int32, (rows, width), 0)
        c = lax.broadcasted_iota(jnp.int32, (rows, width), 1)
        tok = jnp.right_shift(c, kv_shift)
        t_row = jnp.bitwise_and(r, dec_seq - 1)
        same_head = jnp.bitwise_and(c, KV_HEADS - 1) == jnp.right_shift(r, rpk.bit_length() - 1)
        dist = (q_pos0 + t_row - tok).astype(F32)
        ok = same_head & (dist >= 0) if causal else same_head
        return jnp.where(ok, -slope * dist, NEG_INF)

    @pl.when(step == 0)
    def _():
        gate = jnp.concatenate([_dot_nt(q[kv * rpk:(kv + 1) * rpk, :], kmean_ref[kv], precision=HIGHEST)
                                for kv in range(KV_HEADS)], axis=0)
        sel_ref[...] = _top_blocks(gate, n_blocks)
        base_ref[...] = key_bias(PAGE_ROWS, past_len, causal=False)
        s = _dot_nt(qb, knew_ref[...].astype(BF16)) + key_bias(LANES, 0, causal=True)
        m = jnp.max(s, axis=1, keepdims=True)
        p = jnp.exp(s - m)
        m_ref[...] = jnp.broadcast_to(m, m_ref.shape)
        l_ref[...] = jnp.broadcast_to(jnp.sum(p, axis=1, keepdims=True), l_ref.shape)
        acc_ref[...] = _dot(p.astype(BF16), vnew_ref[...].astype(BF16))

    sel = sel_ref[...]
    base = base_ref[...]
    m, l, acc = m_ref[:, 0:1], l_ref[:, 0:1], acc_ref[...]
    scores, offsets = [], []
    m_new = m
    for bb in range(BLOCKS_PER_STEP):
        blk_id = step * BLOCKS_PER_STEP + bb
        chosen = jnp.sum(jnp.where(lane == blk_id, sel, 0.0), axis=1, keepdims=True) > 0.0
        for pg in range(PAGES_PER_BLOCK):
            page_pos = (blk_id * MOBA_BLOCK + pg * PAGE_SIZE).astype(F32)
            mcol = jnp.where(chosen, slope * page_pos, NEG_INF)
            s = _dot_nt(qb, k_pages[bb * PAGES_PER_BLOCK + pg][...].astype(BF16)) + base
            m_new = jnp.maximum(m_new, jnp.max(s, axis=1, keepdims=True) + mcol)
            scores.append(s)
            offsets.append(mcol)
    alpha = jnp.exp(m - m_new)
    l, acc, m = alpha * l, alpha * acc, m_new
    for pg, (s, mcol) in enumerate(zip(scores, offsets)):
        p = jnp.exp(s - (m - mcol))
        l = l + jnp.sum(p, axis=1, keepdims=True)
        acc = acc + _dot(p.astype(BF16), v_pages[pg][...].astype(BF16))
    m_ref[...] = jnp.broadcast_to(m, m_ref.shape)
    l_ref[...] = jnp.broadcast_to(l, l_ref.shape)
    acc_ref[...] = acc

    @pl.when(step == n_blocks // BLOCKS_PER_STEP - 1)
    def _():
        o_ref[...] = acc / l


def moba_sample(q_rows, kmean_t, slope_rows, k_new_pad, v_new_pad, k_pools, v_pools, page_table,
                *, layer, n_blocks, past_len, dec_seq):
    db, rows, _ = q_rows.shape
    rpk = rows // KV_HEADS
    for n in (KV_HEADS, dec_seq, rpk):
        assert n & (n - 1) == 0
    assert n_blocks % BLOCKS_PER_STEP == 0 and dec_seq * KV_HEADS <= LANES
    per_b = lambda *shape: pl.BlockSpec((None,) + shape, lambda b, s, pt: (b,) + (0,) * len(shape))
    return pl.pallas_call(
        functools.partial(_moba_sample_kernel, n_blocks=n_blocks, past_len=past_len, dec_seq=dec_seq),
        out_shape=jax.ShapeDtypeStruct((db, rows, HEAD_DIM), F32),
        grid_spec=pltpu.PrefetchScalarGridSpec(
            num_scalar_prefetch=1,
            grid=(db, n_blocks // BLOCKS_PER_STEP),
            in_specs=[per_b(rows, HEAD_DIM),
                      per_b(KV_HEADS, LANES, HEAD_DIM),
                      pl.BlockSpec((rows, LANES), lambda b, s, pt: (0, 0)),
                      per_b(LANES, HEAD_DIM),
                      per_b(LANES, HEAD_DIM)] + _page_specs(layer) + _page_specs(layer),
            out_specs=per_b(rows, HEAD_DIM),
            scratch_shapes=[pltpu.VMEM((rows, LANES), F32), pltpu.VMEM((rows, LANES), F32),
                            pltpu.VMEM((rows, HEAD_DIM), F32), pltpu.VMEM((rows, LANES), F32),
                            pltpu.VMEM((rows, PAGE_ROWS), F32)]),
        compiler_params=_cparams("parallel", "arbitrary"),
        name="moba_sample",
    )(page_table, q_rows, kmean_t, slope_rows, k_new_pad, v_new_pad,
      *([k_pools] * PAGES_PER_STEP), *([v_pools] * PAGES_PER_STEP))


def _gate_kernel(o_ref, g_ref, y_ref):
    y_ref[...] = (o_ref[...] * _silu(g_ref[...])).astype(y_ref.dtype)


def silu_gate(o, proj, *, col0, tn=1024):
    m, n = o.shape
    assert col0 % tn == 0 and n % tn == 0
    cb = col0 // tn
    return pl.pallas_call(
        _gate_kernel,
        out_shape=jax.ShapeDtypeStruct((m, n), BF16),
        grid=(n // tn,),
        in_specs=[pl.BlockSpec((m, tn), lambda j: (0, j)),
                  pl.BlockSpec((m, tn), lambda j: (0, cb + j))],
        out_specs=pl.BlockSpec((m, tn), lambda j: (0, j)),
        compiler_params=_cparams("parallel"),
        name="silu_gate",
    )(o, proj)


def _cross_layer_kernel(x_ref, nw_ref, wq_ref, k_ref, v_ref, wo_ref, o_ref):
    width = CA_HEADS * CA_HEAD_DIM
    x = x_ref[...]
    ms = jnp.mean(x * x, axis=-1, keepdims=True)
    xn = ((x * lax.rsqrt(ms + NORM_EPS)) * nw_ref[...]).astype(BF16)
    qg = _dot(xn, wq_ref[...])
    gated = []
    for h in range(CA_HEADS):
        c = slice(h * CA_HEAD_DIM, (h + 1) * CA_HEAD_DIM)
        q = (qg[:, c] * (CA_HEAD_DIM ** -0.5)).astype(BF16)
        s = _dot_nt(q, k_ref[:, c].astype(BF16))
        m = jnp.max(s, axis=1, keepdims=True)
        p = jnp.exp(s - m)
        l = jnp.sum(p, axis=1, keepdims=True)
        o = _dot(p.astype(BF16), v_ref[:, c].astype(BF16)) / l
        g = qg[:, width + h * CA_HEAD_DIM:width + (h + 1) * CA_HEAD_DIM]
        gated.append((o * _silu(g)).astype(BF16))
    o_ref[...] = x + _dot(jnp.concatenate(gated, axis=1), wo_ref[...])


def cross_layer(x, norm_w, w_q, mem_k, mem_v, w_out, *, batch, seq, tm=512):
    width = CA_HEADS * CA_HEAD_DIM
    d = x.shape[1]
    n_mem = mem_k.shape[1]
    tm = _tile(seq, tm)
    nt = seq // tm
    const = lambda i: (0, 0)
    return pl.pallas_call(
        _cross_layer_kernel,
        out_shape=jax.ShapeDtypeStruct((batch * seq, d), F32),
        grid=(batch * nt,),
        in_specs=[pl.BlockSpec((tm, d), lambda i: (i, 0)),
                  pl.BlockSpec((1, d), const),
                  pl.BlockSpec((d, 2 * width), const),
                  pl.BlockSpec((None, n_mem, width), lambda i: (i // nt, 0, 0)),
                  pl.BlockSpec((None, n_mem, width), lambda i: (i // nt, 0, 0)),
                  pl.BlockSpec((width, d), const)],
        out_specs=pl.BlockSpec((tm, d), lambda i: (i, 0)),
        compiler_params=_cparams("parallel"),
        name="cross_layer",
    )(x, norm_w.reshape(1, d), w_q, mem_k, mem_v, w_out)


def _alibi_slopes():
    return np.array([2.0 ** (-8.0 * (h + 1) / ATT_HEADS) for h in range(ATT_HEADS)], dtype=np.float32)


def _pad_rows(a, n, front=False):
    extra = n - a.shape[-2]
    pad = [(0, 0)] * a.ndim
    pad[-2] = (extra, 0) if front else (0, extra)
    return jnp.pad(a, pad)


def _pad_lanes(a):
    pad = [(0, 0)] * a.ndim
    pad[-1] = (0, LANES - a.shape[-1])
    return jnp.pad(a, pad)


def _even_layer(x, w, conv_state, ssm_state, sc_state, *, batch, seq):
    d = x.shape[1]
    inner = d
    conv_ch = inner + 2 * SSD_GROUPS * SSD_STATE
    proj = norm_matmul(x, w["norm_mix"], w["w_in"])
    dt_raw = norm_matmul(x, w["norm_mix"], w["w_dt"])
    new_conv = proj.reshape(batch, seq, -1)[:, seq - (SSD_CONV - 1):, inner:inner + conv_ch]

    lpad = -(-seq // SSD_CHUNK) * SSD_CHUNK if seq > SSD_CHUNK else max(seq, SSD_CHUNK)
    if lpad != seq:
        assert seq < SSD_CHUNK
        ssd_in = _pad_rows(proj.reshape(batch, seq, -1)[:, :, :inner + conv_ch], lpad).reshape(batch * lpad, -1)
        dt_in = _pad_rows(dt_raw.reshape(batch, seq, -1), lpad).reshape(batch * lpad, -1)
    else:
        ssd_in, dt_in = proj, dt_raw
    y, h_fin = ssd_mixer(ssd_in, dt_in, _pad_rows(conv_state, SUBLANES, front=True),
                         ssm_state.reshape(batch, inner, SSD_STATE), w,
                         batch=batch, seq=lpad, n_valid=min(seq, SSD_CHUNK))
    if lpad != seq:
        y = y.reshape(batch, lpad, inner)[:, :seq].reshape(batch * seq, inner)

    y_sc, sc_tail = short_conv(proj, _pad_rows(sc_state, SUBLANES, front=True), w["sc_w"],
                               batch=batch, seq=seq, dim=d, col0=inner + conv_ch)
    x = matmul_residual([(y, w["w_out_ssd"]), (y_sc, w["w_out_sc"])], x)
    return (x, new_conv, h_fin.reshape(batch, inner // SSD_HEADDIM, SSD_HEADDIM, SSD_STATE),
            sc_tail[:, SUBLANES - (SC_WIDTH - 1):, :])


def _odd_prompt(x, w, slopes, *, batch, seq):
    proj = norm_matmul(x, w["norm_mix"], w["w_in"])
    att_q, att_kv = ATT_HEADS * HEAD_DIM, KV_HEADS * HEAD_DIM
    k = proj[:, att_q:att_q + att_kv].reshape(batch, seq, KV_HEADS, HEAD_DIM)
    v = proj[:, att_q + att_kv:att_q + 2 * att_kv].reshape(batch, seq, KV_HEADS, HEAD_DIM)
    og = moba_prompt(proj, slopes, batch=batch, seq=seq)
    return matmul_residual([(og, w["w_out"])], x), k, v


def _odd_sample(x, w, k_pools, v_pools, page_table, slope_rows, *, layer, batch, seq, past_len):
    assert past_len % MOBA_BLOCK == 0 and seq <= MOBA_BLOCK and seq <= LANES
    n_blocks = past_len // MOBA_BLOCK
    assert MOBA_TOPK <= n_blocks <= LANES
    proj = norm_matmul(x, w["norm_mix"], w["w_in"])
    att_q, att_kv = ATT_HEADS * HEAD_DIM, KV_HEADS * HEAD_DIM
    k_new = proj[:, att_q:att_q + att_kv]
    v_new = proj[:, att_q + att_kv:att_q + 2 * att_kv]
    q_rows = proj[:, :att_q].reshape(batch, seq, KV_HEADS, Q_PER_KV, HEAD_DIM).transpose(0, 2, 3, 1, 4)
    q_rows = q_rows.reshape(batch, ATT_HEADS * seq, HEAD_DIM)
    kmean = paged_block_means(k_pools, page_table, layer=layer, n_blocks=n_blocks)
    o = moba_sample(q_rows, _pad_rows(kmean.transpose(0, 2, 1, 3), LANES), slope_rows,
                    _pad_rows(k_new.reshape(batch, seq * KV_HEADS, HEAD_DIM), LANES),
                    _pad_rows(v_new.reshape(batch, seq * KV_HEADS, HEAD_DIM), LANES),
                    k_pools, v_pools, page_table, layer=layer, n_blocks=n_blocks, past_len=past_len, dec_seq=seq)
    o = o.reshape(batch, KV_HEADS, Q_PER_KV, seq, HEAD_DIM).transpose(0, 3, 1, 2, 4).reshape(batch * seq, att_q)
    og = silu_gate(o, proj, col0=att_q + 2 * att_kv)
    return (matmul_residual([(og, w["w_out"])], x),
            k_new.reshape(batch, seq, KV_HEADS, HEAD_DIM), v_new.reshape(batch, seq, KV_HEADS, HEAD_DIM))


def _cross_layer(x, w, mem_k, mem_v, *, batch, seq):
    return cross_layer(x, w["norm_cross"], w["ca_w_q"], mem_k, mem_v, w["ca_w_out"], batch=batch, seq=seq)


def kernel(x_prompt, x_sample, mem_prompt, cache_k, cache_v, page_table, state_conv, state_ssm, state_sc,
           cache_mem_k, cache_mem_v, norm_mix_w, norm_cross_w, norm_mem_w, final_norm_w,
           ev_w_in, ev_conv_w, ev_conv_b, ev_dt_bias, ev_a_log, ev_d_skip, ev_norm_w, ev_sc_w, ev_w_out,
           od_w_in, od_w_out, ca_w_q, ca_w_kv, ca_w_out):
    bp, sp, d = x_prompt.shape
    bs, ss, _ = x_sample.shape
    depth = norm_mix_w.shape[0]
    n_mem = mem_prompt.shape[1]
    n_heads = ev_dt_bias.shape[1]
    inner = n_heads * SSD_HEADDIM
    conv_ch = inner + 2 * SSD_GROUPS * SSD_STATE
    past_len = page_table.shape[1] * PAGE_SIZE
    ca_w = CA_HEADS * CA_HEAD_DIM
    assert inner == d and n_heads <= LANES

    slopes = jnp.asarray(_alibi_slopes())
    slope_rows = jnp.asarray(np.repeat(_alibi_slopes(), ss)[:, None] * np.ones((1, LANES), np.float32))
    expand_np = np.zeros((LANES, inner), np.float32)
    for hd in range(n_heads):
        expand_np[hd, hd * SSD_HEADDIM:(hd + 1) * SSD_HEADDIM] = 1.0
    expand = jnp.asarray(expand_np)

    xp = x_prompt.reshape(bp * sp, d)
    xs = x_sample.reshape(bs * ss, d)
    mem = mem_prompt.reshape(bp * n_mem, d)
    k_pools = cache_k.reshape(cache_k.shape[0], cache_k.shape[1], PAGE_ROWS, HEAD_DIM)
    v_pools = cache_v.reshape(cache_v.shape[0], cache_v.shape[1], PAGE_ROWS, HEAD_DIM)

    pk, pv, sk, sv = [], [], [], []
    pconv, pssm, psc, sconv, sssm, ssc = [], [], [], [], [], []
    pmk, pmv = [], []
    for l in range(depth):
        i = l // 2
        if l % 2 == 0:
            w_in = ev_w_in[i]
            dt0 = inner + conv_ch
            w = {
                "norm_mix": norm_mix_w[l],
                "w_in": jnp.concatenate([w_in[:, :dt0], w_in[:, dt0 + n_heads:]], axis=1).astype(BF16),
                "w_dt": _pad_lanes(w_in[:, dt0:dt0 + n_heads]).astype(BF16),
                "conv_w": ev_conv_w[i],
                "conv_b": ev_conv_b[i].reshape(1, conv_ch),
                "dt_bias": _pad_lanes(ev_dt_bias[i].reshape(1, n_heads)),
                "a_log": _pad_lanes(ev_a_log[i].reshape(1, n_heads)),
                "d_skip": jnp.repeat(ev_d_skip[i], SSD_HEADDIM).reshape(1, inner),
                "norm_w": ev_norm_w[i].reshape(1, inner),
                "expand": expand,
                "sc_w": ev_sc_w[i],
                "w_out_ssd": ev_w_out[i][:inner].astype(BF16),
                "w_out_sc": ev_w_out[i][inner:].astype(BF16),
            }
            xp, c1, s1, q1 = _even_layer(xp, w, jnp.zeros((bp, SSD_CONV - 1, conv_ch), F32),
                                         jnp.zeros((bp, n_heads, SSD_HEADDIM, SSD_STATE), F32),
                                         jnp.zeros((bp, SC_WIDTH - 1, d), F32), batch=bp, seq=sp)
            xs, c2, s2, q2 = _even_layer(xs, w, state_conv[i], state_ssm[i], state_sc[i], batch=bs, seq=ss)
            pconv.append(c1); pssm.append(s1); psc.append(q1)
            sconv.append(c2); sssm.append(s2); ssc.append(q2)
        else:
            w = {"norm_mix": norm_mix_w[l], "w_in": od_w_in[i].astype(BF16), "w_out": od_w_out[i].astype(BF16)}
            xp, k1, v1 = _odd_prompt(xp, w, slopes, batch=bp, seq=sp)
            xs, k2, v2 = _odd_sample(xs, w, k_pools, v_pools, page_table, slope_rows,
                                     layer=i, batch=bs, seq=ss, past_len=past_len)
            pk.append(k1); pv.append(v1); sk.append(k2); sv.append(v2)
        wc = {"norm_cross": norm_cross_w[l], "ca_w_q": ca_w_q[l].astype(BF16), "ca_w_out": ca_w_out[l].astype(BF16)}
        mkv = norm_matmul(mem, norm_mem_w[l], ca_w_kv[l].astype(BF16))
        mk = mkv[:, :ca_w].reshape(bp, n_mem, ca_w)
        mv = mkv[:, ca_w:].reshape(bp, n_mem, ca_w)
        pmk.append(mk.reshape(bp, n_mem, CA_HEADS, CA_HEAD_DIM))
        pmv.append(mv.reshape(bp, n_mem, CA_HEADS, CA_HEAD_DIM))
        xp = _cross_layer(xp, wc, mk, mv, batch=bp, seq=sp)
        xs = _cross_layer(xs, wc, cache_mem_k[l].reshape(bs, n_mem, ca_w), cache_mem_v[l].reshape(bs, n_mem, ca_w),
                          batch=bs, seq=ss)
    y_prompt = rmsnorm(xp, final_norm_w).reshape(bp, sp, d)
    y_sample = rmsnorm(xs, final_norm_w).reshape(bs, ss, d)
    return (y_prompt, y_sample,
            jnp.stack(pk), jnp.stack(pv), jnp.stack(pconv), jnp.stack(pssm), jnp.stack(psc),
            jnp.stack(pmk), jnp.stack(pmv),
            jnp.stack(sk), jnp.stack(sv), jnp.stack(sconv), jnp.stack(sssm), jnp.stack(ssc))
```

```python
import functools
import math

import numpy as np
import jax
import jax.numpy as jnp
from jax import lax
from jax.experimental import pallas as pl
from jax.experimental.pallas import tpu as pltpu

F32 = jnp.float32
BF16 = jnp.bfloat16
HIGHEST = lax.Precision.HIGHEST

NORM_EPS = 1e-5
NEG_INF = -1e30

LANES = 128
SUBLANES = 8
VMEM_LIMIT = 48 * 1024 * 1024

SSD_HEADDIM = 64
SSD_GROUPS = 4
SSD_STATE = 128
SSD_CONV = 4
SSD_CHUNK = 128
SC_WIDTH = 3
ATT_HEADS = 16
KV_HEADS = 4
HEAD_DIM = 128
Q_PER_KV = ATT_HEADS // KV_HEADS
MOBA_BLOCK = 256
MOBA_TOPK = 3
PAGE_SIZE = 128
PAGES_PER_BLOCK = MOBA_BLOCK // PAGE_SIZE
PAGE_ROWS = PAGE_SIZE * KV_HEADS
BLOCKS_PER_STEP = 8
PAGES_PER_STEP = BLOCKS_PER_STEP * PAGES_PER_BLOCK
CA_HEADS = 4
CA_HEAD_DIM = 128


def _cparams(*sem):
    return pltpu.CompilerParams(dimension_semantics=sem, vmem_limit_bytes=VMEM_LIMIT)


def _tile(n, pref):
    if n <= pref:
        return n
    t = pref
    while n % t:
        t //= 2
    return t


def _silu(x):
    return x / (1.0 + jnp.exp(-x))


def _dot(a, b, **kw):
    return jnp.dot(a, b, preferred_element_type=F32, **kw)


def _bf16_pieces(x):
    hi = x.astype(BF16)
    rest = x - hi.astype(F32)
    mid = rest.astype(BF16)
    lo = (rest - mid.astype(F32)).astype(BF16)
    return hi, mid, lo


def _dot_exact_left(x, sel):
    return sum(_dot(p, sel) for p in _bf16_pieces(x))


def _dot_exact_right(sel, x):
    return sum(_dot(sel, p) for p in _bf16_pieces(x))


def _dot_nt(a, b, **kw):
    return lax.dot_general(a, b, (((1,), (1,)), ((), ())), preferred_element_type=F32, **kw)


def _norm_mm_kernel(x_ref, nw_ref, w_ref, o_ref, xn_ref):
    @pl.when(pl.program_id(1) == 0)
    def _():
        x = x_ref[...]
        ms = jnp.mean(x * x, axis=-1, keepdims=True)
        xn_ref[...] = ((x * lax.rsqrt(ms + NORM_EPS)) * nw_ref[...]).astype(BF16)

    o_ref[...] = _dot(xn_ref[...], w_ref[...]).astype(o_ref.dtype)


def norm_matmul(x, norm_w, w, *, tm=1024, tn=1024, out_dtype=F32):
    m, k = x.shape
    n = w.shape[1]
    tm, tn = _tile(m, tm), _tile(n, tn)
    return pl.pallas_call(
        _norm_mm_kernel,
        out_shape=jax.ShapeDtypeStruct((m, n), out_dtype),
        grid=(m // tm, n // tn),
        in_specs=[pl.BlockSpec((tm, k), lambda i, j: (i, 0)),
                  pl.BlockSpec((1, k), lambda i, j: (0, 0)),
                  pl.BlockSpec((k, tn), lambda i, j: (0, j))],
        out_specs=pl.BlockSpec((tm, tn), lambda i, j: (i, j)),
        scratch_shapes=[pltpu.VMEM((tm, k), BF16)],
        compiler_params=_cparams("parallel", "arbitrary"),
        name="norm_matmul",
    )(x, norm_w.reshape(1, k), w)


def _mm_res_kernel(*refs, n_pairs):
    res_ref, o_ref = refs[2 * n_pairs], refs[2 * n_pairs + 1]
    acc = res_ref[...]
    for p in range(n_pairs):
        acc = acc + _dot(refs[2 * p][...], refs[2 * p + 1][...])
    o_ref[...] = acc


def matmul_residual(pairs, res, *, tm=512, tn=1024):
    m, n = res.shape
    tm, tn = _tile(m, tm), _tile(n, tn)
    in_specs, args = [], []
    for a, w in pairs:
        k = a.shape[1]
        in_specs += [pl.BlockSpec((tm, k), lambda i, j: (i, 0)),
                     pl.BlockSpec((k, tn), lambda i, j: (0, j))]
        args += [a, w]
    in_specs.append(pl.BlockSpec((tm, tn), lambda i, j: (i, j)))
    return pl.pallas_call(
        functools.partial(_mm_res_kernel, n_pairs=len(pairs)),
        out_shape=jax.ShapeDtypeStruct((m, n), F32),
        grid=(m // tm, n // tn),
        in_specs=in_specs,
        out_specs=pl.BlockSpec((tm, tn), lambda i, j: (i, j)),
        compiler_params=_cparams("parallel", "parallel"),
        name="matmul_residual",
    )(*args, res)


def _mm_kernel(a_ref, w_ref, o_ref):
    o_ref[...] = _dot(a_ref[...], w_ref[...]).astype(o_ref.dtype)


def matmul(a, w, *, tm=1024, tn=1024, out_dtype=F32):
    m, k = a.shape
    n = w.shape[1]
    tm, tn = _tile(m, tm), _tile(n, tn)
    return pl.pallas_call(
        _mm_kernel,
        out_shape=jax.ShapeDtypeStruct((m, n), out_dtype),
        grid=(m // tm, n // tn),
        in_specs=[pl.BlockSpec((tm, k), lambda i, j: (i, 0)),
                  pl.BlockSpec((k, tn), lambda i, j: (0, j))],
        out_specs=pl.BlockSpec((tm, tn), lambda i, j: (i, j)),
        compiler_params=_cparams("parallel", "parallel"),
        name="matmul",
    )(a, w)


def _rmsnorm_kernel(x_ref, nw_ref, o_ref):
    x = x_ref[...]
    ms = jnp.mean(x * x, axis=-1, keepdims=True)
    o_ref[...] = ((x * lax.rsqrt(ms + NORM_EPS)) * nw_ref[...]).astype(o_ref.dtype)


def rmsnorm(x, norm_w, *, tm=512, out_dtype=F32):
    m, k = x.shape
    tm = _tile(m, tm)
    return pl.pallas_call(
        _rmsnorm_kernel,
        out_shape=jax.ShapeDtypeStruct((m, k), out_dtype),
        grid=(m // tm,),
        in_specs=[pl.BlockSpec((tm, k), lambda i: (i, 0)),
                  pl.BlockSpec((1, k), lambda i: (0, 0))],
        out_specs=pl.BlockSpec((tm, k), lambda i: (i, 0)),
        compiler_params=_cparams("parallel"),
        name="rmsnorm",
    )(x, norm_w.reshape(1, k))


def _causal_conv_chunk(x_ref, buf_ref, cw_ref, cb_ref, q):
    x = x_ref[...]
    buf_ref[SUBLANES:SUBLANES + q, :] = x
    y = cw_ref[SSD_CONV - 1:SSD_CONV, :] * x
    for k in range(1, SSD_CONV):
        y = y + cw_ref[SSD_CONV - 1 - k:SSD_CONV - k, :] * buf_ref[SUBLANES - k:SUBLANES - k + q, :]
    if cb_ref is not None:
        y = y + cb_ref[...]
    buf_ref[0:SUBLANES, :] = x[q - SUBLANES:q, :]
    return y


def _ssd_kernel(z_ref, xs_ref, b_ref, c_ref, dt_ref, cs_xs_ref, cs_b_ref, cs_c_ref, h0_ref,
                cw_xs_ref, cw_b_ref, cw_c_ref, cb_xs_ref, cb_b_ref, cb_c_ref,
                dtb_ref, alog_ref, dskip_ref, nw_ref, e_ref,
                y_ref, hfin_ref,
                ht_ref, buf_xs, buf_b, buf_c, *, q, n_valid, n_chunks):
    c = pl.program_id(1)
    n_heads = e_ref.shape[1] // SSD_HEADDIM
    gw = (n_heads // SSD_GROUPS) * SSD_HEADDIM

    @pl.when(c == 0)
    def _():
        ht_ref[...] = h0_ref[...].T
        buf_xs[0:SUBLANES, :] = cs_xs_ref[...]
        buf_b[0:SUBLANES, :] = cs_b_ref[...]
        buf_c[0:SUBLANES, :] = cs_c_ref[...]

    xs = _silu(_causal_conv_chunk(xs_ref, buf_xs, cw_xs_ref, cb_xs_ref, q))
    bm = _silu(_causal_conv_chunk(b_ref, buf_b, cw_b_ref, cb_b_ref, q))
    cm = _silu(_causal_conv_chunk(c_ref, buf_c, cw_c_ref, cb_c_ref, q))

    row = lax.broadcasted_iota(jnp.int32, (q, q), 0)
    col = lax.broadcasted_iota(jnp.int32, (q, q), 1)
    tril = row >= col

    dtv = dt_ref[...] + dtb_ref[...]
    dt = jnp.maximum(dtv, 0.0) + jnp.log(1.0 + jnp.exp(-jnp.abs(dtv)))
    if n_valid < q:
        dt = jnp.where(lax.broadcasted_iota(jnp.int32, dt.shape, 0) < n_valid, dt, 0.0)
    a = dt * (-jnp.exp(alog_ref[...]))
    a_cs = _dot_exact_right(jnp.where(tril, 1.0, 0.0).astype(BF16), a)
    a_cs_t = a_cs.T
    expand = e_ref[...]
    dt_full = _dot_exact_left(dt, expand)
    acs_full = _dot_exact_left(a_cs, expand)
    tot_full = acs_full[q - 1:q, :]
    xr = xs * dt_full
    xr_dec = (xr * jnp.exp(tot_full - acs_full)).astype(BF16)
    exp_acs = jnp.exp(acs_full)
    exp_tot = jnp.exp(tot_full)

    lane = lax.broadcasted_iota(jnp.int32, (q, LANES), 1)
    lo_half = lane < SSD_HEADDIM
    heads_per_tile = LANES // SSD_HEADDIM

    y = dskip_ref[...] * xs
    y_parts = []
    for g in range(SSD_GROUPS):
        bg = bm[:, g * SSD_STATE:(g + 1) * SSD_STATE]
        cg = cm[:, g * SSD_STATE:(g + 1) * SSD_STATE].astype(BF16)
        bg_t = bg.T.astype(BF16)
        cb = _dot(cg, bg_t)
        ht_g = ht_ref[:, g * gw:(g + 1) * gw]
        y_g = _dot(cg, ht_g.astype(BF16)) * exp_acs[:, g * gw:(g + 1) * gw]
        tiles = []
        for t in range(gw // LANES):
            base = g * gw + t * LANES
            xr_t = xr[:, base:base + LANES]
            acc = None
            for e in range(heads_per_tile):
                h = base // SSD_HEADDIM + e
                seg = a_cs[:, h:h + 1] - a_cs_t[h:h + 1, :]
                lmat = jnp.where(tril, jnp.exp(seg), 0.0)
                in_head = lo_half if e == 0 else jnp.logical_not(lo_half)
                part = _dot((cb * lmat).astype(BF16), jnp.where(in_head, xr_t, 0.0).astype(BF16))
                acc = part if acc is None else acc + part
            tiles.append(acc)
        y_g = y_g + jnp.concatenate(tiles, axis=1)
        states = _dot(bg_t, xr_dec[:, g * gw:(g + 1) * gw])
        ht_ref[:, g * gw:(g + 1) * gw] = exp_tot[:, g * gw:(g + 1) * gw] * ht_g + states
        y_parts.append(y_g)
    y = y + jnp.concatenate(y_parts, axis=1)

    y = y * _silu(z_ref[...])
    for g in range(SSD_GROUPS):
        yg = y[:, g * gw:(g + 1) * gw]
        ms = jnp.mean(yg * yg, axis=-1, keepdims=True)
        y_ref[:, g * gw:(g + 1) * gw] = ((yg * lax.rsqrt(ms + NORM_EPS)) * nw_ref[:, g * gw:(g + 1) * gw]).astype(y_ref.dtype)

    @pl.when(c == n_chunks - 1)
    def _():
        hfin_ref[...] = ht_ref[...].T


def ssd_mixer(proj, dt_raw, conv_state8, h0, wts, *, batch, seq, n_valid):
    q = min(SSD_CHUNK, seq)
    assert seq % q == 0 and q % SUBLANES == 0
    nc = seq // q
    assert n_valid == q or nc == 1
    inner = h0.shape[1]
    gn = SSD_GROUPS * SSD_STATE
    assert inner % gn == 0 and (inner // gn) * gn == inner
    kb = inner // gn
    rows = lambda b, c: b * nc + c
    full = lambda b, c: (0, 0)
    in_specs = [
        pl.BlockSpec((q, inner), lambda b, c: (rows(b, c), 0)),
        pl.BlockSpec((q, inner), lambda b, c: (rows(b, c), 1)),
        pl.BlockSpec((q, gn), lambda b, c: (rows(b, c), 2 * kb)),
        pl.BlockSpec((q, gn), lambda b, c: (rows(b, c), 2 * kb + 1)),
        pl.BlockSpec((q, LANES), lambda b, c: (rows(b, c), 0)),
        pl.BlockSpec((None, SUBLANES, inner), lambda b, c: (b, 0, 0)),
        pl.BlockSpec((None, SUBLANES, gn), lambda b, c: (b, 0, kb)),
        pl.BlockSpec((None, SUBLANES, gn), lambda b, c: (b, 0, kb + 1)),
        pl.BlockSpec((None, inner, SSD_STATE), lambda b, c: (b, 0, 0)),
        pl.BlockSpec((SSD_CONV, inner), full),
        pl.BlockSpec((SSD_CONV, gn), lambda b, c: (0, kb)),
        pl.BlockSpec((SSD_CONV, gn), lambda b, c: (0, kb + 1)),
        pl.BlockSpec((1, inner), full),
        pl.BlockSpec((1, gn), lambda b, c: (0, kb)),
        pl.BlockSpec((1, gn), lambda b, c: (0, kb + 1)),
        pl.BlockSpec((1, LANES), full),
        pl.BlockSpec((1, LANES), full),
        pl.BlockSpec((1, inner), full),
        pl.BlockSpec((1, inner), full),
        pl.BlockSpec((LANES, inner), full),
    ]
    cw, cb = wts["conv_w"], wts["conv_b"]
    y, hfin = pl.pallas_call(
        functools.partial(_ssd_kernel, q=q, n_valid=n_valid, n_chunks=nc),
        out_shape=(jax.ShapeDtypeStruct((batch * seq, inner), BF16),
                   jax.ShapeDtypeStruct((batch, inner, SSD_STATE), F32)),
        grid=(batch, nc),
        in_specs=in_specs,
        out_specs=(pl.BlockSpec((q, inner), lambda b, c: (rows(b, c), 0)),
                   pl.BlockSpec((None, inner, SSD_STATE), lambda b, c: (b, 0, 0))),
        scratch_shapes=[pltpu.VMEM((SSD_STATE, inner), F32),
                        pltpu.VMEM((q + SUBLANES, inner), F32),
                        pltpu.VMEM((q + SUBLANES, gn), F32),
                        pltpu.VMEM((q + SUBLANES, gn), F32)],
        compiler_params=_cparams("parallel", "arbitrary"),
        name="ssd_mixer",
    )(proj, proj, proj, proj, dt_raw, conv_state8, conv_state8, conv_state8, h0,
      cw, cw, cw, cb, cb, cb, wts["dt_bias"], wts["a_log"], wts["d_skip"], wts["norm_w"], wts["expand"])
    return y, hfin


def _sc_mixer_kernel(xn_ref, wb_ref, wc_ref, wx_ref, wg_ref, st_ref, w_ref, y_ref, last_ref, buf_ref, *, tq):
    @pl.when(pl.program_id(2) == 0)
    def _():
        buf_ref[0:SUBLANES, :] = st_ref[...]

    xn = xn_ref[...]
    prod = _dot(xn, wc_ref[...]) * _dot(xn, wx_ref[...])
    buf_ref[SUBLANES:SUBLANES + tq, :] = prod
    u = w_ref[SC_WIDTH - 1:SC_WIDTH, :] * prod
    for k in range(1, SC_WIDTH):
        u = u + w_ref[SC_WIDTH - 1 - k:SC_WIDTH - k, :] * buf_ref[SUBLANES - k:SUBLANES - k + tq, :]
    tail = prod[tq - SUBLANES:tq, :]
    buf_ref[0:SUBLANES, :] = tail
    last_ref[...] = tail
    y_ref[...] = (_dot(xn, wb_ref[...]) * u * _silu(_dot(xn, wg_ref[...]))).astype(y_ref.dtype)


def sc_mixer(xn, w_sc, state8, w, *, batch, seq, tq=1024, tc=512):
    k = xn.shape[1]
    dim = w.shape[1]
    tq = _tile(seq, tq)
    nt = seq // tq
    ncb = dim // tc
    assert dim % tc == 0 and tq % SUBLANES == 0 and w_sc.shape[1] == 4 * dim

    def part(p):
        return pl.BlockSpec((k, tc), lambda j, b, t: (0, p * ncb + j))

    return pl.pallas_call(
        functools.partial(_sc_mixer_kernel, tq=tq),
        out_shape=(jax.ShapeDtypeStruct((batch * seq, dim), BF16),
                   jax.ShapeDtypeStruct((batch, SUBLANES, dim), F32)),
        grid=(ncb, batch, nt),
        in_specs=[pl.BlockSpec((tq, k), lambda j, b, t: (b * nt + t, 0)),
                  part(0), part(1), part(2), part(3),
                  pl.BlockSpec((None, SUBLANES, tc), lambda j, b, t: (b, 0, j)),
                  pl.BlockSpec((SC_WIDTH, tc), lambda j, b, t: (0, j))],
        out_specs=(pl.BlockSpec((tq, tc), lambda j, b, t: (b * nt + t, j)),
                   pl.BlockSpec((None, SUBLANES, tc), lambda j, b, t: (b, 0, j))),
        scratch_shapes=[pltpu.VMEM((tq + SUBLANES, tc), F32)],
        compiler_params=_cparams("parallel", "parallel", "arbitrary"),
        name="sc_mixer",
    )(xn, w_sc, w_sc, w_sc, w_sc, state8, w)


def _top_blocks(gate, n_valid, axis=1):
    idx = lax.broadcasted_iota(jnp.int32, gate.shape, axis).astype(F32)
    g = jnp.where(idx < jnp.asarray(n_valid, F32), gate, NEG_INF)
    sel = jnp.zeros(gate.shape, F32)
    for _ in range(MOBA_TOPK):
        m = jnp.max(g, axis=axis, keepdims=True)
        first = jnp.min(jnp.where(g == m, idx, float(gate.shape[axis])), axis=axis, keepdims=True)
        pick = idx == first
        sel = jnp.where(pick & (m > 0.5 * NEG_INF), 1.0, sel)
        g = jnp.where(pick, -jnp.inf, g)
    return sel


def _moba_prompt_kernel(slopes_ref, q_ref, k_ref, v_ref, g_ref, o_ref, kmean_ref, kb_ref, vt_ref, mval_ref, *, n_blocks):
    kv = pl.program_id(1)
    i = pl.program_id(2)
    blk = MOBA_BLOCK
    heads = range(Q_PER_KV)

    @pl.when(i == 0)
    def _():
        kmean_ref[...] = jnp.zeros(kmean_ref.shape, F32)
        for j in range(n_blocks):
            kj = k_ref[j * blk:(j + 1) * blk, :]
            kmean_ref[j:j + 1, :] = jnp.sum(kj, axis=0, keepdims=True) * (1.0 / blk)
            kb_ref[j] = kj.astype(BF16)
            vt_ref[j] = v_ref[j * blk:(j + 1) * blk, :].T.astype(BF16)

    rel = (lax.broadcasted_iota(jnp.int32, (blk, blk), 1)
           - lax.broadcasted_iota(jnp.int32, (blk, blk), 0)).astype(F32)
    blk_idx = lax.broadcasted_iota(jnp.int32, (kmean_ref.shape[0], blk), 0)
    back = (i - blk_idx).astype(F32) * float(blk)
    kmean = kmean_ref[...]

    qt, bias, diag = [], [], []
    for g in heads:
        slope = slopes_ref[kv * Q_PER_KV + g]
        q_t = (q_ref[:, g * HEAD_DIM:(g + 1) * HEAD_DIM] * (HEAD_DIM ** -0.5)).T
        gate = _dot(kmean, q_t, precision=HIGHEST)
        sel = _top_blocks(gate, i, axis=0)
        mval_ref[:, g * blk:(g + 1) * blk] = jnp.where(sel > 0.0, -slope * back, NEG_INF)
        bias.append(-slope * rel)
        diag.append(jnp.where(rel >= 0, -slope * rel, NEG_INF))
        qt.append(q_t.astype(BF16))
    qt = jnp.concatenate(qt, axis=1)
    bias = jnp.concatenate(bias, axis=1)

    s = _dot(kb_ref[i], qt) + jnp.concatenate(diag, axis=1)
    m0 = jnp.max(s, axis=0, keepdims=True)
    p = jnp.exp(s - m0)
    l0 = jnp.sum(p, axis=0, keepdims=True)
    acc0 = _dot(vt_ref[i], p.astype(BF16))

    def past_pair(t, carry):
        m, l, acc = carry
        js = (2 * t, 2 * t + 1)
        ss = [_dot(kb_ref[j], qt) + bias for j in js]
        masks = [mval_ref[pl.ds(j, 1), :] for j in js]
        m_new = m
        for s, mask in zip(ss, masks):
            m_new = jnp.maximum(m_new, jnp.max(s, axis=0, keepdims=True) + mask)
        l = jnp.exp(m - m_new) * l
        acc = jnp.exp(m - m_new) * acc
        for j, s, mask in zip(js, ss, masks):
            p = jnp.exp(s - (m_new - mask))
            l = l + jnp.sum(p, axis=0, keepdims=True)
            acc = acc + _dot(vt_ref[j], p.astype(BF16))
        return m_new, l, acc

    _, l, acc = lax.fori_loop(0, (i + 1) // 2, past_pair, (m0, l0, acc0))
    o_t = acc / l
    for g in heads:
        c = slice(g * HEAD_DIM, (g + 1) * HEAD_DIM)
        o_ref[:, c] = (o_t[:, g * blk:(g + 1) * blk].T * _silu(g_ref[:, c])).astype(o_ref.dtype)


def moba_prompt(proj, slopes, *, batch, seq):
    assert seq % MOBA_BLOCK == 0
    nb = seq // MOBA_BLOCK
    nbp = -(-nb // SUBLANES) * SUBLANES
    gw = Q_PER_KV * HEAD_DIM
    kcol = ATT_HEADS
    vcol = ATT_HEADS + KV_HEADS
    gcol = (ATT_HEADS + 2 * KV_HEADS) * HEAD_DIM // gw
    assert gcol * gw == (ATT_HEADS + 2 * KV_HEADS) * HEAD_DIM
    return pl.pallas_call(
        functools.partial(_moba_prompt_kernel, n_blocks=nb),
        out_shape=jax.ShapeDtypeStruct((batch * seq, ATT_HEADS * HEAD_DIM), BF16),
        grid=(batch, KV_HEADS, nb),
        in_specs=[pl.BlockSpec(memory_space=pltpu.SMEM),
                  pl.BlockSpec((MOBA_BLOCK, gw), lambda b, kv, i: (b * nb + i, kv)),
                  pl.BlockSpec((seq, HEAD_DIM), lambda b, kv, i: (b, kcol + kv)),
                  pl.BlockSpec((seq, HEAD_DIM), lambda b, kv, i: (b, vcol + kv)),
                  pl.BlockSpec((MOBA_BLOCK, gw), lambda b, kv, i: (b * nb + i, gcol + kv))],
        out_specs=pl.BlockSpec((MOBA_BLOCK, gw), lambda b, kv, i: (b * nb + i, kv)),
        scratch_shapes=[pltpu.VMEM((nbp, HEAD_DIM), F32),
                        pltpu.VMEM((nb, MOBA_BLOCK, HEAD_DIM), BF16),
                        pltpu.VMEM((nb, HEAD_DIM, MOBA_BLOCK), BF16),
                        pltpu.VMEM((nbp, Q_PER_KV * MOBA_BLOCK), F32)],
        compiler_params=_cparams("parallel", "parallel", "arbitrary"),
        name="moba_prompt",
    )(slopes, proj, proj, proj, proj)


def _page_specs(layer):
    return [pl.BlockSpec((None, None, PAGE_ROWS, HEAD_DIM),
                         lambda b, s, pt, p=p: (layer, pt[b, s * PAGES_PER_STEP + p], 0, 0))
            for p in range(PAGES_PER_STEP)]


def _kmean_kernel(pt_ref, *refs):
    del pt_ref
    pages, o_ref = refs[:PAGES_PER_STEP], refs[PAGES_PER_STEP]
    for bb in range(BLOCKS_PER_STEP):
        tot = None
        for pg in range(PAGES_PER_BLOCK):
            page = pages[bb * PAGES_PER_BLOCK + pg][...]
            part = jnp.sum(page.reshape(PAGE_ROWS // SUBLANES, SUBLANES, HEAD_DIM), axis=0)
            tot = part if tot is None else tot + part
        per_kv = tot[0:KV_HEADS]
        for par in range(1, SUBLANES // KV_HEADS):
            per_kv = per_kv + tot[par * KV_HEADS:(par + 1) * KV_HEADS]
        o_ref[bb] = per_kv * (1.0 / MOBA_BLOCK)


def paged_block_means(k_pools, page_table, *, layer, n_blocks):
    db = page_table.shape[0]
    assert n_blocks % BLOCKS_PER_STEP == 0 and SUBLANES % KV_HEADS == 0
    return pl.pallas_call(
        _kmean_kernel,
        out_shape=jax.ShapeDtypeStruct((db, n_blocks, KV_HEADS, HEAD_DIM), F32),
        grid_spec=pltpu.PrefetchScalarGridSpec(
            num_scalar_prefetch=1,
            grid=(db, n_blocks // BLOCKS_PER_STEP),
            in_specs=_page_specs(layer),
            out_specs=pl.BlockSpec((None, BLOCKS_PER_STEP, KV_HEADS, HEAD_DIM), lambda b, s, pt: (b, s, 0, 0))),
        compiler_params=_cparams("parallel", "arbitrary"),
        name="paged_block_means",
    )(page_table, *([k_pools] * PAGES_PER_STEP))


def _moba_sample_kernel(pt_ref, q_ref, kmean_ref, slope_ref, knew_ref, vnew_ref, *refs,
                        n_blocks, past_len, dec_seq):
    del pt_ref
    k_pages, v_pages = refs[:PAGES_PER_STEP], refs[PAGES_PER_STEP:2 * PAGES_PER_STEP]
    o_ref, m_ref, l_ref, acc_ref, sel_ref, base_ref = refs[2 * PAGES_PER_STEP:]
    step = pl.program_id(1)
    rows = q_ref.shape[0]
    rpk = rows // KV_HEADS
    kv_shift = KV_HEADS.bit_length() - 1
    q = q_ref[...] * (HEAD_DIM ** -0.5)
    qb = q.astype(BF16)
    slope = slope_ref[:, 0:1]
    lane = lax.broadcasted_iota(jnp.int32, (rows, LANES), 1)

    def key_bias(width, q_pos0, causal):
        r = lax.broadcasted_iota(jnp.int32, (rows, width), 0)
        c = lax.broadcasted_iota(jnp.int32, (rows, width), 1)
        tok = jnp.right_shift(c, kv_shift)
        t_row = jnp.bitwise_and(r, dec_seq - 1)
        same_head = jnp.bitwise_and(c, KV_HEADS - 1) == jnp.right_shift(r, rpk.bit_length() - 1)
        dist = (q_pos0 + t_row - tok).astype(F32)
        ok = same_head & (dist >= 0) if causal else same_head
        return jnp.where(ok, -slope * dist, NEG_INF)

    @pl.when(step == 0)
    def _():
        gate = jnp.concatenate([_dot_nt(q[kv * rpk:(kv + 1) * rpk, :], kmean_ref[kv], precision=HIGHEST)
                                for kv in range(KV_HEADS)], axis=0)
        sel_ref[...] = _top_blocks(gate, n_blocks)
        base_ref[...] = key_bias(PAGE_ROWS, past_len, causal=False)
        s = _dot_nt(qb, knew_ref[...].astype(BF16)) + key_bias(LANES, 0, causal=True)
        m = jnp.max(s, axis=1, keepdims=True)
        p = jnp.exp(s - m)
        m_ref[...] = jnp.broadcast_to(m, m_ref.shape)
        l_ref[...] = jnp.broadcast_to(jnp.sum(p, axis=1, keepdims=True), l_ref.shape)
        acc_ref[...] = _dot(p.astype(BF16), vnew_ref[...].astype(BF16))

    sel = sel_ref[...]
    base = base_ref[...]
    m, l, acc = m_ref[:, 0:1], l_ref[:, 0:1], acc_ref[...]
    scores, offsets = [], []
    m_new = m
    for bb in range(BLOCKS_PER_STEP):
        blk_id = step * BLOCKS_PER_STEP + bb
        chosen = jnp.sum(jnp.where(lane == blk_id, sel, 0.0), axis=1, keepdims=True) > 0.0
        for pg in range(PAGES_PER_BLOCK):
            page_pos = (blk_id * MOBA_BLOCK + pg * PAGE_SIZE).astype(F32)
            mcol = jnp.where(chosen, slope * page_pos, NEG_INF)
            s = _dot_nt(qb, k_pages[bb * PAGES_PER_BLOCK + pg][...].astype(BF16)) + base
            m_new = jnp.maximum(m_new, jnp.max(s, axis=1, keepdims=True) + mcol)
            scores.append(s)
            offsets.append(mcol)
    alpha = jnp.exp(m - m_new)
    l, acc, m = alpha * l, alpha * acc, m_new
    for pg, (s, mcol) in enumerate(zip(scores, offsets)):
        p = jnp.exp(s - (m - mcol))
        l = l + jnp.sum(p, axis=1, keepdims=True)
        acc = acc + _dot(p.astype(BF16), v_pages[pg][...].astype(BF16))
    m_ref[...] = jnp.broadcast_to(m, m_ref.shape)
    l_ref[...] = jnp.broadcast_to(l, l_ref.shape)
    acc_ref[...] = acc

    @pl.when(step == n_blocks // BLOCKS_PER_STEP - 1)
    def _():
        o_ref[...] = acc / l


def moba_sample(q_rows, kmean_t, slope_rows, k_new_pad, v_new_pad, k_pools, v_pools, page_table,
                *, layer, n_blocks, past_len, dec_seq):
    db, rows, _ = q_rows.shape
    rpk = rows // KV_HEADS
    for n in (KV_HEADS, dec_seq, rpk):
        assert n & (n - 1) == 0
    assert n_blocks % BLOCKS_PER_STEP == 0 and dec_seq * KV_HEADS <= LANES
    per_b = lambda *shape: pl.BlockSpec((None,) + shape, lambda b, s, pt: (b,) + (0,) * len(shape))
    return pl.pallas_call(
        functools.partial(_moba_sample_kernel, n_blocks=n_blocks, past_len=past_len, dec_seq=dec_seq),
        out_shape=jax.ShapeDtypeStruct((db, rows, HEAD_DIM), F32),
        grid_spec=pltpu.PrefetchScalarGridSpec(
            num_scalar_prefetch=1,
            grid=(db, n_blocks // BLOCKS_PER_STEP),
            in_specs=[per_b(rows, HEAD_DIM),
                      per_b(KV_HEADS, LANES, HEAD_DIM),
                      pl.BlockSpec((rows, LANES), lambda b, s, pt: (0, 0)),
                      per_b(LANES, HEAD_DIM),
                      per_b(LANES, HEAD_DIM)] + _page_specs(layer) + _page_specs(layer),
            out_specs=per_b(rows, HEAD_DIM),
            scratch_shapes=[pltpu.VMEM((rows, LANES), F32), pltpu.VMEM((rows, LANES), F32),
                            pltpu.VMEM((rows, HEAD_DIM), F32), pltpu.VMEM((rows, LANES), F32),
                            pltpu.VMEM((rows, PAGE_ROWS), F32)]),
        compiler_params=_cparams("parallel", "arbitrary"),
        name="moba_sample",
    )(page_table, q_rows, kmean_t, slope_rows, k_new_pad, v_new_pad,
      *([k_pools] * PAGES_PER_STEP), *([v_pools] * PAGES_PER_STEP))


def _gate_kernel(o_ref, g_ref, y_ref):
    y_ref[...] = (o_ref[...] * _silu(g_ref[...])).astype(y_ref.dtype)


def silu_gate(o, proj, *, col0, tn=1024):
    m, n = o.shape
    assert col0 % tn == 0 and n % tn == 0
    cb = col0 // tn
    return pl.pallas_call(
        _gate_kernel,
        out_shape=jax.ShapeDtypeStruct((m, n), BF16),
        grid=(n // tn,),
        in_specs=[pl.BlockSpec((m, tn), lambda j: (0, j)),
                  pl.BlockSpec((m, tn), lambda j: (0, cb + j))],
        out_specs=pl.BlockSpec((m, tn), lambda j: (0, j)),
        compiler_params=_cparams("parallel"),
        name="silu_gate",
    )(o, proj)


def _cross_layer_kernel(x_ref, nw_ref, wq_ref, k_ref, v_ref, wo_ref, o_ref):
    width = CA_HEADS * CA_HEAD_DIM
    x = x_ref[...]
    ms = jnp.mean(x * x, axis=-1, keepdims=True)
    xn = ((x * lax.rsqrt(ms + NORM_EPS)) * nw_ref[...]).astype(BF16)
    qg = _dot(xn, wq_ref[...])
    gated = []
    for h in range(CA_HEADS):
        c = slice(h * CA_HEAD_DIM, (h + 1) * CA_HEAD_DIM)
        q = (qg[:, c] * (CA_HEAD_DIM ** -0.5)).astype(BF16)
        s = _dot_nt(q, k_ref[:, c].astype(BF16))
        m = jnp.max(s, axis=1, keepdims=True)
        p = jnp.exp(s - m)
        l = jnp.sum(p, axis=1, keepdims=True)
        o = _dot(p.astype(BF16), v_ref[:, c].astype(BF16)) / l
        g = qg[:, width + h * CA_HEAD_DIM:width + (h + 1) * CA_HEAD_DIM]
        gated.append((o * _silu(g)).astype(BF16))
    o_ref[...] = x + _dot(jnp.concatenate(gated, axis=1), wo_ref[...])


def cross_layer(x, norm_w, w_q, mem_k, mem_v, w_out, *, batch, seq, tm=512):
    width = CA_HEADS * CA_HEAD_DIM
    d = x.shape[1]
    n_mem = mem_k.shape[1]
    tm = _tile(seq, tm)
    nt = seq // tm
    const = lambda i: (0, 0)
    return pl.pallas_call(
        _cross_layer_kernel,
        out_shape=jax.ShapeDtypeStruct((batch * seq, d), F32),
        grid=(batch * nt,),
        in_specs=[pl.BlockSpec((tm, d), lambda i: (i, 0)),
                  pl.BlockSpec((1, d), const),
                  pl.BlockSpec((d, 2 * width), const),
                  pl.BlockSpec((None, n_mem, width), lambda i: (i // nt, 0, 0)),
                  pl.BlockSpec((None, n_mem, width), lambda i: (i // nt, 0, 0)),
                  pl.BlockSpec((width, d), const)],
        out_specs=pl.BlockSpec((tm, d), lambda i: (i, 0)),
        compiler_params=_cparams("parallel"),
        name="cross_layer",
    )(x, norm_w.reshape(1, d), w_q, mem_k, mem_v, w_out)


def _alibi_slopes():
    return np.array([2.0 ** (-8.0 * (h + 1) / ATT_HEADS) for h in range(ATT_HEADS)], dtype=np.float32)


def _pad_rows(a, n, front=False):
    extra = n - a.shape[-2]
    pad = [(0, 0)] * a.ndim
    pad[-2] = (extra, 0) if front else (0, extra)
    return jnp.pad(a, pad)


def _pad_lanes(a):
    pad = [(0, 0)] * a.ndim
    pad[-1] = (0, LANES - a.shape[-1])
    return jnp.pad(a, pad)


def _even_layer(x, w, conv_state, ssm_state, sc_state, *, batch, seq):
    d = x.shape[1]
    inner = d
    conv_ch = inner + 2 * SSD_GROUPS * SSD_STATE
    xn = rmsnorm(x, w["norm_mix"], out_dtype=BF16)
    proj = matmul(xn, w["w_in"])
    dt_raw = matmul(xn, w["w_dt"])
    new_conv = proj.reshape(batch, seq, -1)[:, seq - (SSD_CONV - 1):, inner:inner + conv_ch]

    lpad = -(-seq // SSD_CHUNK) * SSD_CHUNK if seq > SSD_CHUNK else max(seq, SSD_CHUNK)
    if lpad != seq:
        assert seq < SSD_CHUNK
        ssd_in = _pad_rows(proj.reshape(batch, seq, -1), lpad).reshape(batch * lpad, -1)
        dt_in = _pad_rows(dt_raw.reshape(batch, seq, -1), lpad).reshape(batch * lpad, -1)
    else:
        ssd_in, dt_in = proj, dt_raw
    y, h_fin = ssd_mixer(ssd_in, dt_in, _pad_rows(conv_state, SUBLANES, front=True),
                         ssm_state.reshape(batch, inner, SSD_STATE), w,
                         batch=batch, seq=lpad, n_valid=min(seq, SSD_CHUNK))
    if lpad != seq:
        y = y.reshape(batch, lpad, inner)[:, :seq].reshape(batch * seq, inner)

    y_sc, sc_tail = sc_mixer(xn, w["w_sc"], _pad_rows(sc_state, SUBLANES, front=True), w["sc_w"],
                             batch=batch, seq=seq)
    x = matmul_residual([(y, w["w_out_ssd"]), (y_sc, w["w_out_sc"])], x)
    return (x, new_conv, h_fin.reshape(batch, inner // SSD_HEADDIM, SSD_HEADDIM, SSD_STATE),
            sc_tail[:, SUBLANES - (SC_WIDTH - 1):, :])


def _odd_prompt(x, w, slopes, *, batch, seq):
    proj = norm_matmul(x, w["norm_mix"], w["w_in"])
    att_q, att_kv = ATT_HEADS * HEAD_DIM, KV_HEADS * HEAD_DIM
    k = proj[:, att_q:att_q + att_kv].reshape(batch, seq, KV_HEADS, HEAD_DIM)
    v = proj[:, att_q + att_kv:att_q + 2 * att_kv].reshape(batch, seq, KV_HEADS, HEAD_DIM)
    og = moba_prompt(proj, slopes, batch=batch, seq=seq)
    return matmul_residual([(og, w["w_out"])], x), k, v


def _odd_sample(x, w, k_pools, v_pools, page_table, slope_rows, *, layer, batch, seq, past_len):
    assert past_len % MOBA_BLOCK == 0 and seq <= MOBA_BLOCK and seq <= LANES
    n_blocks = past_len // MOBA_BLOCK
    assert MOBA_TOPK <= n_blocks <= LANES
    proj = norm_matmul(x, w["norm_mix"], w["w_in"])
    att_q, att_kv = ATT_HEADS * HEAD_DIM, KV_HEADS * HEAD_DIM
    k_new = proj[:, att_q:att_q + att_kv]
    v_new = proj[:, att_q + att_kv:att_q + 2 * att_kv]
    q_rows = proj[:, :att_q].reshape(batch, seq, KV_HEADS, Q_PER_KV, HEAD_DIM).transpose(0, 2, 3, 1, 4)
    q_rows = q_rows.reshape(batch, ATT_HEADS * seq, HEAD_DIM)
    kmean = paged_block_means(k_pools, page_table, layer=layer, n_blocks=n_blocks)
    o = moba_sample(q_rows, _pad_rows(kmean.transpose(0, 2, 1, 3), LANES), slope_rows,
                    _pad_rows(k_new.reshape(batch, seq * KV_HEADS, HEAD_DIM), LANES),
                    _pad_rows(v_new.reshape(batch, seq * KV_HEADS, HEAD_DIM), LANES),
                    k_pools, v_pools, page_table, layer=layer, n_blocks=n_blocks, past_len=past_len, dec_seq=seq)
    o = o.reshape(batch, KV_HEADS, Q_PER_KV, seq, HEAD_DIM).transpose(0, 3, 1, 2, 4).reshape(batch * seq, att_q)
    og = silu_gate(o, proj, col0=att_q + 2 * att_kv)
    return (matmul_residual([(og, w["w_out"])], x),
            k_new.reshape(batch, seq, KV_HEADS, HEAD_DIM), v_new.reshape(batch, seq, KV_HEADS, HEAD_DIM))


def _cross_layer(x, w, mem_k, mem_v, *, batch, seq):
    return cross_layer(x, w["norm_cross"], w["ca_w_q"], mem_k, mem_v, w["ca_w_out"], batch=batch, seq=seq)


def kernel(x_prompt, x_sample, mem_prompt, cache_k, cache_v, page_table, state_conv, state_ssm, state_sc,
           cache_mem_k, cache_mem_v, norm_mix_w, norm_cross_w, norm_mem_w, final_norm_w,
           ev_w_in, ev_conv_w, ev_conv_b, ev_dt_bias, ev_a_log, ev_d_skip, ev_norm_w, ev_sc_w, ev_w_out,
           od_w_in, od_w_out, ca_w_q, ca_w_kv, ca_w_out):
    bp, sp, d = x_prompt.shape
    bs, ss, _ = x_sample.shape
    depth = norm_mix_w.shape[0]
    n_mem = mem_prompt.shape[1]
    n_heads = ev_dt_bias.shape[1]
    inner = n_heads * SSD_HEADDIM
    conv_ch = inner + 2 * SSD_GROUPS * SSD_STATE
    past_len = page_table.shape[1] * PAGE_SIZE
    ca_w = CA_HEADS * CA_HEAD_DIM
    assert inner == d and n_heads <= LANES

    slopes = jnp.asarray(_alibi_slopes())
    slope_rows = jnp.asarray(np.repeat(_alibi_slopes(), ss)[:, None] * np.ones((1, LANES), np.float32))
    expand_np = np.zeros((LANES, inner), np.float32)
    for hd in range(n_heads):
        expand_np[hd, hd * SSD_HEADDIM:(hd + 1) * SSD_HEADDIM] = 1.0
    expand = jnp.asarray(expand_np, dtype=BF16)

    xp = x_prompt.reshape(bp * sp, d)
    xs = x_sample.reshape(bs * ss, d)
    mem = mem_prompt.reshape(bp * n_mem, d)
    k_pools = cache_k.reshape(cache_k.shape[0], cache_k.shape[1], PAGE_ROWS, HEAD_DIM)
    v_pools = cache_v.reshape(cache_v.shape[0], cache_v.shape[1], PAGE_ROWS, HEAD_DIM)

    pk, pv, sk, sv = [], [], [], []
    pconv, pssm, psc, sconv, sssm, ssc = [], [], [], [], [], []
    pmk, pmv = [], []
    for l in range(depth):
        i = l // 2
        if l % 2 == 0:
            w_in = ev_w_in[i]
            dt0 = inner + conv_ch
            w = {
                "norm_mix": norm_mix_w[l],
                "w_in": w_in[:, :dt0].astype(BF16),
                "w_dt": _pad_lanes(w_in[:, dt0:dt0 + n_heads]).astype(BF16),
                "w_sc": w_in[:, dt0 + n_heads:].astype(BF16),
                "conv_w": ev_conv_w[i],
                "conv_b": ev_conv_b[i].reshape(1, conv_ch),
                "dt_bias": _pad_lanes(ev_dt_bias[i].reshape(1, n_heads)),
                "a_log": _pad_lanes(ev_a_log[i].reshape(1, n_heads)),
                "d_skip": jnp.repeat(ev_d_skip[i], SSD_HEADDIM).reshape(1, inner),
                "norm_w": ev_norm_w[i].reshape(1, inner),
                "expand": expand,
                "sc_w": ev_sc_w[i],
                "w_out_ssd": ev_w_out[i][:inner].astype(BF16),
                "w_out_sc": ev_w_out[i][inner:].astype(BF16),
            }
            xp, c1, s1, q1 = _even_layer(xp, w, jnp.zeros((bp, SSD_CONV - 1, conv_ch), F32),
                                         jnp.zeros((bp, n_heads, SSD_HEADDIM, SSD_STATE), F32),
                                         jnp.zeros((bp, SC_WIDTH - 1, d), F32), batch=bp, seq=sp)
            xs, c2, s2, q2 = _even_layer(xs, w, state_conv[i], state_ssm[i], state_sc[i], batch=bs, seq=ss)
            pconv.append(c1); pssm.append(s1); psc.append(q1)
            sconv.append(c2); sssm.append(s2); ssc.append(q2)
        else:
            w = {"norm_mix": norm_mix_w[l], "w_in": od_w_in[i].astype(BF16), "w_out": od_w_out[i].astype(BF16)}
            xp, k1, v1 = _odd_prompt(xp, w, slopes, batch=bp, seq=sp)
            xs, k2, v2 = _odd_sample(xs, w, k_pools, v_pools, page_table, slope_rows,
                                     layer=i, batch=bs, seq=ss, past_len=past_len)
            pk.append(k1); pv.append(v1); sk.append(k2); sv.append(v2)
        wc = {"norm_cross": norm_cross_w[l], "ca_w_q": ca_w_q[l].astype(BF16), "ca_w_out": ca_w_out[l].astype(BF16)}
        mkv = norm_matmul(mem, norm_mem_w[l], ca_w_kv[l].astype(BF16))
        mk = mkv[:, :ca_w].reshape(bp, n_mem, ca_w)
        mv = mkv[:, ca_w:].reshape(bp, n_mem, ca_w)
        pmk.append(mk.reshape(bp, n_mem, CA_HEADS, CA_HEAD_DIM))
        pmv.append(mv.reshape(bp, n_mem, CA_HEADS, CA_HEAD_DIM))
        xp = _cross_layer(xp, wc, mk, mv, batch=bp, seq=sp)
        xs = _cross_layer(xs, wc, cache_mem_k[l].reshape(bs, n_mem, ca_w), cache_mem_v[l].reshape(bs, n_mem, ca_w),
                          batch=bs, seq=ss)
    y_prompt = rmsnorm(xp, final_norm_w).reshape(bp, sp, d)
    y_sample = rmsnorm(xs, final_norm_w).reshape(bs, ss, d)
    return (y_prompt, y_sample,
            jnp.stack(pk), jnp.stack(pv), jnp.stack(pconv), jnp.stack(pssm), jnp.stack(psc),
            jnp.stack(pmk), jnp.stack(pmv),
            jnp.stack(sk), jnp.stack(sv), jnp.stack(sconv), jnp.stack(sssm), jnp.stack(ssc))
```

```python
import functools
import math

import numpy as np
import jax
import jax.numpy as jnp
from jax import lax
from jax.experimental import pallas as pl
from jax.experimental.pallas import tpu as pltpu

F32 = jnp.float32
BF16 = jnp.bfloat16
HIGHEST = lax.Precision.HIGHEST

NORM_EPS = 1e-5
NEG_INF = -1e30

LANES = 128
SUBLANES = 8
VMEM_LIMIT = 48 * 1024 * 1024

SSD_HEADDIM = 64
SSD_GROUPS = 4
SSD_STATE = 128
SSD_CONV = 4
SSD_CHUNK = 128
SC_WIDTH = 3
ATT_HEADS = 16
KV_HEADS = 4
HEAD_DIM = 128
Q_PER_KV = ATT_HEADS // KV_HEADS
MOBA_BLOCK = 256
MOBA_TOPK = 3
PAGE_SIZE = 128
PAGES_PER_BLOCK = MOBA_BLOCK // PAGE_SIZE
PAGE_ROWS = PAGE_SIZE * KV_HEADS
BLOCKS_PER_STEP = 8
PAGES_PER_STEP = BLOCKS_PER_STEP * PAGES_PER_BLOCK
CA_HEADS = 4
CA_HEAD_DIM = 128


def _cparams(*sem):
    return pltpu.CompilerParams(dimension_semantics=sem, vmem_limit_bytes=VMEM_LIMIT)


def _tile(n, pref):
    if n <= pref:
        return n
    t = pref
    while n % t:
        t //= 2
    return t


def _silu(x):
    return x / (1.0 + jnp.exp(-x))


def _dot(a, b, **kw):
    return jnp.dot(a, b, preferred_element_type=F32, **kw)


def _bf16_pieces(x):
    hi = x.astype(BF16)
    rest = x - hi.astype(F32)
    mid = rest.astype(BF16)
    lo = (rest - mid.astype(F32)).astype(BF16)
    return hi, mid, lo


def _dot_exact_left(x, sel):
    return sum(_dot(p, sel) for p in _bf16_pieces(x))


def _dot_exact_right(sel, x):
    return sum(_dot(sel, p) for p in _bf16_pieces(x))


def _dot_nt(a, b, **kw):
    return lax.dot_general(a, b, (((1,), (1,)), ((), ())), preferred_element_type=F32, **kw)


def _norm_mm_kernel(x_ref, nw_ref, w_ref, o_ref, xn_ref):
    @pl.when(pl.program_id(1) == 0)
    def _():
        x = x_ref[...]
        ms = jnp.mean(x * x, axis=-1, keepdims=True)
        xn_ref[...] = ((x * lax.rsqrt(ms + NORM_EPS)) * nw_ref[...]).astype(BF16)

    o_ref[...] = _dot(xn_ref[...], w_ref[...]).astype(o_ref.dtype)


def _norm_mm_tap_kernel(x_ref, nw_ref, w_ref, o_ref, t0_ref, t1_ref, xn_ref, *, tap_tile):
    _norm_mm_kernel(x_ref, nw_ref, w_ref, o_ref, xn_ref)

    @pl.when(pl.program_id(1) == tap_tile)
    def _():
        half = t0_ref.shape[1]
        t0_ref[...] = o_ref[:, :half]
        t1_ref[...] = o_ref[:, half:]


def norm_matmul(x, norm_w, w, *, tm=1024, tn=1024, out_dtype=F32, tap_col0=None):
    m, k = x.shape
    n = w.shape[1]
    tm, tn = _tile(m, tm), _tile(n, tn)
    in_specs = [pl.BlockSpec((tm, k), lambda i, j: (i, 0)),
                pl.BlockSpec((1, k), lambda i, j: (0, 0)),
                pl.BlockSpec((k, tn), lambda i, j: (0, j))]
    out_spec = pl.BlockSpec((tm, tn), lambda i, j: (i, j))
    out_shape = jax.ShapeDtypeStruct((m, n), out_dtype)
    if tap_col0 is None:
        body, out_specs, out_shapes = _norm_mm_kernel, out_spec, out_shape
    else:
        assert tap_col0 % tn == 0 and tn % (2 * LANES) == 0
        body = functools.partial(_norm_mm_tap_kernel, tap_tile=tap_col0 // tn)
        tap_spec = pl.BlockSpec((tm, tn // 2), lambda i, j: (i, 0))
        tap_shape = jax.ShapeDtypeStruct((m, tn // 2), out_dtype)
        out_specs, out_shapes = (out_spec, tap_spec, tap_spec), (out_shape, tap_shape, tap_shape)
    return pl.pallas_call(
        body,
        out_shape=out_shapes,
        grid=(m // tm, n // tn),
        in_specs=in_specs,
        out_specs=out_specs,
        scratch_shapes=[pltpu.VMEM((tm, k), BF16)],
        compiler_params=_cparams("parallel", "arbitrary"),
        name="norm_matmul",
    )(x, norm_w.reshape(1, k), w)


def _mm_res_kernel(*refs, n_pairs):
    res_ref, o_ref = refs[2 * n_pairs], refs[2 * n_pairs + 1]
    acc = res_ref[...]
    for p in range(n_pairs):
        acc = acc + _dot(refs[2 * p][...], refs[2 * p + 1][...])
    o_ref[...] = acc


def matmul_residual(pairs, res, *, tm=512):
    m, n = res.shape
    tm = _tile(m, tm)
    in_specs, args = [], []
    for a, w in pairs:
        k = a.shape[1]
        in_specs += [pl.BlockSpec((tm, k), lambda i: (i, 0)),
                     pl.BlockSpec((k, n), lambda i: (0, 0), pipeline_mode=pl.Buffered(1))]
        args += [a, w]
    in_specs.append(pl.BlockSpec((tm, n), lambda i: (i, 0)))
    return pl.pallas_call(
        functools.partial(_mm_res_kernel, n_pairs=len(pairs)),
        out_shape=jax.ShapeDtypeStruct((m, n), F32),
        grid=(m // tm,),
        in_specs=in_specs,
        out_specs=pl.BlockSpec((tm, n), lambda i: (i, 0)),
        compiler_params=_cparams("parallel"),
        name="matmul_residual",
    )(*args, res)


def _mm_kernel(a_ref, w_ref, o_ref):
    o_ref[...] = _dot(a_ref[...], w_ref[...]).astype(o_ref.dtype)


def matmul(a, w, *, col0=0, n=None, tm=1024, tn=1024, out_dtype=F32):
    m, k = a.shape
    n = w.shape[1] - col0 if n is None else n
    tm, tn = _tile(m, tm), _tile(n, tn)
    assert col0 % tn == 0 and col0 + n <= w.shape[1]
    cb0 = col0 // tn
    return pl.pallas_call(
        _mm_kernel,
        out_shape=jax.ShapeDtypeStruct((m, n), out_dtype),
        grid=(m // tm, n // tn),
        in_specs=[pl.BlockSpec((tm, k), lambda i, j: (i, 0)),
                  pl.BlockSpec((k, tn), lambda i, j: (0, cb0 + j))],
        out_specs=pl.BlockSpec((tm, tn), lambda i, j: (i, j)),
        compiler_params=_cparams("parallel", "parallel"),
        name="matmul",
    )(a, w)


def _rmsnorm_kernel(x_ref, nw_ref, o_ref):
    x = x_ref[...]
    ms = jnp.mean(x * x, axis=-1, keepdims=True)
    o_ref[...] = ((x * lax.rsqrt(ms + NORM_EPS)) * nw_ref[...]).astype(o_ref.dtype)


def rmsnorm(x, norm_w, *, tm=512, out_dtype=F32):
    m, k = x.shape
    tm = _tile(m, tm)
    return pl.pallas_call(
        _rmsnorm_kernel,
        out_shape=jax.ShapeDtypeStruct((m, k), out_dtype),
        grid=(m // tm,),
        in_specs=[pl.BlockSpec((tm, k), lambda i: (i, 0)),
                  pl.BlockSpec((1, k), lambda i: (0, 0))],
        out_specs=pl.BlockSpec((tm, k), lambda i: (i, 0)),
        compiler_params=_cparams("parallel"),
        name="rmsnorm",
    )(x, norm_w.reshape(1, k))


def _causal_conv_chunk(x_ref, buf_ref, cw_ref, cb_ref, q):
    x = x_ref[...]
    buf_ref[SUBLANES:SUBLANES + q, :] = x
    y = cw_ref[SSD_CONV - 1:SSD_CONV, :] * x
    for k in range(1, SSD_CONV):
        y = y + cw_ref[SSD_CONV - 1 - k:SSD_CONV - k, :] * buf_ref[SUBLANES - k:SUBLANES - k + q, :]
    if cb_ref is not None:
        y = y + cb_ref[...]
    buf_ref[0:SUBLANES, :] = x[q - SUBLANES:q, :]
    return y


def _ssd_kernel(z_ref, xs_ref, b_ref, c_ref, dt_ref, cs_xs_ref, cs_b_ref, cs_c_ref, h0_ref,
                cw_xs_ref, cw_b_ref, cw_c_ref, cb_xs_ref, cb_b_ref, cb_c_ref,
                dtb_ref, alog_ref, dskip_ref, nw_ref, e_ref,
                y_ref, hfin_ref,
                ht_ref, buf_xs, buf_b, buf_c, *, q, n_valid, n_chunks):
    c = pl.program_id(1)
    n_heads = e_ref.shape[1] // SSD_HEADDIM
    gw = (n_heads // SSD_GROUPS) * SSD_HEADDIM

    @pl.when(c == 0)
    def _():
        ht_ref[...] = h0_ref[...].T
        buf_xs[0:SUBLANES, :] = cs_xs_ref[...]
        buf_b[0:SUBLANES, :] = cs_b_ref[...]
        buf_c[0:SUBLANES, :] = cs_c_ref[...]

    xs = _silu(_causal_conv_chunk(xs_ref, buf_xs, cw_xs_ref, cb_xs_ref, q))
    bm = _silu(_causal_conv_chunk(b_ref, buf_b, cw_b_ref, cb_b_ref, q))
    cm = _silu(_causal_conv_chunk(c_ref, buf_c, cw_c_ref, cb_c_ref, q))

    row = lax.broadcasted_iota(jnp.int32, (q, q), 0)
    col = lax.broadcasted_iota(jnp.int32, (q, q), 1)
    tril = row >= col

    dtv = dt_ref[...] + dtb_ref[...]
    dt = jnp.maximum(dtv, 0.0) + jnp.log(1.0 + jnp.exp(-jnp.abs(dtv)))
    if n_valid < q:
        dt = jnp.where(lax.broadcasted_iota(jnp.int32, dt.shape, 0) < n_valid, dt, 0.0)
    a = dt * (-jnp.exp(alog_ref[...]))
    a_cs = _dot_exact_right(jnp.where(tril, 1.0, 0.0).astype(BF16), a)
    a_cs_t = a_cs.T
    expand = e_ref[...]
    dt_full = _dot_exact_left(dt, expand)
    acs_full = _dot_exact_left(a_cs, expand)
    tot_full = acs_full[q - 1:q, :]
    xr = xs * dt_full
    xr_dec = (xr * jnp.exp(tot_full - acs_full)).astype(BF16)
    exp_acs = jnp.exp(acs_full)
    exp_tot = jnp.exp(tot_full)

    lane = lax.broadcasted_iota(jnp.int32, (q, LANES), 1)
    lo_half = lane < SSD_HEADDIM
    heads_per_tile = LANES // SSD_HEADDIM

    y = dskip_ref[...] * xs
    y_parts = []
    for g in range(SSD_GROUPS):
        bg = bm[:, g * SSD_STATE:(g + 1) * SSD_STATE]
        cg = cm[:, g * SSD_STATE:(g + 1) * SSD_STATE].astype(BF16)
        bg_t = bg.T.astype(BF16)
        cb = _dot(cg, bg_t)
        ht_g = ht_ref[:, g * gw:(g + 1) * gw]
        y_g = _dot(cg, ht_g.astype(BF16)) * exp_acs[:, g * gw:(g + 1) * gw]
        tiles = []
        for t in range(gw // LANES):
            base = g * gw + t * LANES
            xr_t = xr[:, base:base + LANES]
            acc = None
            for e in range(heads_per_tile):
                h = base // SSD_HEADDIM + e
                seg = a_cs[:, h:h + 1] - a_cs_t[h:h + 1, :]
                lmat = jnp.where(tril, jnp.exp(seg), 0.0)
                in_head = lo_half if e == 0 else jnp.logical_not(lo_half)
                part = _dot((cb * lmat).astype(BF16), jnp.where(in_head, xr_t, 0.0).astype(BF16))
                acc = part if acc is None else acc + part
            tiles.append(acc)
        y_g = y_g + jnp.concatenate(tiles, axis=1)
        states = _dot(bg_t, xr_dec[:, g * gw:(g + 1) * gw])
        ht_ref[:, g * gw:(g + 1) * gw] = exp_tot[:, g * gw:(g + 1) * gw] * ht_g + states
        y_parts.append(y_g)
    y = y + jnp.concatenate(y_parts, axis=1)

    y = y * _silu(z_ref[...])
    for g in range(SSD_GROUPS):
        yg = y[:, g * gw:(g + 1) * gw]
        ms = jnp.mean(yg * yg, axis=-1, keepdims=True)
        y_ref[:, g * gw:(g + 1) * gw] = ((yg * lax.rsqrt(ms + NORM_EPS)) * nw_ref[:, g * gw:(g + 1) * gw]).astype(y_ref.dtype)

    @pl.when(c == n_chunks - 1)
    def _():
        hfin_ref[...] = ht_ref[...].T


def ssd_mixer(proj, dt_raw, conv_state8, h0, wts, *, batch, seq, n_valid):
    q = min(SSD_CHUNK, seq)
    assert seq % q == 0 and q % SUBLANES == 0
    nc = seq // q
    assert n_valid == q or nc == 1
    inner = h0.shape[1]
    gn = SSD_GROUPS * SSD_STATE
    assert inner % gn == 0 and (inner // gn) * gn == inner
    kb = inner // gn
    rows = lambda b, c: b * nc + c
    full = lambda b, c: (0, 0)
    in_specs = [
        pl.BlockSpec((q, inner), lambda b, c: (rows(b, c), 0)),
        pl.BlockSpec((q, inner), lambda b, c: (rows(b, c), 1)),
        pl.BlockSpec((q, gn), lambda b, c: (rows(b, c), 2 * kb)),
        pl.BlockSpec((q, gn), lambda b, c: (rows(b, c), 2 * kb + 1)),
        pl.BlockSpec((q, LANES), lambda b, c: (rows(b, c), 0)),
        pl.BlockSpec((None, SUBLANES, inner), lambda b, c: (b, 0, 0)),
        pl.BlockSpec((None, SUBLANES, gn), lambda b, c: (b, 0, kb)),
        pl.BlockSpec((None, SUBLANES, gn), lambda b, c: (b, 0, kb + 1)),
        pl.BlockSpec((None, inner, SSD_STATE), lambda b, c: (b, 0, 0)),
        pl.BlockSpec((SSD_CONV, inner), full),
        pl.BlockSpec((SSD_CONV, gn), lambda b, c: (0, kb)),
        pl.BlockSpec((SSD_CONV, gn), lambda b, c: (0, kb + 1)),
        pl.BlockSpec((1, inner), full),
        pl.BlockSpec((1, gn), lambda b, c: (0, kb)),
        pl.BlockSpec((1, gn), lambda b, c: (0, kb + 1)),
        pl.BlockSpec((1, LANES), full),
        pl.BlockSpec((1, LANES), full),
        pl.BlockSpec((1, inner), full),
        pl.BlockSpec((1, inner), full),
        pl.BlockSpec((LANES, inner), full),
    ]
    cw, cb = wts["conv_w"], wts["conv_b"]
    y, hfin = pl.pallas_call(
        functools.partial(_ssd_kernel, q=q, n_valid=n_valid, n_chunks=nc),
        out_shape=(jax.ShapeDtypeStruct((batch * seq, inner), BF16),
                   jax.ShapeDtypeStruct((batch, inner, SSD_STATE), F32)),
        grid=(batch, nc),
        in_specs=in_specs,
        out_specs=(pl.BlockSpec((q, inner), lambda b, c: (rows(b, c), 0)),
                   pl.BlockSpec((None, inner, SSD_STATE), lambda b, c: (b, 0, 0))),
        scratch_shapes=[pltpu.VMEM((SSD_STATE, inner), F32),
                        pltpu.VMEM((q + SUBLANES, inner), F32),
                        pltpu.VMEM((q + SUBLANES, gn), F32),
                        pltpu.VMEM((q + SUBLANES, gn), F32)],
        compiler_params=_cparams("parallel", "arbitrary"),
        name="ssd_mixer",
    )(proj, proj, proj, proj, dt_raw, conv_state8, conv_state8, conv_state8, h0,
      cw, cw, cw, cb, cb, cb, wts["dt_bias"], wts["a_log"], wts["d_skip"], wts["norm_w"], wts["expand"])
    return y, hfin


def _sc_mixer_kernel(xn_ref, wb_ref, wc_ref, wx_ref, wg_ref, st_ref, w_ref, y_ref, last_ref, buf_ref, *, tq):
    @pl.when(pl.program_id(2) == 0)
    def _():
        buf_ref[0:SUBLANES, :] = st_ref[...]

    xn = xn_ref[...]
    prod = _dot(xn, wc_ref[...]) * _dot(xn, wx_ref[...])
    buf_ref[SUBLANES:SUBLANES + tq, :] = prod
    u = w_ref[SC_WIDTH - 1:SC_WIDTH, :] * prod
    for k in range(1, SC_WIDTH):
        u = u + w_ref[SC_WIDTH - 1 - k:SC_WIDTH - k, :] * buf_ref[SUBLANES - k:SUBLANES - k + tq, :]
    tail = prod[tq - SUBLANES:tq, :]
    buf_ref[0:SUBLANES, :] = tail
    last_ref[...] = tail
    y_ref[...] = (_dot(xn, wb_ref[...]) * u * _silu(_dot(xn, wg_ref[...]))).astype(y_ref.dtype)


def sc_mixer(xn, w_sc, state8, w, *, batch, seq, tq=1024, tc=512):
    k = xn.shape[1]
    dim = w.shape[1]
    tq = _tile(seq, tq)
    nt = seq // tq
    ncb = dim // tc
    assert dim % tc == 0 and tq % SUBLANES == 0 and w_sc.shape[1] == 4 * dim

    def part(p):
        return pl.BlockSpec((k, tc), lambda j, b, t: (0, p * ncb + j))

    return pl.pallas_call(
        functools.partial(_sc_mixer_kernel, tq=tq),
        out_shape=(jax.ShapeDtypeStruct((batch * seq, dim), BF16),
                   jax.ShapeDtypeStruct((batch, SUBLANES, dim), F32)),
        grid=(ncb, batch, nt),
        in_specs=[pl.BlockSpec((tq, k), lambda j, b, t: (b * nt + t, 0)),
                  part(0), part(1), part(2), part(3),
                  pl.BlockSpec((None, SUBLANES, tc), lambda j, b, t: (b, 0, j)),
                  pl.BlockSpec((SC_WIDTH, tc), lambda j, b, t: (0, j))],
        out_specs=(pl.BlockSpec((tq, tc), lambda j, b, t: (b * nt + t, j)),
                   pl.BlockSpec((None, SUBLANES, tc), lambda j, b, t: (b, 0, j))),
        scratch_shapes=[pltpu.VMEM((tq + SUBLANES, tc), F32)],
        compiler_params=_cparams("parallel", "parallel", "arbitrary"),
        name="sc_mixer",
    )(xn, w_sc, w_sc, w_sc, w_sc, state8, w)


def _top_blocks(gate, n_valid, axis=1):
    idx = lax.broadcasted_iota(jnp.int32, gate.shape, axis).astype(F32)
    g = jnp.where(idx < jnp.asarray(n_valid, F32), gate, NEG_INF)
    sel = jnp.zeros(gate.shape, F32)
    for _ in range(MOBA_TOPK):
        m = jnp.max(g, axis=axis, keepdims=True)
        first = jnp.min(jnp.where(g == m, idx, float(gate.shape[axis])), axis=axis, keepdims=True)
        pick = idx == first
        sel = jnp.where(pick & (m > 0.5 * NEG_INF), 1.0, sel)
        g = jnp.where(pick, -jnp.inf, g)
    return sel


def _moba_prompt_kernel(slopes_ref, q_ref, k_ref, v_ref, g_ref, o_ref, kmean_ref, kb_ref, vt_ref, mval_ref, *, n_blocks):
    kv = pl.program_id(1)
    i = pl.program_id(2)
    blk = MOBA_BLOCK
    heads = range(Q_PER_KV)

    @pl.when(i == 0)
    def _():
        kmean_ref[...] = jnp.zeros(kmean_ref.shape, F32)
        for j in range(n_blocks):
            kj = k_ref[j * blk:(j + 1) * blk, :]
            kmean_ref[j:j + 1, :] = jnp.sum(kj, axis=0, keepdims=True) * (1.0 / blk)
            kb_ref[j] = kj.astype(BF16)
            vt_ref[j] = v_ref[j * blk:(j + 1) * blk, :].T.astype(BF16)

    rel = (lax.broadcasted_iota(jnp.int32, (blk, blk), 1)
           - lax.broadcasted_iota(jnp.int32, (blk, blk), 0)).astype(F32)
    blk_idx = lax.broadcasted_iota(jnp.int32, (kmean_ref.shape[0], blk), 0)
    back = (i - blk_idx).astype(F32) * float(blk)
    kmean = kmean_ref[...]

    qt, bias, diag = [], [], []
    for g in heads:
        slope = slopes_ref[kv * Q_PER_KV + g]
        q_t = (q_ref[:, g * HEAD_DIM:(g + 1) * HEAD_DIM] * (HEAD_DIM ** -0.5)).T
        gate = _dot(kmean, q_t, precision=HIGHEST)
        sel = _top_blocks(gate, i, axis=0)
        mval_ref[:, g * blk:(g + 1) * blk] = jnp.where(sel > 0.0, -slope * back, NEG_INF)
        bias.append(-slope * rel)
        diag.append(jnp.where(rel >= 0, -slope * rel, NEG_INF))
        qt.append(q_t.astype(BF16))
    qt = jnp.concatenate(qt, axis=1)
    bias = jnp.concatenate(bias, axis=1)

    s = _dot(kb_ref[i], qt) + jnp.concatenate(diag, axis=1)
    m0 = jnp.max(s, axis=0, keepdims=True)
    p = jnp.exp(s - m0)
    l0 = jnp.sum(p, axis=0, keepdims=True)
    acc0 = _dot(vt_ref[i], p.astype(BF16))

    def past_pair(t, carry):
        m, l, acc = carry
        js = (2 * t, 2 * t + 1)
        ss = [_dot(kb_ref[j], qt) + bias for j in js]
        masks = [mval_ref[pl.ds(j, 1), :] for j in js]
        m_new = m
        for s, mask in zip(ss, masks):
            m_new = jnp.maximum(m_new, jnp.max(s, axis=0, keepdims=True) + mask)
        l = jnp.exp(m - m_new) * l
        acc = jnp.exp(m - m_new) * acc
        for j, s, mask in zip(js, ss, masks):
            p = jnp.exp(s - (m_new - mask))
            l = l + jnp.sum(p, axis=0, keepdims=True)
            acc = acc + _dot(vt_ref[j], p.astype(BF16))
        return m_new, l, acc

    _, l, acc = lax.fori_loop(0, (i + 1) // 2, past_pair, (m0, l0, acc0))
    o_t = acc / l
    for g in heads:
        c = slice(g * HEAD_DIM, (g + 1) * HEAD_DIM)
        o_ref[:, c] = (o_t[:, g * blk:(g + 1) * blk].T * _silu(g_ref[:, c])).astype(o_ref.dtype)


def moba_prompt(proj, slopes, *, batch, seq):
    assert seq % MOBA_BLOCK == 0
    nb = seq // MOBA_BLOCK
    nbp = -(-nb // SUBLANES) * SUBLANES
    gw = Q_PER_KV * HEAD_DIM
    kcol = ATT_HEADS
    vcol = ATT_HEADS + KV_HEADS
    gcol = (ATT_HEADS + 2 * KV_HEADS) * HEAD_DIM // gw
    assert gcol * gw == (ATT_HEADS + 2 * KV_HEADS) * HEAD_DIM
    return pl.pallas_call(
        functools.partial(_moba_prompt_kernel, n_blocks=nb),
        out_shape=jax.ShapeDtypeStruct((batch * seq, ATT_HEADS * HEAD_DIM), BF16),
        grid=(batch, KV_HEADS, nb),
        in_specs=[pl.BlockSpec(memory_space=pltpu.SMEM),
                  pl.BlockSpec((MOBA_BLOCK, gw), lambda b, kv, i: (b * nb + i, kv)),
                  pl.BlockSpec((seq, HEAD_DIM), lambda b, kv, i: (b, kcol + kv)),
                  pl.BlockSpec((seq, HEAD_DIM), lambda b, kv, i: (b, vcol + kv)),
                  pl.BlockSpec((MOBA_BLOCK, gw), lambda b, kv, i: (b * nb + i, gcol + kv))],
        out_specs=pl.BlockSpec((MOBA_BLOCK, gw), lambda b, kv, i: (b * nb + i, kv)),
        scratch_shapes=[pltpu.VMEM((nbp, HEAD_DIM), F32),
                        pltpu.VMEM((nb, MOBA_BLOCK, HEAD_DIM), BF16),
                        pltpu.VMEM((nb, HEAD_DIM, MOBA_BLOCK), BF16),
                        pltpu.VMEM((nbp, Q_PER_KV * MOBA_BLOCK), F32)],
        compiler_params=_cparams("parallel", "parallel", "arbitrary"),
        name="moba_prompt",
    )(slopes, proj, proj, proj, proj)


def _page_specs(layer):
    return [pl.BlockSpec((None, None, PAGE_ROWS, HEAD_DIM),
                         lambda b, s, pt, p=p: (layer, pt[b, s * PAGES_PER_STEP + p], 0, 0))
            for p in range(PAGES_PER_STEP)]


def _kmean_kernel(pt_ref, *refs):
    del pt_ref
    pages, o_ref = refs[:PAGES_PER_STEP], refs[PAGES_PER_STEP]
    for bb in range(BLOCKS_PER_STEP):
        tot = None
        for pg in range(PAGES_PER_BLOCK):
            page = pages[bb * PAGES_PER_BLOCK + pg][...]
            part = jnp.sum(page.reshape(PAGE_ROWS // SUBLANES, SUBLANES, HEAD_DIM), axis=0)
            tot = part if tot is None else tot + part
        per_kv = tot[0:KV_HEADS]
        for par in range(1, SUBLANES // KV_HEADS):
            per_kv = per_kv + tot[par * KV_HEADS:(par + 1) * KV_HEADS]
        o_ref[bb] = per_kv * (1.0 / MOBA_BLOCK)


def paged_block_means(k_pools, page_table, *, layer, n_blocks):
    db = page_table.shape[0]
    assert n_blocks % BLOCKS_PER_STEP == 0 and SUBLANES % KV_HEADS == 0
    return pl.pallas_call(
        _kmean_kernel,
        out_shape=jax.ShapeDtypeStruct((db, n_blocks, KV_HEADS, HEAD_DIM), F32),
        grid_spec=pltpu.PrefetchScalarGridSpec(
            num_scalar_prefetch=1,
            grid=(db, n_blocks // BLOCKS_PER_STEP),
            in_specs=_page_specs(layer),
            out_specs=pl.BlockSpec((None, BLOCKS_PER_STEP, KV_HEADS, HEAD_DIM), lambda b, s, pt: (b, s, 0, 0))),
        compiler_params=_cparams("parallel", "arbitrary"),
        name="paged_block_means",
    )(page_table, *([k_pools] * PAGES_PER_STEP))


def _moba_sample_kernel(pt_ref, q_ref, kmean_ref, slope_ref, knew_ref, vnew_ref, *refs,
                        n_blocks, past_len, dec_seq):
    del pt_ref
    k_pages, v_pages = refs[:PAGES_PER_STEP], refs[PAGES_PER_STEP:2 * PAGES_PER_STEP]
    o_ref, m_ref, l_ref, acc_ref, sel_ref, base_ref = refs[2 * PAGES_PER_STEP:]
    step = pl.program_id(1)
    rows = q_ref.shape[0]
    rpk = rows // KV_HEADS
    kv_shift = KV_HEADS.bit_length() - 1
    q = q_ref[...] * (HEAD_DIM ** -0.5)
    qb = q.astype(BF16)
    slope = slope_ref[:, 0:1]
    lane = lax.broadcasted_iota(jnp.int32, (rows, LANES), 1)

    def key_bias(width, q_pos0, causal):
        r = lax.broadcasted_iota(jnp.int32, (rows, width), 0)
        c = lax.broadcasted_iota(jnp.int32, (rows, width), 1)
        tok = jnp.right_shift(c, kv_shift)
        t_row = jnp.bitwise_and(r, dec_seq - 1)
        same_head = jnp.bitwise_and(c, KV_HEADS - 1) == jnp.right_shift(r, rpk.bit_length() - 1)
        dist = (q_pos0 + t_row - tok).astype(F32)
        ok = same_head & (dist >= 0) if causal else same_head
        return jnp.where(ok, -slope * dist, NEG_INF)

    @pl.when(step == 0)
    def _():
        gate = jnp.concatenate([_dot_nt(q[kv * rpk:(kv + 1) * rpk, :], kmean_ref[kv], precision=HIGHEST)
                                for kv in range(KV_HEADS)], axis=0)
        sel_ref[...] = _top_blocks(gate, n_blocks)
        base_ref[...] = key_bias(PAGE_ROWS, past_len, causal=False)
        s = _dot_nt(qb, knew_ref[...].astype(BF16)) + key_bias(LANES, 0, causal=True)
        m = jnp.max(s, axis=1, keepdims=True)
        p = jnp.exp(s - m)
        m_ref[...] = jnp.broadcast_to(m, m_ref.shape)
        l_ref[...] = jnp.broadcast_to(jnp.sum(p, axis=1, keepdims=True), l_ref.shape)
        acc_ref[...] = _dot(p.astype(BF16), vnew_ref[...].astype(BF16))

    sel = sel_ref[...]
    base = base_ref[...]
    m, l, acc = m_ref[:, 0:1], l_ref[:, 0:1], acc_ref[...]
    scores, offsets = [], []
    m_new = m
    for bb in range(BLOCKS_PER_STEP):
        blk_id = step * BLOCKS_PER_STEP + bb
        chosen = jnp.sum(jnp.where(lane == blk_id, sel, 0.0), axis=1, keepdims=True) > 0.0
        for pg in range(PAGES_PER_BLOCK):
            page_pos = (blk_id * MOBA_BLOCK + pg * PAGE_SIZE).astype(F32)
            mcol = jnp.where(chosen, slope * page_pos, NEG_INF)
            s = _dot_nt(qb, k_pages[bb * PAGES_PER_BLOCK + pg][...].astype(BF16)) + base
            m_new = jnp.maximum(m_new, jnp.max(s, axis=1, keepdims=True) + mcol)
            scores.append(s)
            offsets.append(mcol)
    alpha = jnp.exp(m - m_new)
    l, acc, m = alpha * l, alpha * acc, m_new
    for pg, (s, mcol) in enumerate(zip(scores, offsets)):
        p = jnp.exp(s - (m - mcol))
        l = l + jnp.sum(p, axis=1, keepdims=True)
        acc = acc + _dot(p.astype(BF16), v_pages[pg][...].astype(BF16))
    m_ref[...] = jnp.broadcast_to(m, m_ref.shape)
    l_ref[...] = jnp.broadcast_to(l, l_ref.shape)
    acc_ref[...] = acc

    @pl.when(step == n_blocks // BLOCKS_PER_STEP - 1)
    def _():
        o_ref[...] = acc / l


def moba_sample(q_rows, kmean_t, slope_rows, k_new_pad, v_new_pad, k_pools, v_pools, page_table,
                *, layer, n_blocks, past_len, dec_seq):
    db, rows, _ = q_rows.shape
    rpk = rows // KV_HEADS
    for n in (KV_HEADS, dec_seq, rpk):
        assert n & (n - 1) == 0
    assert n_blocks % BLOCKS_PER_STEP == 0 and dec_seq * KV_HEADS <= LANES
    per_b = lambda *shape: pl.BlockSpec((None,) + shape, lambda b, s, pt: (b,) + (0,) * len(shape))
    return pl.pallas_call(
        functools.partial(_moba_sample_kernel, n_blocks=n_blocks, past_len=past_len, dec_seq=dec_seq),
        out_shape=jax.ShapeDtypeStruct((db, rows, HEAD_DIM), F32),
        grid_spec=pltpu.PrefetchScalarGridSpec(
            num_scalar_prefetch=1,
            grid=(db, n_blocks // BLOCKS_PER_STEP),
            in_specs=[per_b(rows, HEAD_DIM),
                      per_b(KV_HEADS, LANES, HEAD_DIM),
                      pl.BlockSpec((rows, LANES), lambda b, s, pt: (0, 0)),
                      per_b(LANES, HEAD_DIM),
                      per_b(LANES, HEAD_DIM)] + _page_specs(layer) + _page_specs(layer),
            out_specs=per_b(rows, HEAD_DIM),
            scratch_shapes=[pltpu.VMEM((rows, LANES), F32), pltpu.VMEM((rows, LANES), F32),
                            pltpu.VMEM((rows, HEAD_DIM), F32), pltpu.VMEM((rows, LANES), F32),
                            pltpu.VMEM((rows, PAGE_ROWS), F32)]),
        compiler_params=_cparams("parallel", "arbitrary"),
        name="moba_sample",
    )(page_table, q_rows, kmean_t, slope_rows, k_new_pad, v_new_pad,
      *([k_pools] * PAGES_PER_STEP), *([v_pools] * PAGES_PER_STEP))


def _gate_kernel(o_ref, g_ref, y_ref):
    y_ref[...] = (o_ref[...] * _silu(g_ref[...])).astype(y_ref.dtype)


def silu_gate(o, proj, *, col0, tn=1024):
    m, n = o.shape
    assert col0 % tn == 0 and n % tn == 0
    cb = col0 // tn
    return pl.pallas_call(
        _gate_kernel,
        out_shape=jax.ShapeDtypeStruct((m, n), BF16),
        grid=(n // tn,),
        in_specs=[pl.BlockSpec((m, tn), lambda j: (0, j)),
                  pl.BlockSpec((m, tn), lambda j: (0, cb + j))],
        out_specs=pl.BlockSpec((m, tn), lambda j: (0, j)),
        compiler_params=_cparams("parallel"),
        name="silu_gate",
    )(o, proj)


def _cross_layer_kernel(x_ref, nw_ref, wq_ref, k_ref, v_ref, wo_ref, o_ref):
    width = CA_HEADS * CA_HEAD_DIM
    x = x_ref[...]
    ms = jnp.mean(x * x, axis=-1, keepdims=True)
    xn = ((x * lax.rsqrt(ms + NORM_EPS)) * nw_ref[...]).astype(BF16)
    qg = _dot(xn, wq_ref[...])
    gated = []
    for h in range(CA_HEADS):
        c = slice(h * CA_HEAD_DIM, (h + 1) * CA_HEAD_DIM)
        q = (qg[:, c] * (CA_HEAD_DIM ** -0.5)).astype(BF16)
        s = _dot_nt(q, k_ref[:, c].astype(BF16))
        m = jnp.max(s, axis=1, keepdims=True)
        p = jnp.exp(s - m)
        l = jnp.sum(p, axis=1, keepdims=True)
        o = _dot(p.astype(BF16), v_ref[:, c].astype(BF16)) / l
        g = qg[:, width + h * CA_HEAD_DIM:width + (h + 1) * CA_HEAD_DIM]
        gated.append((o * _silu(g)).astype(BF16))
    o_ref[...] = x + _dot(jnp.concatenate(gated, axis=1), wo_ref[...])


def cross_layer(x, norm_w, w_q, mem_k, mem_v, w_out, *, batch, seq, tm=512):
    width = CA_HEADS * CA_HEAD_DIM
    d = x.shape[1]
    n_mem = mem_k.shape[1]
    tm = _tile(seq, tm)
    nt = seq // tm
    const = lambda i: (0, 0)
    return pl.pallas_call(
        _cross_layer_kernel,
        out_shape=jax.ShapeDtypeStruct((batch * seq, d), F32),
        grid=(batch * nt,),
        in_specs=[pl.BlockSpec((tm, d), lambda i: (i, 0)),
                  pl.BlockSpec((1, d), const),
                  pl.BlockSpec((d, 2 * width), const),
                  pl.BlockSpec((None, n_mem, width), lambda i: (i // nt, 0, 0)),
                  pl.BlockSpec((None, n_mem, width), lambda i: (i // nt, 0, 0)),
                  pl.BlockSpec((width, d), const)],
        out_specs=pl.BlockSpec((tm, d), lambda i: (i, 0)),
        compiler_params=_cparams("parallel"),
        name="cross_layer",
    )(x, norm_w.reshape(1, d), w_q, mem_k, mem_v, w_out)


def _alibi_slopes():
    return np.array([2.0 ** (-8.0 * (h + 1) / ATT_HEADS) for h in range(ATT_HEADS)], dtype=np.float32)


def _pad_rows(a, n, front=False):
    extra = n - a.shape[-2]
    pad = [(0, 0)] * a.ndim
    pad[-2] = (extra, 0) if front else (0, extra)
    return jnp.pad(a, pad)


def _pad_lanes(a):
    pad = [(0, 0)] * a.ndim
    pad[-1] = (0, LANES - a.shape[-1])
    return jnp.pad(a, pad)


def _even_layer(x, w, conv_state, ssm_state, sc_state, *, batch, seq):
    d = x.shape[1]
    inner = d
    conv_ch = inner + 2 * SSD_GROUPS * SSD_STATE
    xn = rmsnorm(x, w["norm_mix"], out_dtype=BF16)
    proj = matmul(xn, w["w_in"], n=inner + conv_ch)
    dt_raw = matmul(xn, w["w_in"], col0=inner + conv_ch, n=LANES)
    new_conv = proj.reshape(batch, seq, -1)[:, seq - (SSD_CONV - 1):, inner:inner + conv_ch]

    lpad = -(-seq // SSD_CHUNK) * SSD_CHUNK if seq > SSD_CHUNK else max(seq, SSD_CHUNK)
    if lpad != seq:
        assert seq < SSD_CHUNK
        ssd_in = _pad_rows(proj.reshape(batch, seq, -1), lpad).reshape(batch * lpad, -1)
        dt_in = _pad_rows(dt_raw.reshape(batch, seq, -1), lpad).reshape(batch * lpad, -1)
    else:
        ssd_in, dt_in = proj, dt_raw
    y, h_fin = ssd_mixer(ssd_in, dt_in, _pad_rows(conv_state, SUBLANES, front=True),
                         ssm_state.reshape(batch, inner, SSD_STATE), w,
                         batch=batch, seq=lpad, n_valid=min(seq, SSD_CHUNK))
    if lpad != seq:
        y = y.reshape(batch, lpad, inner)[:, :seq].reshape(batch * seq, inner)

    y_sc, sc_tail = sc_mixer(xn, w["w_sc"], _pad_rows(sc_state, SUBLANES, front=True), w["sc_w"],
                             batch=batch, seq=seq)
    x = matmul_residual([(y, w["w_out_ssd"]), (y_sc, w["w_out_sc"])], x)
    return (x, new_conv, h_fin.reshape(batch, inner // SSD_HEADDIM, SSD_HEADDIM, SSD_STATE),
            sc_tail[:, SUBLANES - (SC_WIDTH - 1):, :])


def _odd_prompt(x, w, slopes, *, batch, seq):
    att_q = ATT_HEADS * HEAD_DIM
    proj, k, v = norm_matmul(x, w["norm_mix"], w["w_in"], tap_col0=att_q)
    og = moba_prompt(proj, slopes, batch=batch, seq=seq)
    return (matmul_residual([(og, w["w_out"])], x),
            k.reshape(batch, seq, KV_HEADS, HEAD_DIM), v.reshape(batch, seq, KV_HEADS, HEAD_DIM))


def _odd_sample(x, w, k_pools, v_pools, page_table, slope_rows, *, layer, batch, seq, past_len):
    assert past_len % MOBA_BLOCK == 0 and seq <= MOBA_BLOCK and seq <= LANES
    n_blocks = past_len // MOBA_BLOCK
    assert MOBA_TOPK <= n_blocks <= LANES
    att_q, att_kv = ATT_HEADS * HEAD_DIM, KV_HEADS * HEAD_DIM
    proj, k_new, v_new = norm_matmul(x, w["norm_mix"], w["w_in"], tap_col0=att_q)
    q_rows = proj[:, :att_q].reshape(batch, seq, KV_HEADS, Q_PER_KV, HEAD_DIM).transpose(0, 2, 3, 1, 4)
    q_rows = q_rows.reshape(batch, ATT_HEADS * seq, HEAD_DIM)
    kmean = paged_block_means(k_pools, page_table, layer=layer, n_blocks=n_blocks)
    o = moba_sample(q_rows, _pad_rows(kmean.transpose(0, 2, 1, 3), LANES), slope_rows,
                    _pad_rows(k_new.reshape(batch, seq * KV_HEADS, HEAD_DIM), LANES),
                    _pad_rows(v_new.reshape(batch, seq * KV_HEADS, HEAD_DIM), LANES),
                    k_pools, v_pools, page_table, layer=layer, n_blocks=n_blocks, past_len=past_len, dec_seq=seq)
    o = o.reshape(batch, KV_HEADS, Q_PER_KV, seq, HEAD_DIM).transpose(0, 3, 1, 2, 4).reshape(batch * seq, att_q)
    og = silu_gate(o, proj, col0=att_q + 2 * att_kv)
    return (matmul_residual([(og, w["w_out"])], x),
            k_new.reshape(batch, seq, KV_HEADS, HEAD_DIM), v_new.reshape(batch, seq, KV_HEADS, HEAD_DIM))


def _cross_layer(x, w, mem_k, mem_v, *, batch, seq):
    return cross_layer(x, w["norm_cross"], w["ca_w_q"], mem_k, mem_v, w["ca_w_out"], batch=batch, seq=seq)


def kernel(x_prompt, x_sample, mem_prompt, cache_k, cache_v, page_table, state_conv, state_ssm, state_sc,
           cache_mem_k, cache_mem_v, norm_mix_w, norm_cross_w, norm_mem_w, final_norm_w,
           ev_w_in, ev_conv_w, ev_conv_b, ev_dt_bias, ev_a_log, ev_d_skip, ev_norm_w, ev_sc_w, ev_w_out,
           od_w_in, od_w_out, ca_w_q, ca_w_kv, ca_w_out):
    bp, sp, d = x_prompt.shape
    bs, ss, _ = x_sample.shape
    depth = norm_mix_w.shape[0]
    n_mem = mem_prompt.shape[1]
    n_heads = ev_dt_bias.shape[1]
    inner = n_heads * SSD_HEADDIM
    conv_ch = inner + 2 * SSD_GROUPS * SSD_STATE
    past_len = page_table.shape[1] * PAGE_SIZE
    ca_w = CA_HEADS * CA_HEAD_DIM
    assert inner == d and n_heads <= LANES

    slopes = jnp.asarray(_alibi_slopes())
    slope_rows = jnp.asarray(np.repeat(_alibi_slopes(), ss)[:, None] * np.ones((1, LANES), np.float32))
    expand_np = np.zeros((LANES, inner), np.float32)
    for hd in range(n_heads):
        expand_np[hd, hd * SSD_HEADDIM:(hd + 1) * SSD_HEADDIM] = 1.0
    expand = jnp.asarray(expand_np, dtype=BF16)

    xp = x_prompt.reshape(bp * sp, d)
    xs = x_sample.reshape(bs * ss, d)
    mem = mem_prompt.reshape(bp * n_mem, d)
    k_pools = cache_k.reshape(cache_k.shape[0], cache_k.shape[1], PAGE_ROWS, HEAD_DIM)
    v_pools = cache_v.reshape(cache_v.shape[0], cache_v.shape[1], PAGE_ROWS, HEAD_DIM)

    pk, pv, sk, sv = [], [], [], []
    pconv, pssm, psc, sconv, sssm, ssc = [], [], [], [], [], []
    pmk, pmv = [], []
    for l in range(depth):
        i = l // 2
        if l % 2 == 0:
            w_in = ev_w_in[i]
            dt0 = inner + conv_ch
            w = {
                "norm_mix": norm_mix_w[l],
                "w_in": w_in.astype(BF16),
                "w_sc": w_in[:, dt0 + n_heads:].astype(BF16),
                "conv_w": ev_conv_w[i],
                "conv_b": ev_conv_b[i].reshape(1, conv_ch),
                "dt_bias": _pad_lanes(ev_dt_bias[i].reshape(1, n_heads)),
                "a_log": _pad_lanes(ev_a_log[i].reshape(1, n_heads)),
                "d_skip": jnp.repeat(ev_d_skip[i], SSD_HEADDIM).reshape(1, inner),
                "norm_w": ev_norm_w[i].reshape(1, inner),
                "expand": expand,
                "sc_w": ev_sc_w[i],
                "w_out_ssd": ev_w_out[i][:inner].astype(BF16),
                "w_out_sc": ev_w_out[i][inner:].astype(BF16),
            }
            xp, c1, s1, q1 = _even_layer(xp, w, jnp.zeros((bp, SSD_CONV - 1, conv_ch), F32),
                                         jnp.zeros((bp, n_heads, SSD_HEADDIM, SSD_STATE), F32),
                                         jnp.zeros((bp, SC_WIDTH - 1, d), F32), batch=bp, seq=sp)
            xs, c2, s2, q2 = _even_layer(xs, w, state_conv[i], state_ssm[i], state_sc[i], batch=bs, seq=ss)
            pconv.append(c1); pssm.append(s1); psc.append(q1)
            sconv.append(c2); sssm.append(s2); ssc.append(q2)
        else:
            w = {"norm_mix": norm_mix_w[l], "w_in": od_w_in[i].astype(BF16), "w_out": od_w_out[i].astype(BF16)}
            xp, k1, v1 = _odd_prompt(xp, w, slopes, batch=bp, seq=sp)
            xs, k2, v2 = _odd_sample(xs, w, k_pools, v_pools, page_table, slope_rows,
                                     layer=i, batch=bs, seq=ss, past_len=past_len)
            pk.append(k1); pv.append(v1); sk.append(k2); sv.append(v2)
        wc = {"norm_cross": norm_cross_w[l], "ca_w_q": ca_w_q[l].astype(BF16), "ca_w_out": ca_w_out[l].astype(BF16)}
        mkv = norm_matmul(mem, norm_mem_w[l], ca_w_kv[l].astype(BF16))
        mk = mkv[:, :ca_w].reshape(bp, n_mem, ca_w)
        mv = mkv[:, ca_w:].reshape(bp, n_mem, ca_w)
        pmk.append(mk.reshape(bp, n_mem, CA_HEADS, CA_HEAD_DIM))
        pmv.append(mv.reshape(bp, n_mem, CA_HEADS, CA_HEAD_DIM))
        xp = _cross_layer(xp, wc, mk, mv, batch=bp, seq=sp)
        xs = _cross_layer(xs, wc, cache_mem_k[l].reshape(bs, n_mem, ca_w), cache_mem_v[l].reshape(bs, n_mem, ca_w),
                          batch=bs, seq=ss)
    y_prompt = rmsnorm(xp, final_norm_w).reshape(bp, sp, d)
    y_sample = rmsnorm(xs, final_norm_w).reshape(bs, ss, d)
    return (y_prompt, y_sample,
            jnp.stack(pk), jnp.stack(pv), jnp.stack(pconv), jnp.stack(pssm), jnp.stack(psc),
            jnp.stack(pmk), jnp.stack(pmv),
            jnp.stack(sk), jnp.stack(sv), jnp.stack(sconv), jnp.stack(sssm), jnp.stack(ssc))
```

```python
import functools
import math

import numpy as np
import jax
import jax.numpy as jnp
from jax import lax
from jax.experimental import pallas as pl
from jax.experimental.pallas import tpu as pltpu

F32 = jnp.float32
BF16 = jnp.bfloat16
HIGHEST = lax.Precision.HIGHEST

NORM_EPS = 1e-5
NEG_INF = -1e30

LANES = 128
SUBLANES = 8
VMEM_LIMIT = 48 * 1024 * 1024

SSD_HEADDIM = 64
SSD_GROUPS = 4
SSD_STATE = 128
SSD_CONV = 4
SSD_CHUNK = 128
SC_WIDTH = 3
ATT_HEADS = 16
KV_HEADS = 4
HEAD_DIM = 128
Q_PER_KV = ATT_HEADS // KV_HEADS
MOBA_BLOCK = 256
MOBA_TOPK = 3
PAGE_SIZE = 128
PAGES_PER_BLOCK = MOBA_BLOCK // PAGE_SIZE
PAGE_ROWS = PAGE_SIZE * KV_HEADS
BLOCKS_PER_STEP = 8
PAGES_PER_STEP = BLOCKS_PER_STEP * PAGES_PER_BLOCK
CA_HEADS = 4
CA_HEAD_DIM = 128


def _cparams(*sem):
    return pltpu.CompilerParams(dimension_semantics=sem, vmem_limit_bytes=VMEM_LIMIT)


def _tile(n, pref):
    if n <= pref:
        return n
    t = pref
    while n % t:
        t //= 2
    return t


def _silu(x):
    return x / (1.0 + jnp.exp(-x))


def _dot(a, b, **kw):
    return jnp.dot(a, b, preferred_element_type=F32, **kw)


def _bf16_pieces(x):
    hi = x.astype(BF16)
    rest = x - hi.astype(F32)
    mid = rest.astype(BF16)
    lo = (rest - mid.astype(F32)).astype(BF16)
    return hi, mid, lo


def _dot_exact_left(x, sel):
    return sum(_dot(p, sel) for p in _bf16_pieces(x))


def _dot_exact_right(sel, x):
    return sum(_dot(sel, p) for p in _bf16_pieces(x))


def _dot_nt(a, b, **kw):
    return lax.dot_general(a, b, (((1,), (1,)), ((), ())), preferred_element_type=F32, **kw)


def _norm_mm_kernel(x_ref, nw_ref, w_ref, o_ref, xn_ref):
    @pl.when(pl.program_id(1) == 0)
    def _():
        x = x_ref[...]
        ms = jnp.mean(x * x, axis=-1, keepdims=True)
        xn_ref[...] = ((x * lax.rsqrt(ms + NORM_EPS)) * nw_ref[...]).astype(BF16)

    o_ref[...] = _dot(xn_ref[...], w_ref[...]).astype(o_ref.dtype)


def _norm_mm_tap_kernel(x_ref, nw_ref, w_ref, o_ref, t0_ref, t1_ref, xn_ref, *, tap_tile):
    _norm_mm_kernel(x_ref, nw_ref, w_ref, o_ref, xn_ref)

    @pl.when(pl.program_id(1) == tap_tile)
    def _():
        half = t0_ref.shape[1]
        t0_ref[...] = o_ref[:, :half]
        t1_ref[...] = o_ref[:, half:]


def _layer_spec(layer, block, index, **kw):
    return pl.BlockSpec((None,) + block, lambda *g: (layer,) + index(*g), **kw)


def norm_matmul(x, norm_w, w, layer, *, tm=1024, tn=1024, out_dtype=F32, tap_col0=None):
    m, k = x.shape
    n = w.shape[2]
    tm, tn = _tile(m, tm), _tile(n, tn)
    in_specs = [pl.BlockSpec((tm, k), lambda i, j: (i, 0)),
                pl.BlockSpec((1, k), lambda i, j: (0, 0)),
                _layer_spec(layer, (k, tn), lambda i, j: (0, j))]
    out_spec = pl.BlockSpec((tm, tn), lambda i, j: (i, j))
    out_shape = jax.ShapeDtypeStruct((m, n), out_dtype)
    if tap_col0 is None:
        body, out_specs, out_shapes = _norm_mm_kernel, out_spec, out_shape
    else:
        assert tap_col0 % tn == 0 and tn % (2 * LANES) == 0
        body = functools.partial(_norm_mm_tap_kernel, tap_tile=tap_col0 // tn)
        tap_spec = pl.BlockSpec((tm, tn // 2), lambda i, j: (i, 0))
        tap_shape = jax.ShapeDtypeStruct((m, tn // 2), out_dtype)
        out_specs, out_shapes = (out_spec, tap_spec, tap_spec), (out_shape, tap_shape, tap_shape)
    return pl.pallas_call(
        body,
        out_shape=out_shapes,
        grid=(m // tm, n // tn),
        in_specs=in_specs,
        out_specs=out_specs,
        scratch_shapes=[pltpu.VMEM((tm, k), BF16)],
        compiler_params=_cparams("parallel", "arbitrary"),
        name="norm_matmul",
    )(x, norm_w.reshape(1, k), w)


def _mm_res_kernel(*refs, n_pairs):
    res_ref, o_ref = refs[2 * n_pairs], refs[2 * n_pairs + 1]
    acc = res_ref[...]
    for p in range(n_pairs):
        acc = acc + _dot(refs[2 * p][...], refs[2 * p + 1][...])
    o_ref[...] = acc


def matmul_residual(pairs, res, layer, *, tm=512):
    m, n = res.shape
    tm = _tile(m, tm)
    in_specs, args = [], []
    for a, w, rb in pairs:
        k = a.shape[1]
        assert w.shape[1] % k == 0 and w.shape[2] == n
        in_specs += [pl.BlockSpec((tm, k), lambda i: (i, 0)),
                     _layer_spec(layer, (k, n), lambda i, rb=rb: (rb, 0), pipeline_mode=pl.Buffered(1))]
        args += [a, w]
    in_specs.append(pl.BlockSpec((tm, n), lambda i: (i, 0)))
    return pl.pallas_call(
        functools.partial(_mm_res_kernel, n_pairs=len(pairs)),
        out_shape=jax.ShapeDtypeStruct((m, n), F32),
        grid=(m // tm,),
        in_specs=in_specs,
        out_specs=pl.BlockSpec((tm, n), lambda i: (i, 0)),
        compiler_params=_cparams("parallel"),
        name="matmul_residual",
    )(*args, res)


def _mm_kernel(a_ref, w_ref, o_ref):
    o_ref[...] = _dot(a_ref[...], w_ref[...]).astype(o_ref.dtype)


def matmul(a, w, layer, *, col0=0, n=None, tm=1024, tn=1024, out_dtype=F32):
    m, k = a.shape
    n = w.shape[2] - col0 if n is None else n
    tm, tn = _tile(m, tm), _tile(n, tn)
    assert col0 % tn == 0 and col0 + n <= w.shape[2]
    cb0 = col0 // tn
    return pl.pallas_call(
        _mm_kernel,
        out_shape=jax.ShapeDtypeStruct((m, n), out_dtype),
        grid=(m // tm, n // tn),
        in_specs=[pl.BlockSpec((tm, k), lambda i, j: (i, 0)),
                  _layer_spec(layer, (k, tn), lambda i, j: (0, cb0 + j))],
        out_specs=pl.BlockSpec((tm, tn), lambda i, j: (i, j)),
        compiler_params=_cparams("parallel", "parallel"),
        name="matmul",
    )(a, w)


def _rmsnorm_kernel(x_ref, nw_ref, o_ref):
    x = x_ref[...]
    ms = jnp.mean(x * x, axis=-1, keepdims=True)
    o_ref[...] = ((x * lax.rsqrt(ms + NORM_EPS)) * nw_ref[...]).astype(o_ref.dtype)


def rmsnorm(x, norm_w, *, tm=512, out_dtype=F32):
    m, k = x.shape
    tm = _tile(m, tm)
    return pl.pallas_call(
        _rmsnorm_kernel,
        out_shape=jax.ShapeDtypeStruct((m, k), out_dtype),
        grid=(m // tm,),
        in_specs=[pl.BlockSpec((tm, k), lambda i: (i, 0)),
                  pl.BlockSpec((1, k), lambda i: (0, 0))],
        out_specs=pl.BlockSpec((tm, k), lambda i: (i, 0)),
        compiler_params=_cparams("parallel"),
        name="rmsnorm",
    )(x, norm_w.reshape(1, k))


def _rows_padded(ref, q):
    v = ref[...]
    if v.shape[0] < q:
        v = jnp.concatenate([v, jnp.zeros((q - v.shape[0], v.shape[1]), v.dtype)], axis=0)
    return v


def _causal_conv_chunk(x_ref, buf_ref, cw_ref, cb_ref, q):
    x = _rows_padded(x_ref, q)
    buf_ref[SUBLANES:SUBLANES + q, :] = x
    y = cw_ref[SSD_CONV - 1:SSD_CONV, :] * x
    for k in range(1, SSD_CONV):
        y = y + cw_ref[SSD_CONV - 1 - k:SSD_CONV - k, :] * buf_ref[SUBLANES - k:SUBLANES - k + q, :]
    if cb_ref is not None:
        y = y + cb_ref[...]
    buf_ref[0:SUBLANES, :] = x[q - SUBLANES:q, :]
    return y


def _ssd_kernel(z_ref, xs_ref, b_ref, c_ref, dt_ref, cs_xs_ref, cs_b_ref, cs_c_ref, h0_ref,
                cw_xs_ref, cw_b_ref, cw_c_ref, cb_xs_ref, cb_b_ref, cb_c_ref,
                dtb_ref, alog_ref, dskip_ref, nw_ref, e_ref,
                y_ref, hfin_ref,
                ht_ref, buf_xs, buf_b, buf_c, *, q, n_valid, n_chunks):
    c = pl.program_id(1)
    n_heads = e_ref.shape[1] // SSD_HEADDIM
    gw = (n_heads // SSD_GROUPS) * SSD_HEADDIM

    @pl.when(c == 0)
    def _():
        ht_ref[...] = h0_ref[...].T
        buf_xs[0:SUBLANES, :] = cs_xs_ref[...]
        buf_b[0:SUBLANES, :] = cs_b_ref[...]
        buf_c[0:SUBLANES, :] = cs_c_ref[...]

    xs = _silu(_causal_conv_chunk(xs_ref, buf_xs, cw_xs_ref, cb_xs_ref, q))
    bm = _silu(_causal_conv_chunk(b_ref, buf_b, cw_b_ref, cb_b_ref, q))
    cm = _silu(_causal_conv_chunk(c_ref, buf_c, cw_c_ref, cb_c_ref, q))

    row = lax.broadcasted_iota(jnp.int32, (q, q), 0)
    col = lax.broadcasted_iota(jnp.int32, (q, q), 1)
    tril = row >= col

    dtv = _rows_padded(dt_ref, q) + dtb_ref[...]
    dt = jnp.maximum(dtv, 0.0) + jnp.log(1.0 + jnp.exp(-jnp.abs(dtv)))
    if n_valid < q:
        dt = jnp.where(lax.broadcasted_iota(jnp.int32, dt.shape, 0) < n_valid, dt, 0.0)
    a = dt * (-jnp.exp(alog_ref[...]))
    a_cs = _dot_exact_right(jnp.where(tril, 1.0, 0.0).astype(BF16), a)
    a_cs_t = a_cs.T
    expand = e_ref[...]
    dt_full = _dot_exact_left(dt, expand)
    acs_full = _dot_exact_left(a_cs, expand)
    tot_full = acs_full[q - 1:q, :]
    xr = xs * dt_full
    xr_dec = (xr * jnp.exp(tot_full - acs_full)).astype(BF16)
    exp_acs = jnp.exp(acs_full)
    exp_tot = jnp.exp(tot_full)

    lane = lax.broadcasted_iota(jnp.int32, (q, LANES), 1)
    lo_half = lane < SSD_HEADDIM
    heads_per_tile = LANES // SSD_HEADDIM

    y = dskip_ref[...] * xs
    y_parts = []
    for g in range(SSD_GROUPS):
        bg = bm[:, g * SSD_STATE:(g + 1) * SSD_STATE]
        cg = cm[:, g * SSD_STATE:(g + 1) * SSD_STATE].astype(BF16)
        bg_t = bg.T.astype(BF16)
        cb = _dot(cg, bg_t)
        ht_g = ht_ref[:, g * gw:(g + 1) * gw]
        y_g = _dot(cg, ht_g.astype(BF16)) * exp_acs[:, g * gw:(g + 1) * gw]
        tiles = []
        for t in range(gw // LANES):
            base = g * gw + t * LANES
            xr_t = xr[:, base:base + LANES]
            acc = None
            for e in range(heads_per_tile):
                h = base // SSD_HEADDIM + e
                seg = a_cs[:, h:h + 1] - a_cs_t[h:h + 1, :]
                lmat = jnp.where(tril, jnp.exp(seg), 0.0)
                in_head = lo_half if e == 0 else jnp.logical_not(lo_half)
                part = _dot((cb * lmat).astype(BF16), jnp.where(in_head, xr_t, 0.0).astype(BF16))
                acc = part if acc is None else acc + part
            tiles.append(acc)
        y_g = y_g + jnp.concatenate(tiles, axis=1)
        states = _dot(bg_t, xr_dec[:, g * gw:(g + 1) * gw])
        ht_ref[:, g * gw:(g + 1) * gw] = exp_tot[:, g * gw:(g + 1) * gw] * ht_g + states
        y_parts.append(y_g)
    y = y + jnp.concatenate(y_parts, axis=1)

    y = y * _silu(_rows_padded(z_ref, q))
    for g in range(SSD_GROUPS):
        yg = y[:, g * gw:(g + 1) * gw]
        ms = jnp.mean(yg * yg, axis=-1, keepdims=True)
        y_ref[:, g * gw:(g + 1) * gw] = ((yg * lax.rsqrt(ms + NORM_EPS)) * nw_ref[:, g * gw:(g + 1) * gw]).astype(y_ref.dtype)

    @pl.when(c == n_chunks - 1)
    def _():
        hfin_ref[...] = ht_ref[...].T


def ssd_mixer(proj, dt_raw, conv_state8, h0, wts, *, batch, seq):
    q = SSD_CHUNK
    n_valid = min(seq, q)
    assert seq % n_valid == 0 and n_valid % SUBLANES == 0
    nc = seq // n_valid
    assert n_valid == q or nc == 1
    inner = h0.shape[1]
    gn = SSD_GROUPS * SSD_STATE
    assert inner % gn == 0 and (inner // gn) * gn == inner
    kb = inner // gn
    rows = lambda b, c: b * nc + c
    full = lambda b, c: (0, 0)
    in_specs = [
        pl.BlockSpec((n_valid, inner), lambda b, c: (rows(b, c), 0)),
        pl.BlockSpec((n_valid, inner), lambda b, c: (rows(b, c), 1)),
        pl.BlockSpec((n_valid, gn), lambda b, c: (rows(b, c), 2 * kb)),
        pl.BlockSpec((n_valid, gn), lambda b, c: (rows(b, c), 2 * kb + 1)),
        pl.BlockSpec((n_valid, LANES), lambda b, c: (rows(b, c), 0)),
        pl.BlockSpec((None, SUBLANES, inner), lambda b, c: (b, 0, 0)),
        pl.BlockSpec((None, SUBLANES, gn), lambda b, c: (b, 0, kb)),
        pl.BlockSpec((None, SUBLANES, gn), lambda b, c: (b, 0, kb + 1)),
        pl.BlockSpec((None, inner, SSD_STATE), lambda b, c: (b, 0, 0)),
        pl.BlockSpec((SSD_CONV, inner), full),
        pl.BlockSpec((SSD_CONV, gn), lambda b, c: (0, kb)),
        pl.BlockSpec((SSD_CONV, gn), lambda b, c: (0, kb + 1)),
        pl.BlockSpec((1, inner), full),
        pl.BlockSpec((1, gn), lambda b, c: (0, kb)),
        pl.BlockSpec((1, gn), lambda b, c: (0, kb + 1)),
        pl.BlockSpec((1, LANES), full),
        pl.BlockSpec((1, LANES), full),
        pl.BlockSpec((1, inner), full),
        pl.BlockSpec((1, inner), full),
        pl.BlockSpec((LANES, inner), full),
    ]
    cw, cb = wts["conv_w"], wts["conv_b"]
    y, hfin = pl.pallas_call(
        functools.partial(_ssd_kernel, q=q, n_valid=n_valid, n_chunks=nc),
        out_shape=(jax.ShapeDtypeStruct((batch * nc * q, inner), BF16),
                   jax.ShapeDtypeStruct((batch, inner, SSD_STATE), F32)),
        grid=(batch, nc),
        in_specs=in_specs,
        out_specs=(pl.BlockSpec((q, inner), lambda b, c: (rows(b, c), 0)),
                   pl.BlockSpec((None, inner, SSD_STATE), lambda b, c: (b, 0, 0))),
        scratch_shapes=[pltpu.VMEM((SSD_STATE, inner), F32),
                        pltpu.VMEM((q + SUBLANES, inner), F32),
                        pltpu.VMEM((q + SUBLANES, gn), F32),
                        pltpu.VMEM((q + SUBLANES, gn), F32)],
        compiler_params=_cparams("parallel", "arbitrary"),
        name="ssd_mixer",
    )(proj, proj, proj, proj, dt_raw, conv_state8, conv_state8, conv_state8, h0,
      cw, cw, cw, cb, cb, cb, wts["dt_bias"], wts["a_log"], wts["d_skip"], wts["norm_w"], wts["expand"])
    return y, hfin


def _sc_mixer_kernel(xn_ref, wb_ref, wc_ref, wx_ref, wg_ref, st_ref, w_ref, y_ref, last_ref, buf_ref, *, tq):
    @pl.when(pl.program_id(2) == 0)
    def _():
        buf_ref[0:SUBLANES, :] = st_ref[...]

    xn = xn_ref[...]
    prod = _dot(xn, wc_ref[...]) * _dot(xn, wx_ref[...])
    buf_ref[SUBLANES:SUBLANES + tq, :] = prod
    u = w_ref[SC_WIDTH - 1:SC_WIDTH, :] * prod
    for k in range(1, SC_WIDTH):
        u = u + w_ref[SC_WIDTH - 1 - k:SC_WIDTH - k, :] * buf_ref[SUBLANES - k:SUBLANES - k + tq, :]
    tail = prod[tq - SUBLANES:tq, :]
    buf_ref[0:SUBLANES, :] = tail
    last_ref[...] = tail
    y_ref[...] = (_dot(xn, wb_ref[...]) * u * _silu(_dot(xn, wg_ref[...]))).astype(y_ref.dtype)


def sc_mixer(xn, w_sc, layer, state8, w, *, batch, seq, tq=1024, tc=512):
    k = xn.shape[1]
    dim = w.shape[1]
    tq = _tile(seq, tq)
    nt = seq // tq
    ncb = dim // tc
    assert dim % tc == 0 and tq % SUBLANES == 0 and w_sc.shape[2] == 4 * dim

    def part(p):
        return _layer_spec(layer, (k, tc), lambda j, b, t: (0, p * ncb + j))

    return pl.pallas_call(
        functools.partial(_sc_mixer_kernel, tq=tq),
        out_shape=(jax.ShapeDtypeStruct((batch * seq, dim), BF16),
                   jax.ShapeDtypeStruct((batch, SUBLANES, dim), F32)),
        grid=(ncb, batch, nt),
        in_specs=[pl.BlockSpec((tq, k), lambda j, b, t: (b * nt + t, 0)),
                  part(0), part(1), part(2), part(3),
                  pl.BlockSpec((None, SUBLANES, tc), lambda j, b, t: (b, 0, j)),
                  pl.BlockSpec((SC_WIDTH, tc), lambda j, b, t: (0, j))],
        out_specs=(pl.BlockSpec((tq, tc), lambda j, b, t: (b * nt + t, j)),
                   pl.BlockSpec((None, SUBLANES, tc), lambda j, b, t: (b, 0, j))),
        scratch_shapes=[pltpu.VMEM((tq + SUBLANES, tc), F32)],
        compiler_params=_cparams("parallel", "parallel", "arbitrary"),
        name="sc_mixer",
    )(xn, w_sc, w_sc, w_sc, w_sc, state8, w)


def _top_blocks(gate, n_valid, axis=1):
    idx = lax.broadcasted_iota(jnp.int32, gate.shape, axis).astype(F32)
    g = jnp.where(idx < jnp.asarray(n_valid, F32), gate, NEG_INF)
    sel = jnp.zeros(gate.shape, F32)
    for _ in range(MOBA_TOPK):
        m = jnp.max(g, axis=axis, keepdims=True)
        first = jnp.min(jnp.where(g == m, idx, float(gate.shape[axis])), axis=axis, keepdims=True)
        pick = idx == first
        sel = jnp.where(pick & (m > 0.5 * NEG_INF), 1.0, sel)
        g = jnp.where(pick, -jnp.inf, g)
    return sel


def _moba_prompt_kernel(slopes_ref, q_ref, k_ref, v_ref, g_ref, o_ref, kmean_ref, kb_ref, vt_ref, mval_ref, *, n_blocks):
    kv = pl.program_id(1)
    i = pl.program_id(2)
    blk = MOBA_BLOCK
    heads = range(Q_PER_KV)

    @pl.when(i == 0)
    def _():
        kmean_ref[...] = jnp.zeros(kmean_ref.shape, F32)
        for j in range(n_blocks):
            kj = k_ref[j * blk:(j + 1) * blk, :]
            kmean_ref[j:j + 1, :] = jnp.sum(kj, axis=0, keepdims=True) * (1.0 / blk)
            kb_ref[j] = kj.astype(BF16)
            vt_ref[j] = v_ref[j * blk:(j + 1) * blk, :].T.astype(BF16)

    rel = (lax.broadcasted_iota(jnp.int32, (blk, blk), 1)
           - lax.broadcasted_iota(jnp.int32, (blk, blk), 0)).astype(F32)
    blk_idx = lax.broadcasted_iota(jnp.int32, (kmean_ref.shape[0], blk), 0)
    back = (i - blk_idx).astype(F32) * float(blk)
    kmean = kmean_ref[...]

    qt, bias, diag = [], [], []
    for g in heads:
        slope = slopes_ref[kv * Q_PER_KV + g]
        q_t = (q_ref[:, g * HEAD_DIM:(g + 1) * HEAD_DIM] * (HEAD_DIM ** -0.5)).T
        gate = _dot(kmean, q_t, precision=HIGHEST)
        sel = _top_blocks(gate, i, axis=0)
        mval_ref[:, g * blk:(g + 1) * blk] = jnp.where(sel > 0.0, -slope * back, NEG_INF)
        bias.append(-slope * rel)
        diag.append(jnp.where(rel >= 0, -slope * rel, NEG_INF))
        qt.append(q_t.astype(BF16))
    qt = jnp.concatenate(qt, axis=1)
    bias = jnp.concatenate(bias, axis=1)

    s = _dot(kb_ref[i], qt) + jnp.concatenate(diag, axis=1)
    m0 = jnp.max(s, axis=0, keepdims=True)
    p = jnp.exp(s - m0)
    l0 = jnp.sum(p, axis=0, keepdims=True)
    acc0 = _dot(vt_ref[i], p.astype(BF16))

    def past_pair(t, carry):
        m, l, acc = carry
        js = (2 * t, 2 * t + 1)
        ss = [_dot(kb_ref[j], qt) + bias for j in js]
        masks = [mval_ref[pl.ds(j, 1), :] for j in js]
        m_new = m
        for s, mask in zip(ss, masks):
            m_new = jnp.maximum(m_new, jnp.max(s, axis=0, keepdims=True) + mask)
        l = jnp.exp(m - m_new) * l
        acc = jnp.exp(m - m_new) * acc
        for j, s, mask in zip(js, ss, masks):
            p = jnp.exp(s - (m_new - mask))
            l = l + jnp.sum(p, axis=0, keepdims=True)
            acc = acc + _dot(vt_ref[j], p.astype(BF16))
        return m_new, l, acc

    _, l, acc = lax.fori_loop(0, (i + 1) // 2, past_pair, (m0, l0, acc0))
    o_t = acc / l
    for g in heads:
        c = slice(g * HEAD_DIM, (g + 1) * HEAD_DIM)
        o_ref[:, c] = (o_t[:, g * blk:(g + 1) * blk].T * _silu(g_ref[:, c])).astype(o_ref.dtype)


def moba_prompt(proj, slopes, *, batch, seq):
    assert seq % MOBA_BLOCK == 0
    nb = seq // MOBA_BLOCK
    nbp = -(-nb // SUBLANES) * SUBLANES
    gw = Q_PER_KV * HEAD_DIM
    kcol = ATT_HEADS
    vcol = ATT_HEADS + KV_HEADS
    gcol = (ATT_HEADS + 2 * KV_HEADS) * HEAD_DIM // gw
    assert gcol * gw == (ATT_HEADS + 2 * KV_HEADS) * HEAD_DIM
    return pl.pallas_call(
        functools.partial(_moba_prompt_kernel, n_blocks=nb),
        out_shape=jax.ShapeDtypeStruct((batch * seq, ATT_HEADS * HEAD_DIM), BF16),
        grid=(batch, KV_HEADS, nb),
        in_specs=[pl.BlockSpec(memory_space=pltpu.SMEM),
                  pl.BlockSpec((MOBA_BLOCK, gw), lambda b, kv, i: (b * nb + i, kv)),
                  pl.BlockSpec((seq, HEAD_DIM), lambda b, kv, i: (b, kcol + kv)),
                  pl.BlockSpec((seq, HEAD_DIM), lambda b, kv, i: (b, vcol + kv)),
                  pl.BlockSpec((MOBA_BLOCK, gw), lambda b, kv, i: (b * nb + i, gcol + kv))],
        out_specs=pl.BlockSpec((MOBA_BLOCK, gw), lambda b, kv, i: (b * nb + i, kv)),
        scratch_shapes=[pltpu.VMEM((nbp, HEAD_DIM), F32),
                        pltpu.VMEM((nb, MOBA_BLOCK, HEAD_DIM), BF16),
                        pltpu.VMEM((nb, HEAD_DIM, MOBA_BLOCK), BF16),
                        pltpu.VMEM((nbp, Q_PER_KV * MOBA_BLOCK), F32)],
        compiler_params=_cparams("parallel", "parallel", "arbitrary"),
        name="moba_prompt",
    )(slopes, proj, proj, proj, proj)


def _page_specs(layer):
    return [pl.BlockSpec((None, None, PAGE_ROWS, HEAD_DIM),
                         lambda b, s, pt, p=p: (layer, pt[b, s * PAGES_PER_STEP + p], 0, 0))
            for p in range(PAGES_PER_STEP)]


def _kmean_kernel(pt_ref, *refs):
    del pt_ref
    pages, o_ref = refs[:PAGES_PER_STEP], refs[PAGES_PER_STEP]
    for bb in range(BLOCKS_PER_STEP):
        tot = None
        for pg in range(PAGES_PER_BLOCK):
            page = pages[bb * PAGES_PER_BLOCK + pg][...]
            part = jnp.sum(page.reshape(PAGE_ROWS // SUBLANES, SUBLANES, HEAD_DIM), axis=0)
            tot = part if tot is None else tot + part
        per_kv = tot[0:KV_HEADS]
        for par in range(1, SUBLANES // KV_HEADS):
            per_kv = per_kv + tot[par * KV_HEADS:(par + 1) * KV_HEADS]
        o_ref[bb] = per_kv * (1.0 / MOBA_BLOCK)


def paged_block_means(k_pools, page_table, *, layer, n_blocks):
    db = page_table.shape[0]
    assert n_blocks % BLOCKS_PER_STEP == 0 and SUBLANES % KV_HEADS == 0
    return pl.pallas_call(
        _kmean_kernel,
        out_shape=jax.ShapeDtypeStruct((db, n_blocks, KV_HEADS, HEAD_DIM), F32),
        grid_spec=pltpu.PrefetchScalarGridSpec(
            num_scalar_prefetch=1,
            grid=(db, n_blocks // BLOCKS_PER_STEP),
            in_specs=_page_specs(layer),
            out_specs=pl.BlockSpec((None, BLOCKS_PER_STEP, KV_HEADS, HEAD_DIM), lambda b, s, pt: (b, s, 0, 0))),
        compiler_params=_cparams("parallel", "arbitrary"),
        name="paged_block_means",
    )(page_table, *([k_pools] * PAGES_PER_STEP))


def _moba_sample_kernel(pt_ref, q_ref, kmean_ref, slope_ref, knew_ref, vnew_ref, *refs,
                        n_blocks, past_len, dec_seq):
    del pt_ref
    k_pages, v_pages = refs[:PAGES_PER_STEP], refs[PAGES_PER_STEP:2 * PAGES_PER_STEP]
    o_ref, m_ref, l_ref, acc_ref, sel_ref, base_ref = refs[2 * PAGES_PER_STEP:]
    step = pl.program_id(1)
    rows = q_ref.shape[0]
    rpk = rows // KV_HEADS
    kv_shift = KV_HEADS.bit_length() - 1
    q = q_ref[...] * (HEAD_DIM ** -0.5)
    qb = q.astype(BF16)
    slope = slope_ref[:, 0:1]
    lane = lax.broadcasted_iota(jnp.int32, (rows, LANES), 1)

    def key_bias(width, q_pos0, causal):
        r = lax.broadcasted_iota(jnp.int32, (rows, width), 0)
        c = lax.broadcasted_iota(jnp.int32, (rows, width), 1)
        tok = jnp.right_shift(c, kv_shift)
        t_row = jnp.bitwise_and(r, dec_seq - 1)
        same_head = jnp.bitwise_and(c, KV_HEADS - 1) == jnp.right_shift(r, rpk.bit_length() - 1)
        dist = (q_pos0 + t_row - tok).astype(F32)
        ok = same_head & (dist >= 0) if causal else same_head
        return jnp.where(ok, -slope * dist, NEG_INF)

    @pl.when(step == 0)
    def _():
        gate = jnp.concatenate([_dot_nt(q[kv * rpk:(kv + 1) * rpk, :], kmean_ref[kv], precision=HIGHEST)
                                for kv in range(KV_HEADS)], axis=0)
        sel_ref[...] = _top_blocks(gate, n_blocks)
        base_ref[...] = key_bias(PAGE_ROWS, past_len, causal=False)
        s = _dot_nt(qb, knew_ref[...].astype(BF16)) + key_bias(LANES, 0, causal=True)
        m = jnp.max(s, axis=1, keepdims=True)
        p = jnp.exp(s - m)
        m_ref[...] = jnp.broadcast_to(m, m_ref.shape)
        l_ref[...] = jnp.broadcast_to(jnp.sum(p, axis=1, keepdims=True), l_ref.shape)
        acc_ref[...] = _dot(p.astype(BF16), vnew_ref[...].astype(BF16))

    sel = sel_ref[...]
    base = base_ref[...]
    m, l, acc = m_ref[:, 0:1], l_ref[:, 0:1], acc_ref[...]
    scores, offsets = [], []
    m_new = m
    for bb in range(BLOCKS_PER_STEP):
        blk_id = step * BLOCKS_PER_STEP + bb
        chosen = jnp.sum(jnp.where(lane == blk_id, sel, 0.0), axis=1, keepdims=True) > 0.0
        for pg in range(PAGES_PER_BLOCK):
            page_pos = (blk_id * MOBA_BLOCK + pg * PAGE_SIZE).astype(F32)
            mcol = jnp.where(chosen, slope * page_pos, NEG_INF)
            s = _dot_nt(qb, k_pages[bb * PAGES_PER_BLOCK + pg][...].astype(BF16)) + base
            m_new = jnp.maximum(m_new, jnp.max(s, axis=1, keepdims=True) + mcol)
            scores.append(s)
            offsets.append(mcol)
    alpha = jnp.exp(m - m_new)
    l, acc, m = alpha * l, alpha * acc, m_new
    for pg, (s, mcol) in enumerate(zip(scores, offsets)):
        p = jnp.exp(s - (m - mcol))
        l = l + jnp.sum(p, axis=1, keepdims=True)
        acc = acc + _dot(p.astype(BF16), v_pages[pg][...].astype(BF16))
    m_ref[...] = jnp.broadcast_to(m, m_ref.shape)
    l_ref[...] = jnp.broadcast_to(l, l_ref.shape)
    acc_ref[...] = acc

    @pl.when(step == n_blocks // BLOCKS_PER_STEP - 1)
    def _():
        o_ref[...] = acc / l


def moba_sample(q_rows, kmean_t, slope_rows, k_new_pad, v_new_pad, k_pools, v_pools, page_table,
                *, layer, n_blocks, past_len, dec_seq):
    db, rows, _ = q_rows.shape
    rpk = rows // KV_HEADS
    for n in (KV_HEADS, dec_seq, rpk):
        assert n & (n - 1) == 0
    assert n_blocks % BLOCKS_PER_STEP == 0 and dec_seq * KV_HEADS <= LANES
    per_b = lambda *shape: pl.BlockSpec((None,) + shape, lambda b, s, pt: (b,) + (0,) * len(shape))
    return pl.pallas_call(
        functools.partial(_moba_sample_kernel, n_blocks=n_blocks, past_len=past_len, dec_seq=dec_seq),
        out_shape=jax.ShapeDtypeStruct((db, rows, HEAD_DIM), F32),
        grid_spec=pltpu.PrefetchScalarGridSpec(
            num_scalar_prefetch=1,
            grid=(db, n_blocks // BLOCKS_PER_STEP),
            in_specs=[per_b(rows, HEAD_DIM),
                      per_b(KV_HEADS, LANES, HEAD_DIM),
                      pl.BlockSpec((rows, LANES), lambda b, s, pt: (0, 0)),
                      per_b(LANES, HEAD_DIM),
                      per_b(LANES, HEAD_DIM)] + _page_specs(layer) + _page_specs(layer),
            out_specs=per_b(rows, HEAD_DIM),
            scratch_shapes=[pltpu.VMEM((rows, LANES), F32), pltpu.VMEM((rows, LANES), F32),
                            pltpu.VMEM((rows, HEAD_DIM), F32), pltpu.VMEM((rows, LANES), F32),
                            pltpu.VMEM((rows, PAGE_ROWS), F32)]),
        compiler_params=_cparams("parallel", "arbitrary"),
        name="moba_sample",
    )(page_table, q_rows, kmean_t, slope_rows, k_new_pad, v_new_pad,
      *([k_pools] * PAGES_PER_STEP), *([v_pools] * PAGES_PER_STEP))


def _gate_kernel(o_ref, g_ref, y_ref):
    y_ref[...] = (o_ref[...] * _silu(g_ref[...])).astype(y_ref.dtype)


def silu_gate(o, proj, *, col0, tn=1024):
    m, n = o.shape
    assert col0 % tn == 0 and n % tn == 0
    cb = col0 // tn
    return pl.pallas_call(
        _gate_kernel,
        out_shape=jax.ShapeDtypeStruct((m, n), BF16),
        grid=(n // tn,),
        in_specs=[pl.BlockSpec((m, tn), lambda j: (0, j)),
                  pl.BlockSpec((m, tn), lambda j: (0, cb + j))],
        out_specs=pl.BlockSpec((m, tn), lambda j: (0, j)),
        compiler_params=_cparams("parallel"),
        name="silu_gate",
    )(o, proj)


def _cross_layer_kernel(x_ref, nw_ref, wq_ref, k_ref, v_ref, wo_ref, o_ref):
    width = CA_HEADS * CA_HEAD_DIM
    x = x_ref[...]
    ms = jnp.mean(x * x, axis=-1, keepdims=True)
    xn = ((x * lax.rsqrt(ms + NORM_EPS)) * nw_ref[...]).astype(BF16)
    qg = _dot(xn, wq_ref[...])
    gated = []
    for h in range(CA_HEADS):
        c = slice(h * CA_HEAD_DIM, (h + 1) * CA_HEAD_DIM)
        q = (qg[:, c] * (CA_HEAD_DIM ** -0.5)).astype(BF16)
        s = _dot_nt(q, k_ref[:, c].astype(BF16))
        m = jnp.max(s, axis=1, keepdims=True)
        p = jnp.exp(s - m)
        l = jnp.sum(p, axis=1, keepdims=True)
        o = _dot(p.astype(BF16), v_ref[:, c].astype(BF16)) / l
        g = qg[:, width + h * CA_HEAD_DIM:width + (h + 1) * CA_HEAD_DIM]
        gated.append((o * _silu(g)).astype(BF16))
    o_ref[...] = x + _dot(jnp.concatenate(gated, axis=1), wo_ref[...])


def cross_layer(x, norm_w, w_q, mem_k, mem_v, w_out, layer, *, batch, seq, tm=512):
    width = CA_HEADS * CA_HEAD_DIM
    d = x.shape[1]
    n_mem = mem_k.shape[1]
    tm = _tile(seq, tm)
    nt = seq // tm
    const = lambda i: (0, 0)
    return pl.pallas_call(
        _cross_layer_kernel,
        out_shape=jax.ShapeDtypeStruct((batch * seq, d), F32),
        grid=(batch * nt,),
        in_specs=[pl.BlockSpec((tm, d), lambda i: (i, 0)),
                  pl.BlockSpec((1, d), const),
                  _layer_spec(layer, (d, 2 * width), const),
                  pl.BlockSpec((None, n_mem, width), lambda i: (i // nt, 0, 0)),
                  pl.BlockSpec((None, n_mem, width), lambda i: (i // nt, 0, 0)),
                  _layer_spec(layer, (width, d), const)],
        out_specs=pl.BlockSpec((tm, d), lambda i: (i, 0)),
        compiler_params=_cparams("parallel"),
        name="cross_layer",
    )(x, norm_w.reshape(1, d), w_q, mem_k, mem_v, w_out)


def _alibi_slopes():
    return np.array([2.0 ** (-8.0 * (h + 1) / ATT_HEADS) for h in range(ATT_HEADS)], dtype=np.float32)


def _pad_rows(a, n, front=False):
    extra = n - a.shape[-2]
    pad = [(0, 0)] * a.ndim
    pad[-2] = (extra, 0) if front else (0, extra)
    return jnp.pad(a, pad)


def _pad_lanes(a):
    pad = [(0, 0)] * a.ndim
    pad[-1] = (0, LANES - a.shape[-1])
    return jnp.pad(a, pad)


def _even_layer(x, w, layer, conv_state, ssm_state, sc_state, *, batch, seq):
    d = x.shape[1]
    inner = d
    conv_ch = inner + 2 * SSD_GROUPS * SSD_STATE
    xn = rmsnorm(x, w["norm_mix"], out_dtype=BF16)
    proj = matmul(xn, w["w_in"], layer, n=inner + conv_ch)
    dt_raw = matmul(xn, w["w_in"], layer, col0=inner + conv_ch, n=LANES)
    new_conv = proj.reshape(batch, seq, -1)[:, seq - (SSD_CONV - 1):, inner:inner + conv_ch]

    y, h_fin = ssd_mixer(proj, dt_raw, _pad_rows(conv_state, SUBLANES, front=True),
                         ssm_state.reshape(batch, inner, SSD_STATE), w, batch=batch, seq=seq)
    if y.shape[0] != batch * seq:
        y = y.reshape(batch, -1, inner)[:, :seq].reshape(batch * seq, inner)

    y_sc, sc_tail = sc_mixer(xn, w["w_sc"], layer, _pad_rows(sc_state, SUBLANES, front=True), w["sc_w"],
                             batch=batch, seq=seq)
    x = matmul_residual([(y, w["w_out"], 0), (y_sc, w["w_out"], 1)], x, layer)
    return (x, new_conv, h_fin.reshape(batch, inner // SSD_HEADDIM, SSD_HEADDIM, SSD_STATE),
            sc_tail[:, SUBLANES - (SC_WIDTH - 1):, :])


def _odd_prompt(x, w, slopes, *, layer, batch, seq):
    att_q = ATT_HEADS * HEAD_DIM
    proj, k, v = norm_matmul(x, w["norm_mix"], w["w_in"], layer, tap_col0=att_q)
    og = moba_prompt(proj, slopes, batch=batch, seq=seq)
    return (matmul_residual([(og, w["w_out"], 0)], x, layer),
            k.reshape(batch, seq, KV_HEADS, HEAD_DIM), v.reshape(batch, seq, KV_HEADS, HEAD_DIM))


def _odd_sample(x, w, k_pools, v_pools, page_table, slope_rows, *, layer, batch, seq, past_len):
    assert past_len % MOBA_BLOCK == 0 and seq <= MOBA_BLOCK and seq <= LANES
    n_blocks = past_len // MOBA_BLOCK
    assert MOBA_TOPK <= n_blocks <= LANES
    att_q, att_kv = ATT_HEADS * HEAD_DIM, KV_HEADS * HEAD_DIM
    proj, k_new, v_new = norm_matmul(x, w["norm_mix"], w["w_in"], layer, tap_col0=att_q)
    q_rows = proj[:, :att_q].reshape(batch, seq, KV_HEADS, Q_PER_KV, HEAD_DIM).transpose(0, 2, 3, 1, 4)
    q_rows = q_rows.reshape(batch, ATT_HEADS * seq, HEAD_DIM)
    kmean = paged_block_means(k_pools, page_table, layer=layer, n_blocks=n_blocks)
    o = moba_sample(q_rows, _pad_rows(kmean.transpose(0, 2, 1, 3), LANES), slope_rows,
                    _pad_rows(k_new.reshape(batch, seq * KV_HEADS, HEAD_DIM), LANES),
                    _pad_rows(v_new.reshape(batch, seq * KV_HEADS, HEAD_DIM), LANES),
                    k_pools, v_pools, page_table, layer=layer, n_blocks=n_blocks, past_len=past_len, dec_seq=seq)
    o = o.reshape(batch, KV_HEADS, Q_PER_KV, seq, HEAD_DIM).transpose(0, 3, 1, 2, 4).reshape(batch * seq, att_q)
    og = silu_gate(o, proj, col0=att_q + 2 * att_kv)
    return (matmul_residual([(og, w["w_out"], 0)], x, layer),
            k_new.reshape(batch, seq, KV_HEADS, HEAD_DIM), v_new.reshape(batch, seq, KV_HEADS, HEAD_DIM))


def _cross_layer(x, w, layer, mem_k, mem_v, *, batch, seq):
    return cross_layer(x, w["norm_cross"], w["ca_w_q"], mem_k, mem_v, w["ca_w_out"], layer, batch=batch, seq=seq)


def kernel(x_prompt, x_sample, mem_prompt, cache_k, cache_v, page_table, state_conv, state_ssm, state_sc,
           cache_mem_k, cache_mem_v, norm_mix_w, norm_cross_w, norm_mem_w, final_norm_w,
           ev_w_in, ev_conv_w, ev_conv_b, ev_dt_bias, ev_a_log, ev_d_skip, ev_norm_w, ev_sc_w, ev_w_out,
           od_w_in, od_w_out, ca_w_q, ca_w_kv, ca_w_out):
    bp, sp, d = x_prompt.shape
    bs, ss, _ = x_sample.shape
    depth = norm_mix_w.shape[0]
    n_mem = mem_prompt.shape[1]
    n_heads = ev_dt_bias.shape[1]
    inner = n_heads * SSD_HEADDIM
    conv_ch = inner + 2 * SSD_GROUPS * SSD_STATE
    past_len = page_table.shape[1] * PAGE_SIZE
    ca_w = CA_HEADS * CA_HEAD_DIM
    assert inner == d and n_heads <= LANES

    slopes = jnp.asarray(_alibi_slopes())
    slope_rows = jnp.asarray(np.repeat(_alibi_slopes(), ss)[:, None] * np.ones((1, LANES), np.float32))
    expand_np = np.zeros((LANES, inner), np.float32)
    for hd in range(n_heads):
        expand_np[hd, hd * SSD_HEADDIM:(hd + 1) * SSD_HEADDIM] = 1.0
    expand = jnp.asarray(expand_np, dtype=BF16)

    xp = x_prompt.reshape(bp * sp, d)
    xs = x_sample.reshape(bs * ss, d)
    mem = mem_prompt.reshape(bp * n_mem, d)
    k_pools = cache_k.reshape(cache_k.shape[0], cache_k.shape[1], PAGE_ROWS, HEAD_DIM)
    v_pools = cache_v.reshape(cache_v.shape[0], cache_v.shape[1], PAGE_ROWS, HEAD_DIM)

    ev_w_in_b = ev_w_in.astype(BF16)
    ev_w_sc_b = ev_w_in_b[:, :, inner + conv_ch + n_heads:]
    ev_w_out_b = ev_w_out.astype(BF16)
    od_w_in_b, od_w_out_b = od_w_in.astype(BF16), od_w_out.astype(BF16)
    ca_w_q_b, ca_w_kv_b, ca_w_out_b = ca_w_q.astype(BF16), ca_w_kv.astype(BF16), ca_w_out.astype(BF16)

    pk, pv, sk, sv = [], [], [], []
    pconv, pssm, psc, sconv, sssm, ssc = [], [], [], [], [], []
    pmk, pmv = [], []
    for l in range(depth):
        i = l // 2
        if l % 2 == 0:
            w = {
                "norm_mix": norm_mix_w[l],
                "w_in": ev_w_in_b,
                "w_sc": ev_w_sc_b,
                "w_out": ev_w_out_b,
                "conv_w": ev_conv_w[i],
                "conv_b": ev_conv_b[i].reshape(1, conv_ch),
                "dt_bias": _pad_lanes(ev_dt_bias[i].reshape(1, n_heads)),
                "a_log": _pad_lanes(ev_a_log[i].reshape(1, n_heads)),
                "d_skip": jnp.repeat(ev_d_skip[i], SSD_HEADDIM).reshape(1, inner),
                "norm_w": ev_norm_w[i].reshape(1, inner),
                "expand": expand,
                "sc_w": ev_sc_w[i],
            }
            xp, c1, s1, q1 = _even_layer(xp, w, i, jnp.zeros((bp, SSD_CONV - 1, conv_ch), F32),
                                         jnp.zeros((bp, n_heads, SSD_HEADDIM, SSD_STATE), F32),
                                         jnp.zeros((bp, SC_WIDTH - 1, d), F32), batch=bp, seq=sp)
            xs, c2, s2, q2 = _even_layer(xs, w, i, state_conv[i], state_ssm[i], state_sc[i], batch=bs, seq=ss)
            pconv.append(c1); pssm.append(s1); psc.append(q1)
            sconv.append(c2); sssm.append(s2); ssc.append(q2)
        else:
            w = {"norm_mix": norm_mix_w[l], "w_in": od_w_in_b, "w_out": od_w_out_b}
            xp, k1, v1 = _odd_prompt(xp, w, slopes, layer=i, batch=bp, seq=sp)
            xs, k2, v2 = _odd_sample(xs, w, k_pools, v_pools, page_table, slope_rows,
                                     layer=i, batch=bs, seq=ss, past_len=past_len)
            pk.append(k1); pv.append(v1); sk.append(k2); sv.append(v2)
        wc = {"norm_cross": norm_cross_w[l], "ca_w_q": ca_w_q_b, "ca_w_out": ca_w_out_b}
        mkv = norm_matmul(mem, norm_mem_w[l], ca_w_kv_b, l)
        mk = mkv[:, :ca_w].reshape(bp, n_mem, ca_w)
        mv = mkv[:, ca_w:].reshape(bp, n_mem, ca_w)
        pmk.append(mk.reshape(bp, n_mem, CA_HEADS, CA_HEAD_DIM))
        pmv.append(mv.reshape(bp, n_mem, CA_HEADS, CA_HEAD_DIM))
        xp = _cross_layer(xp, wc, l, mk, mv, batch=bp, seq=sp)
        xs = _cross_layer(xs, wc, l, cache_mem_k[l].reshape(bs, n_mem, ca_w), cache_mem_v[l].reshape(bs, n_mem, ca_w),
                          batch=bs, seq=ss)
    y_prompt = rmsnorm(xp, final_norm_w).reshape(bp, sp, d)
    y_sample = rmsnorm(xs, final_norm_w).reshape(bs, ss, d)
    return (y_prompt, y_sample,
            jnp.stack(pk), jnp.stack(pv), jnp.stack(pconv), jnp.stack(pssm), jnp.stack(psc),
            jnp.stack(pmk), jnp.stack(pmv),
            jnp.stack(sk), jnp.stack(sv), jnp.stack(sconv), jnp.stack(sssm), jnp.stack(ssc))
```

```python
import functools
import math

import numpy as np
import jax
import jax.numpy as jnp
from jax import lax
from jax.experimental import pallas as pl
from jax.experimental.pallas import tpu as pltpu

F32 = jnp.float32
BF16 = jnp.bfloat16
HIGHEST = lax.Precision.HIGHEST

NORM_EPS = 1e-5
NEG_INF = -1e30

LANES = 128
SUBLANES = 8
VMEM_LIMIT = 48 * 1024 * 1024

SSD_HEADDIM = 64
SSD_GROUPS = 4
SSD_STATE = 128
SSD_CONV = 4
SSD_CHUNK = 128
SC_WIDTH = 3
ATT_HEADS = 16
KV_HEADS = 4
HEAD_DIM = 128
Q_PER_KV = ATT_HEADS // KV_HEADS
MOBA_BLOCK = 256
MOBA_TOPK = 3
PAGE_SIZE = 128
PAGES_PER_BLOCK = MOBA_BLOCK // PAGE_SIZE
PAGE_ROWS = PAGE_SIZE * KV_HEADS
BLOCKS_PER_STEP = 8
PAGES_PER_STEP = BLOCKS_PER_STEP * PAGES_PER_BLOCK
CA_HEADS = 4
CA_HEAD_DIM = 128


def _cparams(*sem):
    return pltpu.CompilerParams(dimension_semantics=sem, vmem_limit_bytes=VMEM_LIMIT)


def _tile(n, pref):
    if n <= pref:
        return n
    t = pref
    while n % t:
        t //= 2
    return t


def _silu(x):
    return x / (1.0 + jnp.exp(-x))


def _dot(a, b, **kw):
    return jnp.dot(a, b, preferred_element_type=F32, **kw)


def _bf16_pieces(x):
    hi = x.astype(BF16)
    rest = x - hi.astype(F32)
    mid = rest.astype(BF16)
    lo = (rest - mid.astype(F32)).astype(BF16)
    return hi, mid, lo


def _dot_exact_left(x, sel):
    return sum(_dot(p, sel) for p in _bf16_pieces(x))


def _dot_exact_right(sel, x):
    return sum(_dot(sel, p) for p in _bf16_pieces(x))


def _dot_nt(a, b, **kw):
    return lax.dot_general(a, b, (((1,), (1,)), ((), ())), preferred_element_type=F32, **kw)


def _norm_mm_kernel(x_ref, nw_ref, w_ref, o_ref, xn_ref):
    @pl.when(pl.program_id(1) == 0)
    def _():
        x = x_ref[...]
        ms = jnp.mean(x * x, axis=-1, keepdims=True)
        xn_ref[...] = ((x * lax.rsqrt(ms + NORM_EPS)) * nw_ref[...]).astype(BF16)

    o_ref[...] = _dot(xn_ref[...], w_ref[...]).astype(o_ref.dtype)


def _norm_mm_tap_kernel(x_ref, nw_ref, w_ref, o_ref, t0_ref, t1_ref, xn_ref, *, tap_tile):
    _norm_mm_kernel(x_ref, nw_ref, w_ref, o_ref, xn_ref)

    @pl.when(pl.program_id(1) == tap_tile)
    def _():
        heads, hd = t0_ref.shape[1], t0_ref.shape[2]
        for h in range(heads):
            t0_ref[:, h, :] = o_ref[:, h * hd:(h + 1) * hd]
            t1_ref[:, h, :] = o_ref[:, (heads + h) * hd:(heads + h + 1) * hd]


def _layer_spec(layer, block, index, **kw):
    return pl.BlockSpec((None,) + block, lambda *g: (layer,) + index(*g), **kw)


def norm_matmul(x, norm_w, w, layer, *, tm=1024, tn=1024, out_dtype=F32, tap_col0=None):
    m, k = x.shape
    n = w.shape[2]
    tm, tn = _tile(m, tm), _tile(n, tn)
    in_specs = [pl.BlockSpec((tm, k), lambda i, j: (i, 0)),
                pl.BlockSpec((1, k), lambda i, j: (0, 0)),
                _layer_spec(layer, (k, tn), lambda i, j: (0, j))]
    out_spec = pl.BlockSpec((tm, tn), lambda i, j: (i, j))
    out_shape = jax.ShapeDtypeStruct((m, n), out_dtype)
    if tap_col0 is None:
        body, out_specs, out_shapes = _norm_mm_kernel, out_spec, out_shape
    else:
        assert tap_col0 % tn == 0 and tn % (2 * LANES) == 0
        body = functools.partial(_norm_mm_tap_kernel, tap_tile=tap_col0 // tn)
        tap_heads = tn // 2 // HEAD_DIM
        tap_spec = pl.BlockSpec((tm, tap_heads, HEAD_DIM), lambda i, j: (i, 0, 0))
        tap_shape = jax.ShapeDtypeStruct((m, tap_heads, HEAD_DIM), out_dtype)
        out_specs, out_shapes = (out_spec, tap_spec, tap_spec), (out_shape, tap_shape, tap_shape)
    return pl.pallas_call(
        body,
        out_shape=out_shapes,
        grid=(m // tm, n // tn),
        in_specs=in_specs,
        out_specs=out_specs,
        scratch_shapes=[pltpu.VMEM((tm, k), BF16)],
        compiler_params=_cparams("parallel", "arbitrary"),
        name="norm_matmul",
    )(x, norm_w.reshape(1, k), w)


def _mm_res_kernel(*refs, n_pairs):
    res_ref, o_ref = refs[2 * n_pairs], refs[2 * n_pairs + 1]
    acc = res_ref[...]
    for p in range(n_pairs):
        acc = acc + _dot(refs[2 * p][...], refs[2 * p + 1][...])
    o_ref[...] = acc


def matmul_residual(pairs, res, layer, *, tm=512):
    m, n = res.shape
    tm = _tile(m, tm)
    in_specs, args = [], []
    for a, w, rb in pairs:
        k = a.shape[1]
        assert w.shape[1] % k == 0 and w.shape[2] == n
        in_specs += [pl.BlockSpec((tm, k), lambda i: (i, 0)),
                     _layer_spec(layer, (k, n), lambda i, rb=rb: (rb, 0), pipeline_mode=pl.Buffered(1))]
        args += [a, w]
    in_specs.append(pl.BlockSpec((tm, n), lambda i: (i, 0)))
    return pl.pallas_call(
        functools.partial(_mm_res_kernel, n_pairs=len(pairs)),
        out_shape=jax.ShapeDtypeStruct((m, n), F32),
        grid=(m // tm,),
        in_specs=in_specs,
        out_specs=pl.BlockSpec((tm, n), lambda i: (i, 0)),
        compiler_params=_cparams("parallel"),
        name="matmul_residual",
    )(*args, res)


def _mm_kernel(a_ref, w_ref, o_ref):
    o_ref[...] = _dot(a_ref[...], w_ref[...]).astype(o_ref.dtype)


def matmul(a, w, layer, *, col0=0, n=None, tm=1024, tn=1024, out_dtype=F32):
    m, k = a.shape
    n = w.shape[2] - col0 if n is None else n
    tm, tn = _tile(m, tm), _tile(n, tn)
    assert col0 % tn == 0 and col0 + n <= w.shape[2]
    cb0 = col0 // tn
    return pl.pallas_call(
        _mm_kernel,
        out_shape=jax.ShapeDtypeStruct((m, n), out_dtype),
        grid=(m // tm, n // tn),
        in_specs=[pl.BlockSpec((tm, k), lambda i, j: (i, 0)),
                  _layer_spec(layer, (k, tn), lambda i, j: (0, cb0 + j))],
        out_specs=pl.BlockSpec((tm, tn), lambda i, j: (i, j)),
        compiler_params=_cparams("parallel", "parallel"),
        name="matmul",
    )(a, w)


def _rmsnorm_kernel(x_ref, nw_ref, o_ref):
    x = x_ref[...]
    ms = jnp.mean(x * x, axis=-1, keepdims=True)
    o_ref[...] = ((x * lax.rsqrt(ms + NORM_EPS)) * nw_ref[...]).astype(o_ref.dtype)


def rmsnorm(x, norm_w, *, tm=512, out_dtype=F32):
    m, k = x.shape
    tm = _tile(m, tm)
    return pl.pallas_call(
        _rmsnorm_kernel,
        out_shape=jax.ShapeDtypeStruct((m, k), out_dtype),
        grid=(m // tm,),
        in_specs=[pl.BlockSpec((tm, k), lambda i: (i, 0)),
                  pl.BlockSpec((1, k), lambda i: (0, 0))],
        out_specs=pl.BlockSpec((tm, k), lambda i: (i, 0)),
        compiler_params=_cparams("parallel"),
        name="rmsnorm",
    )(x, norm_w.reshape(1, k))


def _rows_padded(ref, q):
    v = ref[...]
    if v.shape[0] < q:
        v = jnp.concatenate([v, jnp.zeros((q - v.shape[0], v.shape[1]), v.dtype)], axis=0)
    return v


def _causal_conv_chunk(x_ref, buf_ref, cw_ref, cb_ref, q):
    x = _rows_padded(x_ref, q)
    buf_ref[SUBLANES:SUBLANES + q, :] = x
    y = cw_ref[SSD_CONV - 1:SSD_CONV, :] * x
    for k in range(1, SSD_CONV):
        y = y + cw_ref[SSD_CONV - 1 - k:SSD_CONV - k, :] * buf_ref[SUBLANES - k:SUBLANES - k + q, :]
    if cb_ref is not None:
        y = y + cb_ref[...]
    buf_ref[0:SUBLANES, :] = x[q - SUBLANES:q, :]
    return y


def _ssd_kernel(z_ref, xs_ref, b_ref, c_ref, dt_ref, cs_xs_ref, cs_b_ref, cs_c_ref, h0_ref,
                cw_xs_ref, cw_b_ref, cw_c_ref, cb_xs_ref, cb_b_ref, cb_c_ref,
                dtb_ref, alog_ref, dskip_ref, nw_ref, e_ref,
                y_ref, hfin_ref,
                ht_ref, buf_xs, buf_b, buf_c, *, q, n_valid, n_chunks):
    c = pl.program_id(1)
    n_heads = e_ref.shape[1] // SSD_HEADDIM
    gw = (n_heads // SSD_GROUPS) * SSD_HEADDIM

    @pl.when(c == 0)
    def _():
        ht_ref[...] = h0_ref[...].T
        buf_xs[0:SUBLANES, :] = cs_xs_ref[...]
        buf_b[0:SUBLANES, :] = cs_b_ref[...]
        buf_c[0:SUBLANES, :] = cs_c_ref[...]

    xs = _silu(_causal_conv_chunk(xs_ref, buf_xs, cw_xs_ref, cb_xs_ref, q))
    bm = _silu(_causal_conv_chunk(b_ref, buf_b, cw_b_ref, cb_b_ref, q))
    cm = _silu(_causal_conv_chunk(c_ref, buf_c, cw_c_ref, cb_c_ref, q))

    row = lax.broadcasted_iota(jnp.int32, (q, q), 0)
    col = lax.broadcasted_iota(jnp.int32, (q, q), 1)
    tril = row >= col

    dtv = _rows_padded(dt_ref, q) + dtb_ref[...]
    dt = jnp.maximum(dtv, 0.0) + jnp.log(1.0 + jnp.exp(-jnp.abs(dtv)))
    if n_valid < q:
        dt = jnp.where(lax.broadcasted_iota(jnp.int32, dt.shape, 0) < n_valid, dt, 0.0)
    a = dt * (-jnp.exp(alog_ref[...]))
    a_cs = _dot_exact_right(jnp.where(tril, 1.0, 0.0).astype(BF16), a)
    a_cs_t = a_cs.T
    expand = e_ref[...]
    dt_full = _dot_exact_left(dt, expand)
    acs_full = _dot_exact_left(a_cs, expand)
    tot_full = acs_full[q - 1:q, :]
    xr = xs * dt_full
    xr_dec = (xr * jnp.exp(tot_full - acs_full)).astype(BF16)
    exp_acs = jnp.exp(acs_full)
    exp_tot = jnp.exp(tot_full)

    lane = lax.broadcasted_iota(jnp.int32, (q, LANES), 1)
    lo_half = lane < SSD_HEADDIM
    heads_per_tile = LANES // SSD_HEADDIM

    y = dskip_ref[...] * xs
    y_parts = []
    for g in range(SSD_GROUPS):
        bg = bm[:, g * SSD_STATE:(g + 1) * SSD_STATE]
        cg = cm[:, g * SSD_STATE:(g + 1) * SSD_STATE].astype(BF16)
        bg_t = bg.T.astype(BF16)
        cb = _dot(cg, bg_t)
        ht_g = ht_ref[:, g * gw:(g + 1) * gw]
        y_g = _dot(cg, ht_g.astype(BF16)) * exp_acs[:, g * gw:(g + 1) * gw]
        tiles = []
        for t in range(gw // LANES):
            base = g * gw + t * LANES
            xr_t = xr[:, base:base + LANES]
            acc = None
            for e in range(heads_per_tile):
                h = base // SSD_HEADDIM + e
                seg = a_cs[:, h:h + 1] - a_cs_t[h:h + 1, :]
                lmat = jnp.where(tril, jnp.exp(seg), 0.0)
                in_head = lo_half if e == 0 else jnp.logical_not(lo_half)
                part = _dot((cb * lmat).astype(BF16), jnp.where(in_head, xr_t, 0.0).astype(BF16))
                acc = part if acc is None else acc + part
            tiles.append(acc)
        y_g = y_g + jnp.concatenate(tiles, axis=1)
        states = _dot(bg_t, xr_dec[:, g * gw:(g + 1) * gw])
        ht_ref[:, g * gw:(g + 1) * gw] = exp_tot[:, g * gw:(g + 1) * gw] * ht_g + states
        y_parts.append(y_g)
    y = y + jnp.concatenate(y_parts, axis=1)

    y = y * _silu(_rows_padded(z_ref, q))
    for g in range(SSD_GROUPS):
        yg = y[:, g * gw:(g + 1) * gw]
        ms = jnp.mean(yg * yg, axis=-1, keepdims=True)
        y_ref[:, g * gw:(g + 1) * gw] = ((yg * lax.rsqrt(ms + NORM_EPS)) * nw_ref[:, g * gw:(g + 1) * gw]).astype(y_ref.dtype)

    @pl.when(c == n_chunks - 1)
    def _():
        hfin_ref[...] = ht_ref[...].T


def ssd_mixer(proj, dt_raw, conv_state8, h0, wts, *, batch, seq):
    q = SSD_CHUNK
    n_valid = min(seq, q)
    assert seq % n_valid == 0 and n_valid % SUBLANES == 0
    nc = seq // n_valid
    assert n_valid == q or nc == 1
    inner = h0.shape[1]
    gn = SSD_GROUPS * SSD_STATE
    assert inner % gn == 0 and (inner // gn) * gn == inner
    kb = inner // gn
    rows = lambda b, c: b * nc + c
    full = lambda b, c: (0, 0)
    in_specs = [
        pl.BlockSpec((n_valid, inner), lambda b, c: (rows(b, c), 0)),
        pl.BlockSpec((n_valid, inner), lambda b, c: (rows(b, c), 1)),
        pl.BlockSpec((n_valid, gn), lambda b, c: (rows(b, c), 2 * kb)),
        pl.BlockSpec((n_valid, gn), lambda b, c: (rows(b, c), 2 * kb + 1)),
        pl.BlockSpec((n_valid, LANES), lambda b, c: (rows(b, c), 0)),
        pl.BlockSpec((None, SUBLANES, inner), lambda b, c: (b, 0, 0)),
        pl.BlockSpec((None, SUBLANES, gn), lambda b, c: (b, 0, kb)),
        pl.BlockSpec((None, SUBLANES, gn), lambda b, c: (b, 0, kb + 1)),
        pl.BlockSpec((None, inner, SSD_STATE), lambda b, c: (b, 0, 0)),
        pl.BlockSpec((SSD_CONV, inner), full),
        pl.BlockSpec((SSD_CONV, gn), lambda b, c: (0, kb)),
        pl.BlockSpec((SSD_CONV, gn), lambda b, c: (0, kb + 1)),
        pl.BlockSpec((1, inner), full),
        pl.BlockSpec((1, gn), lambda b, c: (0, kb)),
        pl.BlockSpec((1, gn), lambda b, c: (0, kb + 1)),
        pl.BlockSpec((1, LANES), full),
        pl.BlockSpec((1, LANES), full),
        pl.BlockSpec((1, inner), full),
        pl.BlockSpec((1, inner), full),
        pl.BlockSpec((LANES, inner), full),
    ]
    cw, cb = wts["conv_w"], wts["conv_b"]
    y, hfin = pl.pallas_call(
        functools.partial(_ssd_kernel, q=q, n_valid=n_valid, n_chunks=nc),
        out_shape=(jax.ShapeDtypeStruct((batch * nc * q, inner), BF16),
                   jax.ShapeDtypeStruct((batch, inner, SSD_STATE), F32)),
        grid=(batch, nc),
        in_specs=in_specs,
        out_specs=(pl.BlockSpec((q, inner), lambda b, c: (rows(b, c), 0)),
                   pl.BlockSpec((None, inner, SSD_STATE), lambda b, c: (b, 0, 0))),
        scratch_shapes=[pltpu.VMEM((SSD_STATE, inner), F32),
                        pltpu.VMEM((q + SUBLANES, inner), F32),
                        pltpu.VMEM((q + SUBLANES, gn), F32),
                        pltpu.VMEM((q + SUBLANES, gn), F32)],
        compiler_params=_cparams("parallel", "arbitrary"),
        name="ssd_mixer",
    )(proj, proj, proj, proj, dt_raw, conv_state8, conv_state8, conv_state8, h0,
      cw, cw, cw, cb, cb, cb, wts["dt_bias"], wts["a_log"], wts["d_skip"], wts["norm_w"], wts["expand"])
    return y, hfin


def _sc_mixer_kernel(xn_ref, wb_ref, wc_ref, wx_ref, wg_ref, st_ref, w_ref, y_ref, last_ref, buf_ref, *, tq):
    @pl.when(pl.program_id(2) == 0)
    def _():
        buf_ref[0:SUBLANES, :] = st_ref[...]

    xn = xn_ref[...]
    prod = _dot(xn, wc_ref[...]) * _dot(xn, wx_ref[...])
    buf_ref[SUBLANES:SUBLANES + tq, :] = prod
    u = w_ref[SC_WIDTH - 1:SC_WIDTH, :] * prod
    for k in range(1, SC_WIDTH):
        u = u + w_ref[SC_WIDTH - 1 - k:SC_WIDTH - k, :] * buf_ref[SUBLANES - k:SUBLANES - k + tq, :]
    tail = prod[tq - SUBLANES:tq, :]
    buf_ref[0:SUBLANES, :] = tail
    last_ref[...] = tail
    y_ref[...] = (_dot(xn, wb_ref[...]) * u * _silu(_dot(xn, wg_ref[...]))).astype(y_ref.dtype)


def sc_mixer(xn, w_sc, layer, state8, w, *, batch, seq, tq=1024, tc=512):
    k = xn.shape[1]
    dim = w.shape[1]
    tq = _tile(seq, tq)
    nt = seq // tq
    ncb = dim // tc
    assert dim % tc == 0 and tq % SUBLANES == 0 and w_sc.shape[2] == 4 * dim

    def part(p):
        return _layer_spec(layer, (k, tc), lambda j, b, t: (0, p * ncb + j))

    return pl.pallas_call(
        functools.partial(_sc_mixer_kernel, tq=tq),
        out_shape=(jax.ShapeDtypeStruct((batch * seq, dim), BF16),
                   jax.ShapeDtypeStruct((batch, SUBLANES, dim), F32)),
        grid=(ncb, batch, nt),
        in_specs=[pl.BlockSpec((tq, k), lambda j, b, t: (b * nt + t, 0)),
                  part(0), part(1), part(2), part(3),
                  pl.BlockSpec((None, SUBLANES, tc), lambda j, b, t: (b, 0, j)),
                  pl.BlockSpec((SC_WIDTH, tc), lambda j, b, t: (0, j))],
        out_specs=(pl.BlockSpec((tq, tc), lambda j, b, t: (b * nt + t, j)),
                   pl.BlockSpec((None, SUBLANES, tc), lambda j, b, t: (b, 0, j))),
        scratch_shapes=[pltpu.VMEM((tq + SUBLANES, tc), F32)],
        compiler_params=_cparams("parallel", "parallel", "arbitrary"),
        name="sc_mixer",
    )(xn, w_sc, w_sc, w_sc, w_sc, state8, w)


def _top_blocks(gate, n_valid, axis=1):
    idx = lax.broadcasted_iota(jnp.int32, gate.shape, axis).astype(F32)
    g = jnp.where(idx < jnp.asarray(n_valid, F32), gate, NEG_INF)
    sel = jnp.zeros(gate.shape, F32)
    for _ in range(MOBA_TOPK):
        m = jnp.max(g, axis=axis, keepdims=True)
        first = jnp.min(jnp.where(g == m, idx, float(gate.shape[axis])), axis=axis, keepdims=True)
        pick = idx == first
        sel = jnp.where(pick & (m > 0.5 * NEG_INF), 1.0, sel)
        g = jnp.where(pick, -jnp.inf, g)
    return sel


def _moba_prompt_kernel(slopes_ref, q_ref, k_ref, v_ref, g_ref, o_ref, kmean_ref, kb_ref, vt_ref, mval_ref, *, n_blocks):
    kv = pl.program_id(1)
    i = pl.program_id(2)
    blk = MOBA_BLOCK
    heads = range(Q_PER_KV)

    @pl.when(i == 0)
    def _():
        kmean_ref[...] = jnp.zeros(kmean_ref.shape, F32)
        for j in range(n_blocks):
            kj = k_ref[j * blk:(j + 1) * blk, :]
            kmean_ref[j:j + 1, :] = jnp.sum(kj, axis=0, keepdims=True) * (1.0 / blk)
            kb_ref[j] = kj.astype(BF16)
            vt_ref[j] = v_ref[j * blk:(j + 1) * blk, :].T.astype(BF16)

    rel = (lax.broadcasted_iota(jnp.int32, (blk, blk), 1)
           - lax.broadcasted_iota(jnp.int32, (blk, blk), 0)).astype(F32)
    blk_idx = lax.broadcasted_iota(jnp.int32, (kmean_ref.shape[0], blk), 0)
    back = (i - blk_idx).astype(F32) * float(blk)
    kmean = kmean_ref[...]

    qt, bias, diag = [], [], []
    for g in heads:
        slope = slopes_ref[kv * Q_PER_KV + g]
        q_t = (q_ref[:, g * HEAD_DIM:(g + 1) * HEAD_DIM] * (HEAD_DIM ** -0.5)).T
        gate = _dot(kmean, q_t, precision=HIGHEST)
        sel = _top_blocks(gate, i, axis=0)
        mval_ref[:, g * blk:(g + 1) * blk] = jnp.where(sel > 0.0, -slope * back, NEG_INF)
        bias.append(-slope * rel)
        diag.append(jnp.where(rel >= 0, -slope * rel, NEG_INF))
        qt.append(q_t.astype(BF16))
    qt = jnp.concatenate(qt, axis=1)
    bias = jnp.concatenate(bias, axis=1)

    s = _dot(kb_ref[i], qt) + jnp.concatenate(diag, axis=1)
    m0 = jnp.max(s, axis=0, keepdims=True)
    p = jnp.exp(s - m0)
    l0 = jnp.sum(p, axis=0, keepdims=True)
    acc0 = _dot(vt_ref[i], p.astype(BF16))

    def past_pair(t, carry):
        m, l, acc = carry
        js = (2 * t, 2 * t + 1)
        ss = [_dot(kb_ref[j], qt) + bias for j in js]
        masks = [mval_ref[pl.ds(j, 1), :] for j in js]
        m_new = m
        for s, mask in zip(ss, masks):
            m_new = jnp.maximum(m_new, jnp.max(s, axis=0, keepdims=True) + mask)
        l = jnp.exp(m - m_new) * l
        acc = jnp.exp(m - m_new) * acc
        for j, s, mask in zip(js, ss, masks):
            p = jnp.exp(s - (m_new - mask))
            l = l + jnp.sum(p, axis=0, keepdims=True)
            acc = acc + _dot(vt_ref[j], p.astype(BF16))
        return m_new, l, acc

    _, l, acc = lax.fori_loop(0, (i + 1) // 2, past_pair, (m0, l0, acc0))
    o_t = acc / l
    for g in heads:
        c = slice(g * HEAD_DIM, (g + 1) * HEAD_DIM)
        o_ref[:, c] = (o_t[:, g * blk:(g + 1) * blk].T * _silu(g_ref[:, c])).astype(o_ref.dtype)


def moba_prompt(proj, slopes, *, batch, seq):
    assert seq % MOBA_BLOCK == 0
    nb = seq // MOBA_BLOCK
    nbp = -(-nb // SUBLANES) * SUBLANES
    gw = Q_PER_KV * HEAD_DIM
    kcol = ATT_HEADS
    vcol = ATT_HEADS + KV_HEADS
    gcol = (ATT_HEADS + 2 * KV_HEADS) * HEAD_DIM // gw
    assert gcol * gw == (ATT_HEADS + 2 * KV_HEADS) * HEAD_DIM
    return pl.pallas_call(
        functools.partial(_moba_prompt_kernel, n_blocks=nb),
        out_shape=jax.ShapeDtypeStruct((batch * seq, ATT_HEADS * HEAD_DIM), BF16),
        grid=(batch, KV_HEADS, nb),
        in_specs=[pl.BlockSpec(memory_space=pltpu.SMEM),
                  pl.BlockSpec((MOBA_BLOCK, gw), lambda b, kv, i: (b * nb + i, kv)),
                  pl.BlockSpec((seq, HEAD_DIM), lambda b, kv, i: (b, kcol + kv)),
                  pl.BlockSpec((seq, HEAD_DIM), lambda b, kv, i: (b, vcol + kv)),
                  pl.BlockSpec((MOBA_BLOCK, gw), lambda b, kv, i: (b * nb + i, gcol + kv))],
        out_specs=pl.BlockSpec((MOBA_BLOCK, gw), lambda b, kv, i: (b * nb + i, kv)),
        scratch_shapes=[pltpu.VMEM((nbp, HEAD_DIM), F32),
                        pltpu.VMEM((nb, MOBA_BLOCK, HEAD_DIM), BF16),
                        pltpu.VMEM((nb, HEAD_DIM, MOBA_BLOCK), BF16),
                        pltpu.VMEM((nbp, Q_PER_KV * MOBA_BLOCK), F32)],
        compiler_params=_cparams("parallel", "parallel", "arbitrary"),
        name="moba_prompt",
    )(slopes, proj, proj, proj, proj)


def _page_specs(layer):
    return [pl.BlockSpec((None, None, PAGE_ROWS, HEAD_DIM),
                         lambda b, s, pt, p=p: (layer, pt[b, s * PAGES_PER_STEP + p], 0, 0))
            for p in range(PAGES_PER_STEP)]


def _kmean_kernel(pt_ref, *refs):
    del pt_ref
    pages, o_ref = refs[:PAGES_PER_STEP], refs[PAGES_PER_STEP]
    for bb in range(BLOCKS_PER_STEP):
        tot = None
        for pg in range(PAGES_PER_BLOCK):
            page = pages[bb * PAGES_PER_BLOCK + pg][...]
            part = jnp.sum(page.reshape(PAGE_ROWS // SUBLANES, SUBLANES, HEAD_DIM), axis=0)
            tot = part if tot is None else tot + part
        per_kv = tot[0:KV_HEADS]
        for par in range(1, SUBLANES // KV_HEADS):
            per_kv = per_kv + tot[par * KV_HEADS:(par + 1) * KV_HEADS]
        o_ref[bb] = per_kv * (1.0 / MOBA_BLOCK)


def paged_block_means(k_pools, page_table, *, layer, n_blocks):
    db = page_table.shape[0]
    assert n_blocks % BLOCKS_PER_STEP == 0 and SUBLANES % KV_HEADS == 0
    return pl.pallas_call(
        _kmean_kernel,
        out_shape=jax.ShapeDtypeStruct((db, n_blocks, KV_HEADS, HEAD_DIM), F32),
        grid_spec=pltpu.PrefetchScalarGridSpec(
            num_scalar_prefetch=1,
            grid=(db, n_blocks // BLOCKS_PER_STEP),
            in_specs=_page_specs(layer),
            out_specs=pl.BlockSpec((None, BLOCKS_PER_STEP, KV_HEADS, HEAD_DIM), lambda b, s, pt: (b, s, 0, 0))),
        compiler_params=_cparams("parallel", "arbitrary"),
        name="paged_block_means",
    )(page_table, *([k_pools] * PAGES_PER_STEP))


def _moba_sample_kernel(pt_ref, q_ref, kmean_ref, slope_ref, knew_ref, vnew_ref, *refs,
                        n_blocks, past_len, dec_seq):
    del pt_ref
    k_pages, v_pages = refs[:PAGES_PER_STEP], refs[PAGES_PER_STEP:2 * PAGES_PER_STEP]
    o_ref, m_ref, l_ref, acc_ref, sel_ref, base_ref = refs[2 * PAGES_PER_STEP:]
    step = pl.program_id(1)
    rows = q_ref.shape[0]
    rpk = rows // KV_HEADS
    q = q_ref[...] * (HEAD_DIM ** -0.5)
    qb = q.astype(BF16)
    slope = slope_ref[:, 0:1]
    lane = lax.broadcasted_iota(jnp.int32, (rows, LANES), 1)
    t_row = jnp.bitwise_and(lax.broadcasted_iota(jnp.int32, (rows, LANES), 0), dec_seq - 1)

    def per_head(fn):
        return jnp.concatenate([fn(h, slice(h * rpk, (h + 1) * rpk)) for h in range(KV_HEADS)], axis=0)

    def page_head(page_ref, h):
        return page_ref[pl.ds(h, PAGE_SIZE, stride=KV_HEADS), :].astype(BF16)

    @pl.when(step == 0)
    def _():
        gate = per_head(lambda h, r: _dot_nt(q[r, :], kmean_ref[h], precision=HIGHEST))
        sel_ref[...] = _top_blocks(gate, n_blocks)
        base_ref[...] = -slope * (past_len + t_row - lane).astype(F32)
        dist = (t_row - lane).astype(F32)
        s = (per_head(lambda h, r: _dot_nt(qb[r, :], knew_ref[h].astype(BF16)))
             + jnp.where(dist >= 0, -slope * dist, NEG_INF))
        m = jnp.max(s, axis=1, keepdims=True)
        p = jnp.exp(s - m)
        pb = p.astype(BF16)
        m_ref[...] = jnp.broadcast_to(m, m_ref.shape)
        l_ref[...] = jnp.broadcast_to(jnp.sum(p, axis=1, keepdims=True), l_ref.shape)
        acc_ref[...] = per_head(lambda h, r: _dot(pb[r, :], vnew_ref[h].astype(BF16)))

    sel = sel_ref[...]
    base = base_ref[...]
    m, l, acc = m_ref[:, 0:1], l_ref[:, 0:1], acc_ref[...]
    scores, offsets = [], []
    m_new = m
    for bb in range(BLOCKS_PER_STEP):
        blk_id = step * BLOCKS_PER_STEP + bb
        chosen = jnp.sum(jnp.where(lane == blk_id, sel, 0.0), axis=1, keepdims=True) > 0.0
        for pg in range(PAGES_PER_BLOCK):
            page_pos = (blk_id * MOBA_BLOCK + pg * PAGE_SIZE).astype(F32)
            mcol = jnp.where(chosen, slope * page_pos, NEG_INF)
            k_page = k_pages[bb * PAGES_PER_BLOCK + pg]
            s = per_head(lambda h, r: _dot_nt(qb[r, :], page_head(k_page, h))) + base
            m_new = jnp.maximum(m_new, jnp.max(s, axis=1, keepdims=True) + mcol)
            scores.append(s)
            offsets.append(mcol)
    alpha = jnp.exp(m - m_new)
    l, acc, m = alpha * l, alpha * acc, m_new
    for pg, (s, mcol) in enumerate(zip(scores, offsets)):
        p = jnp.exp(s - (m - mcol))
        pb = p.astype(BF16)
        l = l + jnp.sum(p, axis=1, keepdims=True)
        acc = acc + per_head(lambda h, r: _dot(pb[r, :], page_head(v_pages[pg], h)))
    m_ref[...] = jnp.broadcast_to(m, m_ref.shape)
    l_ref[...] = jnp.broadcast_to(l, l_ref.shape)
    acc_ref[...] = acc

    @pl.when(step == n_blocks // BLOCKS_PER_STEP - 1)
    def _():
        o_ref[...] = acc / l


def moba_sample(q_rows, kmean_t, slope_rows, k_new_pad, v_new_pad, k_pools, v_pools, page_table,
                *, layer, n_blocks, past_len, dec_seq):
    db, rows, _ = q_rows.shape
    rpk = rows // KV_HEADS
    for n in (KV_HEADS, dec_seq, rpk):
        assert n & (n - 1) == 0
    assert n_blocks % BLOCKS_PER_STEP == 0 and dec_seq * KV_HEADS <= LANES
    per_b = lambda *shape: pl.BlockSpec((None,) + shape, lambda b, s, pt: (b,) + (0,) * len(shape))
    return pl.pallas_call(
        functools.partial(_moba_sample_kernel, n_blocks=n_blocks, past_len=past_len, dec_seq=dec_seq),
        out_shape=jax.ShapeDtypeStruct((db, rows, HEAD_DIM), F32),
        grid_spec=pltpu.PrefetchScalarGridSpec(
            num_scalar_prefetch=1,
            grid=(db, n_blocks // BLOCKS_PER_STEP),
            in_specs=[per_b(rows, HEAD_DIM),
                      per_b(KV_HEADS, LANES, HEAD_DIM),
                      pl.BlockSpec((rows, LANES), lambda b, s, pt: (0, 0)),
                      per_b(KV_HEADS, LANES, HEAD_DIM),
                      per_b(KV_HEADS, LANES, HEAD_DIM)] + _page_specs(layer) + _page_specs(layer),
            out_specs=per_b(rows, HEAD_DIM),
            scratch_shapes=[pltpu.VMEM((rows, LANES), F32), pltpu.VMEM((rows, LANES), F32),
                            pltpu.VMEM((rows, HEAD_DIM), F32), pltpu.VMEM((rows, LANES), F32),
                            pltpu.VMEM((rows, PAGE_SIZE), F32)]),
        compiler_params=_cparams("parallel", "arbitrary"),
        name="moba_sample",
    )(page_table, q_rows, kmean_t, slope_rows, k_new_pad, v_new_pad,
      *([k_pools] * PAGES_PER_STEP), *([v_pools] * PAGES_PER_STEP))


def _gate_kernel(o_ref, g_ref, y_ref):
    y_ref[...] = (o_ref[...] * _silu(g_ref[...])).astype(y_ref.dtype)


def silu_gate(o, proj, *, col0, tn=1024):
    m, n = o.shape
    assert col0 % tn == 0 and n % tn == 0
    cb = col0 // tn
    return pl.pallas_call(
        _gate_kernel,
        out_shape=jax.ShapeDtypeStruct((m, n), BF16),
        grid=(n // tn,),
        in_specs=[pl.BlockSpec((m, tn), lambda j: (0, j)),
                  pl.BlockSpec((m, tn), lambda j: (0, cb + j))],
        out_specs=pl.BlockSpec((m, tn), lambda j: (0, j)),
        compiler_params=_cparams("parallel"),
        name="silu_gate",
    )(o, proj)


def _cross_layer_kernel(x_ref, nw_ref, wq_ref, k_ref, v_ref, wo_ref, o_ref):
    width = CA_HEADS * CA_HEAD_DIM
    x = x_ref[...]
    ms = jnp.mean(x * x, axis=-1, keepdims=True)
    xn = ((x * lax.rsqrt(ms + NORM_EPS)) * nw_ref[...]).astype(BF16)
    qg = _dot(xn, wq_ref[...])
    gated = []
    for h in range(CA_HEADS):
        c = slice(h * CA_HEAD_DIM, (h + 1) * CA_HEAD_DIM)
        q = (qg[:, c] * (CA_HEAD_DIM ** -0.5)).astype(BF16)
        s = _dot_nt(q, k_ref[:, c].astype(BF16))
        m = jnp.max(s, axis=1, keepdims=True)
        p = jnp.exp(s - m)
        l = jnp.sum(p, axis=1, keepdims=True)
        o = _dot(p.astype(BF16), v_ref[:, c].astype(BF16)) / l
        g = qg[:, width + h * CA_HEAD_DIM:width + (h + 1) * CA_HEAD_DIM]
        gated.append((o * _silu(g)).astype(BF16))
    o_ref[...] = x + _dot(jnp.concatenate(gated, axis=1), wo_ref[...])


def cross_layer(x, norm_w, w_q, mem_k, mem_v, w_out, layer, *, batch, seq, tm=512):
    width = CA_HEADS * CA_HEAD_DIM
    d = x.shape[1]
    n_mem = mem_k.shape[1]
    tm = _tile(seq, tm)
    nt = seq // tm
    const = lambda i: (0, 0)
    return pl.pallas_call(
        _cross_layer_kernel,
        out_shape=jax.ShapeDtypeStruct((batch * seq, d), F32),
        grid=(batch * nt,),
        in_specs=[pl.BlockSpec((tm, d), lambda i: (i, 0)),
                  pl.BlockSpec((1, d), const),
                  _layer_spec(layer, (d, 2 * width), const),
                  pl.BlockSpec((None, n_mem, width), lambda i: (i // nt, 0, 0)),
                  pl.BlockSpec((None, n_mem, width), lambda i: (i // nt, 0, 0)),
                  _layer_spec(layer, (width, d), const)],
        out_specs=pl.BlockSpec((tm, d), lambda i: (i, 0)),
        compiler_params=_cparams("parallel"),
        name="cross_layer",
    )(x, norm_w.reshape(1, d), w_q, mem_k, mem_v, w_out)


def _alibi_slopes():
    return np.array([2.0 ** (-8.0 * (h + 1) / ATT_HEADS) for h in range(ATT_HEADS)], dtype=np.float32)


def _pad_rows(a, n, front=False):
    extra = n - a.shape[-2]
    pad = [(0, 0)] * a.ndim
    pad[-2] = (extra, 0) if front else (0, extra)
    return jnp.pad(a, pad)


def _pad_lanes(a):
    pad = [(0, 0)] * a.ndim
    pad[-1] = (0, LANES - a.shape[-1])
    return jnp.pad(a, pad)


def _even_layer(x, w, layer, conv_state, ssm_state, sc_state, *, batch, seq):
    d = x.shape[1]
    inner = d
    conv_ch = inner + 2 * SSD_GROUPS * SSD_STATE
    xn = rmsnorm(x, w["norm_mix"], out_dtype=BF16)
    proj = matmul(xn, w["w_in"], layer, n=inner + conv_ch)
    dt_raw = matmul(xn, w["w_in"], layer, col0=inner + conv_ch, n=LANES)
    new_conv = proj.reshape(batch, seq, -1)[:, seq - (SSD_CONV - 1):, inner:inner + conv_ch]

    y, h_fin = ssd_mixer(proj, dt_raw, _pad_rows(conv_state, SUBLANES, front=True),
                         ssm_state.reshape(batch, inner, SSD_STATE), w, batch=batch, seq=seq)
    if y.shape[0] != batch * seq:
        y = y.reshape(batch, -1, inner)[:, :seq].reshape(batch * seq, inner)

    y_sc, sc_tail = sc_mixer(xn, w["w_sc"], layer, _pad_rows(sc_state, SUBLANES, front=True), w["sc_w"],
                             batch=batch, seq=seq)
    x = matmul_residual([(y, w["w_out"], 0), (y_sc, w["w_out"], 1)], x, layer)
    return (x, new_conv, h_fin.reshape(batch, inner // SSD_HEADDIM, SSD_HEADDIM, SSD_STATE),
            sc_tail[:, SUBLANES - (SC_WIDTH - 1):, :])


def _odd_prompt(x, w, slopes, *, layer, batch, seq):
    att_q = ATT_HEADS * HEAD_DIM
    proj, k, v = norm_matmul(x, w["norm_mix"], w["w_in"], layer, tap_col0=att_q)
    og = moba_prompt(proj, slopes, batch=batch, seq=seq)
    return (matmul_residual([(og, w["w_out"], 0)], x, layer),
            k.reshape(batch, seq, KV_HEADS, HEAD_DIM), v.reshape(batch, seq, KV_HEADS, HEAD_DIM))


def _odd_sample(x, w, k_pools, v_pools, page_table, slope_rows, *, layer, batch, seq, past_len):
    assert past_len % MOBA_BLOCK == 0 and seq <= MOBA_BLOCK and seq <= LANES
    n_blocks = past_len // MOBA_BLOCK
    assert MOBA_TOPK <= n_blocks <= LANES
    att_q, att_kv = ATT_HEADS * HEAD_DIM, KV_HEADS * HEAD_DIM
    proj, k_new, v_new = norm_matmul(x, w["norm_mix"], w["w_in"], layer, tap_col0=att_q)
    q_rows = proj[:, :att_q].reshape(batch, seq, KV_HEADS, Q_PER_KV, HEAD_DIM).transpose(0, 2, 3, 1, 4)
    q_rows = q_rows.reshape(batch, ATT_HEADS * seq, HEAD_DIM)
    kmean = paged_block_means(k_pools, page_table, layer=layer, n_blocks=n_blocks)
    new_keys = lambda a: _pad_rows(a.reshape(batch, seq, KV_HEADS, HEAD_DIM).transpose(0, 2, 1, 3), LANES)
    o = moba_sample(q_rows, _pad_rows(kmean.transpose(0, 2, 1, 3), LANES), slope_rows,
                    new_keys(k_new), new_keys(v_new),
                    k_pools, v_pools, page_table, layer=layer, n_blocks=n_blocks, past_len=past_len, dec_seq=seq)
    o = o.reshape(batch, KV_HEADS, Q_PER_KV, seq, HEAD_DIM).transpose(0, 3, 1, 2, 4).reshape(batch * seq, att_q)
    og = silu_gate(o, proj, col0=att_q + 2 * att_kv)
    return (matmul_residual([(og, w["w_out"], 0)], x, layer),
            k_new.reshape(batch, seq, KV_HEADS, HEAD_DIM), v_new.reshape(batch, seq, KV_HEADS, HEAD_DIM))


def _cross_layer(x, w, layer, mem_k, mem_v, *, batch, seq):
    return cross_layer(x, w["norm_cross"], w["ca_w_q"], mem_k, mem_v, w["ca_w_out"], layer, batch=batch, seq=seq)


def kernel(x_prompt, x_sample, mem_prompt, cache_k, cache_v, page_table, state_conv, state_ssm, state_sc,
           cache_mem_k, cache_mem_v, norm_mix_w, norm_cross_w, norm_mem_w, final_norm_w,
           ev_w_in, ev_conv_w, ev_conv_b, ev_dt_bias, ev_a_log, ev_d_skip, ev_norm_w, ev_sc_w, ev_w_out,
           od_w_in, od_w_out, ca_w_q, ca_w_kv, ca_w_out):
    bp, sp, d = x_prompt.shape
    bs, ss, _ = x_sample.shape
    depth = norm_mix_w.shape[0]
    n_mem = mem_prompt.shape[1]
    n_heads = ev_dt_bias.shape[1]
    inner = n_heads * SSD_HEADDIM
    conv_ch = inner + 2 * SSD_GROUPS * SSD_STATE
    past_len = page_table.shape[1] * PAGE_SIZE
    ca_w = CA_HEADS * CA_HEAD_DIM
    assert inner == d and n_heads <= LANES

    slopes = jnp.asarray(_alibi_slopes())
    slope_rows = jnp.asarray(np.repeat(_alibi_slopes(), ss)[:, None] * np.ones((1, LANES), np.float32))
    expand_np = np.zeros((LANES, inner), np.float32)
    for hd in range(n_heads):
        expand_np[hd, hd * SSD_HEADDIM:(hd + 1) * SSD_HEADDIM] = 1.0
    expand = jnp.asarray(expand_np, dtype=BF16)

    xp = x_prompt.reshape(bp * sp, d)
    xs = x_sample.reshape(bs * ss, d)
    mem = mem_prompt.reshape(bp * n_mem, d)
    k_pools = cache_k.reshape(cache_k.shape[0], cache_k.shape[1], PAGE_ROWS, HEAD_DIM)
    v_pools = cache_v.reshape(cache_v.shape[0], cache_v.shape[1], PAGE_ROWS, HEAD_DIM)

    dt0 = inner + conv_ch
    ev_w_in_b = ev_w_in[:, :, :dt0 + LANES].astype(BF16)
    ev_w_sc_b = ev_w_in[:, :, dt0 + n_heads:].astype(BF16)
    ev_w_out_b = ev_w_out.astype(BF16)
    od_w_in_b, od_w_out_b = od_w_in.astype(BF16), od_w_out.astype(BF16)
    ca_w_q_b, ca_w_kv_b, ca_w_out_b = ca_w_q.astype(BF16), ca_w_kv.astype(BF16), ca_w_out.astype(BF16)

    pk, pv, sk, sv = [], [], [], []
    pconv, pssm, psc, sconv, sssm, ssc = [], [], [], [], [], []
    pmk, pmv = [], []
    for l in range(depth):
        i = l // 2
        if l % 2 == 0:
            w = {
                "norm_mix": norm_mix_w[l],
                "w_in": ev_w_in_b,
                "w_sc": ev_w_sc_b,
                "w_out": ev_w_out_b,
                "conv_w": ev_conv_w[i],
                "conv_b": ev_conv_b[i].reshape(1, conv_ch),
                "dt_bias": _pad_lanes(ev_dt_bias[i].reshape(1, n_heads)),
                "a_log": _pad_lanes(ev_a_log[i].reshape(1, n_heads)),
                "d_skip": jnp.repeat(ev_d_skip[i], SSD_HEADDIM).reshape(1, inner),
                "norm_w": ev_norm_w[i].reshape(1, inner),
                "expand": expand,
                "sc_w": ev_sc_w[i],
            }
            xp, c1, s1, q1 = _even_layer(xp, w, i, jnp.zeros((bp, SSD_CONV - 1, conv_ch), F32),
                                         jnp.zeros((bp, n_heads, SSD_HEADDIM, SSD_STATE), F32),
                                         jnp.zeros((bp, SC_WIDTH - 1, d), F32), batch=bp, seq=sp)
            xs, c2, s2, q2 = _even_layer(xs, w, i, state_conv[i], state_ssm[i], state_sc[i], batch=bs, seq=ss)
            pconv.append(c1); pssm.append(s1); psc.append(q1)
            sconv.append(c2); sssm.append(s2); ssc.append(q2)
        else:
            w = {"norm_mix": norm_mix_w[l], "w_in": od_w_in_b, "w_out": od_w_out_b}
            xp, k1, v1 = _odd_prompt(xp, w, slopes, layer=i, batch=bp, seq=sp)
            xs, k2, v2 = _odd_sample(xs, w, k_pools, v_pools, page_table, slope_rows,
                                     layer=i, batch=bs, seq=ss, past_len=past_len)
            pk.append(k1); pv.append(v1); sk.append(k2); sv.append(v2)
        wc = {"norm_cross": norm_cross_w[l], "ca_w_q": ca_w_q_b, "ca_w_out": ca_w_out_b}
        mkv = norm_matmul(mem, norm_mem_w[l], ca_w_kv_b, l)
        mk = mkv[:, :ca_w].reshape(bp, n_mem, ca_w)
        mv = mkv[:, ca_w:].reshape(bp, n_mem, ca_w)
        pmk.append(mk.reshape(bp, n_mem, CA_HEADS, CA_HEAD_DIM))
        pmv.append(mv.reshape(bp, n_mem, CA_HEADS, CA_HEAD_DIM))
        xp = _cross_layer(xp, wc, l, mk, mv, batch=bp, seq=sp)
        xs = _cross_layer(xs, wc, l, cache_mem_k[l].reshape(bs, n_mem, ca_w), cache_mem_v[l].reshape(bs, n_mem, ca_w),
                          batch=bs, seq=ss)
    y_prompt = rmsnorm(xp, final_norm_w).reshape(bp, sp, d)
    y_sample = rmsnorm(xs, final_norm_w).reshape(bs, ss, d)
    return (y_prompt, y_sample,
            jnp.stack(pk), jnp.stack(pv), jnp.stack(pconv), jnp.stack(pssm), jnp.stack(psc),
            jnp.stack(pmk), jnp.stack(pmv),
            jnp.stack(sk), jnp.stack(sv), jnp.stack(sconv), jnp.stack(sssm), jnp.stack(ssc))
```

```python
import functools
import math

import numpy as np
import jax
import jax.numpy as jnp
from jax import lax
from jax.experimental import pallas as pl
from jax.experimental.pallas import tpu as pltpu

F32 = jnp.float32
BF16 = jnp.bfloat16
HIGHEST = lax.Precision.HIGHEST

NORM_EPS = 1e-5
NEG_INF = -1e30

LANES = 128
SUBLANES = 8
VMEM_LIMIT = 48 * 1024 * 1024

SSD_HEADDIM = 64
SSD_GROUPS = 4
SSD_STATE = 128
SSD_CONV = 4
SSD_CHUNK = 128
SC_WIDTH = 3
ATT_HEADS = 16
KV_HEADS = 4
HEAD_DIM = 128
Q_PER_KV = ATT_HEADS // KV_HEADS
MOBA_BLOCK = 256
MOBA_TOPK = 3
PAGE_SIZE = 128
PAGES_PER_BLOCK = MOBA_BLOCK // PAGE_SIZE
PAGE_ROWS = PAGE_SIZE * KV_HEADS
BLOCKS_PER_STEP = 8
PAGES_PER_STEP = BLOCKS_PER_STEP * PAGES_PER_BLOCK
CA_HEADS = 4
CA_HEAD_DIM = 128


def _cparams(*sem):
    return pltpu.CompilerParams(dimension_semantics=sem, vmem_limit_bytes=VMEM_LIMIT)


def _tile(n, pref):
    if n <= pref:
        return n
    t = pref
    while n % t:
        t //= 2
    return t


def _silu(x):
    return x / (1.0 + jnp.exp(-x))


def _dot(a, b, **kw):
    return jnp.dot(a, b, preferred_element_type=F32, **kw)


def _bf16_pieces(x):
    hi = x.astype(BF16)
    rest = x - hi.astype(F32)
    mid = rest.astype(BF16)
    lo = (rest - mid.astype(F32)).astype(BF16)
    return hi, mid, lo


def _dot_exact_left(x, sel):
    return sum(_dot(p, sel) for p in _bf16_pieces(x))


def _dot_exact_right(sel, x):
    return sum(_dot(sel, p) for p in _bf16_pieces(x))


def _dot_nt(a, b, **kw):
    return lax.dot_general(a, b, (((1,), (1,)), ((), ())), preferred_element_type=F32, **kw)


def _norm_mm_kernel(x_ref, nw_ref, w_ref, o_ref, xn_ref):
    @pl.when(pl.program_id(1) == 0)
    def _():
        x = x_ref[...]
        ms = jnp.mean(x * x, axis=-1, keepdims=True)
        xn_ref[...] = ((x * lax.rsqrt(ms + NORM_EPS)) * nw_ref[...]).astype(BF16)

    o_ref[...] = _dot(xn_ref[...], w_ref[...]).astype(o_ref.dtype)


def _norm_mm_tap_kernel(x_ref, nw_ref, w_ref, o_ref, t0_ref, t1_ref, xn_ref, *, tap_tile):
    _norm_mm_kernel(x_ref, nw_ref, w_ref, o_ref, xn_ref)

    @pl.when(pl.program_id(1) == tap_tile)
    def _():
        heads, hd = t0_ref.shape[1], t0_ref.shape[2]
        for h in range(heads):
            t0_ref[:, h, :] = o_ref[:, h * hd:(h + 1) * hd]
            t1_ref[:, h, :] = o_ref[:, (heads + h) * hd:(heads + h + 1) * hd]


def _layer_spec(layer, block, index, **kw):
    return pl.BlockSpec((None,) + block, lambda *g: (layer,) + index(*g), **kw)


def norm_matmul(x, norm_w, w, layer, *, tm=1024, tn=1024, out_dtype=F32, tap_col0=None):
    m, k = x.shape
    n = w.shape[2]
    tm, tn = _tile(m, tm), _tile(n, tn)
    in_specs = [pl.BlockSpec((tm, k), lambda i, j: (i, 0)),
                pl.BlockSpec((1, k), lambda i, j: (0, 0)),
                _layer_spec(layer, (k, tn), lambda i, j: (0, j))]
    out_spec = pl.BlockSpec((tm, tn), lambda i, j: (i, j))
    out_shape = jax.ShapeDtypeStruct((m, n), out_dtype)
    if tap_col0 is None:
        body, out_specs, out_shapes = _norm_mm_kernel, out_spec, out_shape
    else:
        assert tap_col0 % tn == 0 and tn % (2 * LANES) == 0
        body = functools.partial(_norm_mm_tap_kernel, tap_tile=tap_col0 // tn)
        tap_heads = tn // 2 // HEAD_DIM
        tap_spec = pl.BlockSpec((tm, tap_heads, HEAD_DIM), lambda i, j: (i, 0, 0))
        tap_shape = jax.ShapeDtypeStruct((m, tap_heads, HEAD_DIM), out_dtype)
        out_specs, out_shapes = (out_spec, tap_spec, tap_spec), (out_shape, tap_shape, tap_shape)
    return pl.pallas_call(
        body,
        out_shape=out_shapes,
        grid=(m // tm, n // tn),
        in_specs=in_specs,
        out_specs=out_specs,
        scratch_shapes=[pltpu.VMEM((tm, k), BF16)],
        compiler_params=_cparams("parallel", "arbitrary"),
        name="norm_matmul",
    )(x, norm_w.reshape(1, k), w)


def _mm_res_kernel(*refs, n_pairs):
    res_ref, o_ref = refs[2 * n_pairs], refs[2 * n_pairs + 1]
    acc = res_ref[...]
    for p in range(n_pairs):
        acc = acc + _dot(refs[2 * p][...], refs[2 * p + 1][...])
    o_ref[...] = acc


def matmul_residual(pairs, res, layer, *, tm=512):
    m, n = res.shape
    tm = _tile(m, tm)
    in_specs, args = [], []
    for a, w, rb in pairs:
        k = a.shape[1]
        assert w.shape[1] % k == 0 and w.shape[2] == n
        in_specs += [pl.BlockSpec((tm, k), lambda i: (i, 0)),
                     _layer_spec(layer, (k, n), lambda i, rb=rb: (rb, 0), pipeline_mode=pl.Buffered(1))]
        args += [a, w]
    in_specs.append(pl.BlockSpec((tm, n), lambda i: (i, 0)))
    return pl.pallas_call(
        functools.partial(_mm_res_kernel, n_pairs=len(pairs)),
        out_shape=jax.ShapeDtypeStruct((m, n), F32),
        grid=(m // tm,),
        in_specs=in_specs,
        out_specs=pl.BlockSpec((tm, n), lambda i: (i, 0)),
        compiler_params=_cparams("parallel"),
        name="matmul_residual",
    )(*args, res)


def _mm_kernel(a_ref, w_ref, o_ref):
    o_ref[...] = _dot(a_ref[...], w_ref[...]).astype(o_ref.dtype)


def matmul(a, w, layer, *, col0=0, n=None, tm=1024, tn=1024, out_dtype=F32):
    m, k = a.shape
    n = w.shape[2] - col0 if n is None else n
    tm, tn = _tile(m, tm), _tile(n, tn)
    assert col0 % tn == 0 and col0 + n <= w.shape[2]
    cb0 = col0 // tn
    return pl.pallas_call(
        _mm_kernel,
        out_shape=jax.ShapeDtypeStruct((m, n), out_dtype),
        grid=(m // tm, n // tn),
        in_specs=[pl.BlockSpec((tm, k), lambda i, j: (i, 0)),
                  _layer_spec(layer, (k, tn), lambda i, j: (0, cb0 + j))],
        out_specs=pl.BlockSpec((tm, tn), lambda i, j: (i, j)),
        compiler_params=_cparams("parallel", "parallel"),
        name="matmul",
    )(a, w)


def _cast_split_kernel(w_ref, left_ref, right_ref, *, right0):
    left_ref[...] = w_ref[:, :left_ref.shape[1]].astype(BF16)
    right_ref[...] = w_ref[:, right0:].astype(BF16)


def cast_split(w, n_left, right0, *, tk=256):
    layers, k, n = w.shape
    tk = _tile(k, tk)
    return pl.pallas_call(
        functools.partial(_cast_split_kernel, right0=right0),
        out_shape=(jax.ShapeDtypeStruct((layers, k, n_left), BF16),
                   jax.ShapeDtypeStruct((layers, k, n - right0), BF16)),
        grid=(layers, k // tk),
        in_specs=[pl.BlockSpec((None, tk, n), lambda l, i: (l, i, 0))],
        out_specs=(pl.BlockSpec((None, tk, n_left), lambda l, i: (l, i, 0)),
                   pl.BlockSpec((None, tk, n - right0), lambda l, i: (l, i, 0))),
        compiler_params=_cparams("parallel", "parallel"),
        name="cast_split",
    )(w)


def _rmsnorm_kernel(x_ref, nw_ref, o_ref):
    x = x_ref[...]
    ms = jnp.mean(x * x, axis=-1, keepdims=True)
    o_ref[...] = ((x * lax.rsqrt(ms + NORM_EPS)) * nw_ref[...]).astype(o_ref.dtype)


def rmsnorm(x, norm_w, *, tm=512, out_dtype=F32):
    m, k = x.shape
    tm = _tile(m, tm)
    return pl.pallas_call(
        _rmsnorm_kernel,
        out_shape=jax.ShapeDtypeStruct((m, k), out_dtype),
        grid=(m // tm,),
        in_specs=[pl.BlockSpec((tm, k), lambda i: (i, 0)),
                  pl.BlockSpec((1, k), lambda i: (0, 0))],
        out_specs=pl.BlockSpec((tm, k), lambda i: (i, 0)),
        compiler_params=_cparams("parallel"),
        name="rmsnorm",
    )(x, norm_w.reshape(1, k))


def _rows_padded(ref, q):
    v = ref[...]
    if v.shape[0] < q:
        v = jnp.concatenate([v, jnp.zeros((q - v.shape[0], v.shape[1]), v.dtype)], axis=0)
    return v


def _causal_conv_chunk(x_ref, buf_ref, cw_ref, cb_ref, q):
    x = _rows_padded(x_ref, q)
    buf_ref[SUBLANES:SUBLANES + q, :] = x
    y = cw_ref[SSD_CONV - 1:SSD_CONV, :] * x
    for k in range(1, SSD_CONV):
        y = y + cw_ref[SSD_CONV - 1 - k:SSD_CONV - k, :] * buf_ref[SUBLANES - k:SUBLANES - k + q, :]
    if cb_ref is not None:
        y = y + cb_ref[...]
    buf_ref[0:SUBLANES, :] = x[q - SUBLANES:q, :]
    return y


def _ssd_kernel(z_ref, xs_ref, b_ref, c_ref, dt_ref, cs_xs_ref, cs_b_ref, cs_c_ref, h0_ref,
                cw_xs_ref, cw_b_ref, cw_c_ref, cb_xs_ref, cb_b_ref, cb_c_ref,
                dtb_ref, alog_ref, dskip_ref, nw_ref, e_ref,
                y_ref, hfin_ref,
                ht_ref, buf_xs, buf_b, buf_c, *, q, n_valid, n_chunks):
    c = pl.program_id(1)
    n_heads = e_ref.shape[1] // SSD_HEADDIM
    gw = (n_heads // SSD_GROUPS) * SSD_HEADDIM

    @pl.when(c == 0)
    def _():
        ht_ref[...] = h0_ref[...].T
        buf_xs[0:SUBLANES, :] = cs_xs_ref[...]
        buf_b[0:SUBLANES, :] = cs_b_ref[...]
        buf_c[0:SUBLANES, :] = cs_c_ref[...]

    xs = _silu(_causal_conv_chunk(xs_ref, buf_xs, cw_xs_ref, cb_xs_ref, q))
    bm = _silu(_causal_conv_chunk(b_ref, buf_b, cw_b_ref, cb_b_ref, q))
    cm = _silu(_causal_conv_chunk(c_ref, buf_c, cw_c_ref, cb_c_ref, q))

    row = lax.broadcasted_iota(jnp.int32, (q, q), 0)
    col = lax.broadcasted_iota(jnp.int32, (q, q), 1)
    tril = row >= col

    dtv = _rows_padded(dt_ref, q) + dtb_ref[...]
    dt = jnp.maximum(dtv, 0.0) + jnp.log(1.0 + jnp.exp(-jnp.abs(dtv)))
    if n_valid < q:
        dt = jnp.where(lax.broadcasted_iota(jnp.int32, dt.shape, 0) < n_valid, dt, 0.0)
    a = dt * (-jnp.exp(alog_ref[...]))
    a_cs = _dot_exact_right(jnp.where(tril, 1.0, 0.0).astype(BF16), a)
    a_cs_t = a_cs.T
    expand = e_ref[...]
    dt_full = _dot_exact_left(dt, expand)
    acs_full = _dot_exact_left(a_cs, expand)
    tot_full = acs_full[q - 1:q, :]
    xr = xs * dt_full
    xr_dec = (xr * jnp.exp(tot_full - acs_full)).astype(BF16)
    exp_acs = jnp.exp(acs_full)
    exp_tot = jnp.exp(tot_full)

    lane = lax.broadcasted_iota(jnp.int32, (q, LANES), 1)
    lo_half = lane < SSD_HEADDIM
    heads_per_tile = LANES // SSD_HEADDIM

    y = dskip_ref[...] * xs
    y_parts = []
    for g in range(SSD_GROUPS):
        bg = bm[:, g * SSD_STATE:(g + 1) * SSD_STATE]
        cg = cm[:, g * SSD_STATE:(g + 1) * SSD_STATE].astype(BF16)
        bg_t = bg.T.astype(BF16)
        cb = _dot(cg, bg_t)
        ht_g = ht_ref[:, g * gw:(g + 1) * gw]
        y_g = _dot(cg, ht_g.astype(BF16)) * exp_acs[:, g * gw:(g + 1) * gw]
        tiles = []
        for t in range(gw // LANES):
            base = g * gw + t * LANES
            xr_t = xr[:, base:base + LANES]
            acc = None
            for e in range(heads_per_tile):
                h = base // SSD_HEADDIM + e
                seg = a_cs[:, h:h + 1] - a_cs_t[h:h + 1, :]
                lmat = jnp.where(tril, jnp.exp(seg), 0.0)
                in_head = lo_half if e == 0 else jnp.logical_not(lo_half)
                part = _dot((cb * lmat).astype(BF16), jnp.where(in_head, xr_t, 0.0).astype(BF16))
                acc = part if acc is None else acc + part
            tiles.append(acc)
        y_g = y_g + jnp.concatenate(tiles, axis=1)
        states = _dot(bg_t, xr_dec[:, g * gw:(g + 1) * gw])
        ht_ref[:, g * gw:(g + 1) * gw] = exp_tot[:, g * gw:(g + 1) * gw] * ht_g + states
        y_parts.append(y_g)
    y = y + jnp.concatenate(y_parts, axis=1)

    y = y * _silu(_rows_padded(z_ref, q))
    for g in range(SSD_GROUPS):
        yg = y[:, g * gw:(g + 1) * gw]
        ms = jnp.mean(yg * yg, axis=-1, keepdims=True)
        y_ref[:, g * gw:(g + 1) * gw] = ((yg * lax.rsqrt(ms + NORM_EPS)) * nw_ref[:, g * gw:(g + 1) * gw]).astype(y_ref.dtype)

    @pl.when(c == n_chunks - 1)
    def _():
        hfin_ref[...] = ht_ref[...].T


def ssd_mixer(proj, dt_raw, conv_state8, h0, wts, *, batch, seq):
    q = SSD_CHUNK
    n_valid = min(seq, q)
    assert seq % n_valid == 0 and n_valid % SUBLANES == 0
    nc = seq // n_valid
    assert n_valid == q or nc == 1
    inner = h0.shape[1]
    gn = SSD_GROUPS * SSD_STATE
    assert inner % gn == 0 and (inner // gn) * gn == inner
    kb = inner // gn
    rows = lambda b, c: b * nc + c
    full = lambda b, c: (0, 0)
    in_specs = [
        pl.BlockSpec((n_valid, inner), lambda b, c: (rows(b, c), 0)),
        pl.BlockSpec((n_valid, inner), lambda b, c: (rows(b, c), 1)),
        pl.BlockSpec((n_valid, gn), lambda b, c: (rows(b, c), 2 * kb)),
        pl.BlockSpec((n_valid, gn), lambda b, c: (rows(b, c), 2 * kb + 1)),
        pl.BlockSpec((n_valid, LANES), lambda b, c: (rows(b, c), 0)),
        pl.BlockSpec((None, SUBLANES, inner), lambda b, c: (b, 0, 0)),
        pl.BlockSpec((None, SUBLANES, gn), lambda b, c: (b, 0, kb)),
        pl.BlockSpec((None, SUBLANES, gn), lambda b, c: (b, 0, kb + 1)),
        pl.BlockSpec((None, inner, SSD_STATE), lambda b, c: (b, 0, 0)),
        pl.BlockSpec((SSD_CONV, inner), full),
        pl.BlockSpec((SSD_CONV, gn), lambda b, c: (0, kb)),
        pl.BlockSpec((SSD_CONV, gn), lambda b, c: (0, kb + 1)),
        pl.BlockSpec((1, inner), full),
        pl.BlockSpec((1, gn), lambda b, c: (0, kb)),
        pl.BlockSpec((1, gn), lambda b, c: (0, kb + 1)),
        pl.BlockSpec((1, LANES), full),
        pl.BlockSpec((1, LANES), full),
        pl.BlockSpec((1, inner), full),
        pl.BlockSpec((1, inner), full),
        pl.BlockSpec((LANES, inner), full),
    ]
    cw, cb = wts["conv_w"], wts["conv_b"]
    y, hfin = pl.pallas_call(
        functools.partial(_ssd_kernel, q=q, n_valid=n_valid, n_chunks=nc),
        out_shape=(jax.ShapeDtypeStruct((batch * nc * q, inner), BF16),
                   jax.ShapeDtypeStruct((batch, inner, SSD_STATE), F32)),
        grid=(batch, nc),
        in_specs=in_specs,
        out_specs=(pl.BlockSpec((q, inner), lambda b, c: (rows(b, c), 0)),
                   pl.BlockSpec((None, inner, SSD_STATE), lambda b, c: (b, 0, 0))),
        scratch_shapes=[pltpu.VMEM((SSD_STATE, inner), F32),
                        pltpu.VMEM((q + SUBLANES, inner), F32),
                        pltpu.VMEM((q + SUBLANES, gn), F32),
                        pltpu.VMEM((q + SUBLANES, gn), F32)],
        compiler_params=_cparams("parallel", "arbitrary"),
        name="ssd_mixer",
    )(proj, proj, proj, proj, dt_raw, conv_state8, conv_state8, conv_state8, h0,
      cw, cw, cw, cb, cb, cb, wts["dt_bias"], wts["a_log"], wts["d_skip"], wts["norm_w"], wts["expand"])
    return y, hfin


def _sc_mixer_kernel(xn_ref, wb_ref, wc_ref, wx_ref, wg_ref, st_ref, w_ref, y_ref, last_ref, buf_ref, *, tq, slabs):
    xn = xn_ref[...]
    prod = _dot(xn, wc_ref[...]) * _dot(xn, wx_ref[...])
    gate = _dot(xn, wb_ref[...]) * _silu(_dot(xn, wg_ref[...]))
    us = []
    for s in range(slabs):
        if slabs == 1:
            @pl.when(pl.program_id(2) == 0)
            def _():
                buf_ref[0:SUBLANES, :] = st_ref[0]
        else:
            buf_ref[0:SUBLANES, :] = st_ref[s]
        ps = prod[s * tq:(s + 1) * tq, :]
        buf_ref[SUBLANES:SUBLANES + tq, :] = ps
        u = w_ref[SC_WIDTH - 1:SC_WIDTH, :] * ps
        for k in range(1, SC_WIDTH):
            u = u + w_ref[SC_WIDTH - 1 - k:SC_WIDTH - k, :] * buf_ref[SUBLANES - k:SUBLANES - k + tq, :]
        tail = ps[tq - SUBLANES:tq, :]
        buf_ref[0:SUBLANES, :] = tail
        last_ref[s] = tail
        us.append(u)
    u = us[0] if slabs == 1 else jnp.concatenate(us, axis=0)
    y_ref[...] = (gate * u).astype(y_ref.dtype)


def sc_mixer(xn, w_sc, layer, state8, w, *, batch, seq, tq=1024, tc=512):
    k = xn.shape[1]
    dim = w.shape[1]
    slabs = batch if batch * seq <= tq else 1
    tq = seq if slabs > 1 else _tile(seq, tq)
    nt = seq // tq
    nb = batch // slabs
    ncb = dim // tc
    assert dim % tc == 0 and tq % SUBLANES == 0 and w_sc.shape[2] == 4 * dim

    def part(p):
        return _layer_spec(layer, (k, tc), lambda j, b, t: (0, p * ncb + j))

    return pl.pallas_call(
        functools.partial(_sc_mixer_kernel, tq=tq, slabs=slabs),
        out_shape=(jax.ShapeDtypeStruct((batch * seq, dim), BF16),
                   jax.ShapeDtypeStruct((batch, SUBLANES, dim), F32)),
        grid=(ncb, nb, nt),
        in_specs=[pl.BlockSpec((slabs * tq, k), lambda j, b, t: (b * nt + t, 0)),
                  part(0), part(1), part(2), part(3),
                  pl.BlockSpec((slabs, SUBLANES, tc), lambda j, b, t: (b, 0, j)),
                  pl.BlockSpec((SC_WIDTH, tc), lambda j, b, t: (0, j))],
        out_specs=(pl.BlockSpec((slabs * tq, tc), lambda j, b, t: (b * nt + t, j)),
                   pl.BlockSpec((slabs, SUBLANES, tc), lambda j, b, t: (b, 0, j))),
        scratch_shapes=[pltpu.VMEM((tq + SUBLANES, tc), F32)],
        compiler_params=_cparams("parallel", "parallel", "arbitrary"),
        name="sc_mixer",
    )(xn, w_sc, w_sc, w_sc, w_sc, state8, w)


def _top_blocks(gate, n_valid, axis=1):
    idx = lax.broadcasted_iota(jnp.int32, gate.shape, axis).astype(F32)
    g = jnp.where(idx < jnp.asarray(n_valid, F32), gate, NEG_INF)
    sel = jnp.zeros(gate.shape, F32)
    for _ in range(MOBA_TOPK):
        m = jnp.max(g, axis=axis, keepdims=True)
        first = jnp.min(jnp.where(g == m, idx, float(gate.shape[axis])), axis=axis, keepdims=True)
        pick = idx == first
        sel = jnp.where(pick & (m > 0.5 * NEG_INF), 1.0, sel)
        g = jnp.where(pick, -jnp.inf, g)
    return sel


def _moba_prompt_kernel(slopes_ref, q_ref, k_ref, v_ref, g_ref, o_ref, kmean_ref, kb_ref, vt_ref, mval_ref, *, n_blocks):
    kv = pl.program_id(1)
    i = pl.program_id(2)
    blk = MOBA_BLOCK
    heads = range(Q_PER_KV)

    @pl.when(i == 0)
    def _():
        kmean_ref[...] = jnp.zeros(kmean_ref.shape, F32)
        for j in range(n_blocks):
            kj = k_ref[j * blk:(j + 1) * blk, :]
            kmean_ref[j:j + 1, :] = jnp.sum(kj, axis=0, keepdims=True) * (1.0 / blk)
            kb_ref[j] = kj.astype(BF16)
            vt_ref[j] = v_ref[j * blk:(j + 1) * blk, :].T.astype(BF16)

    rel = (lax.broadcasted_iota(jnp.int32, (blk, blk), 1)
           - lax.broadcasted_iota(jnp.int32, (blk, blk), 0)).astype(F32)
    blk_idx = lax.broadcasted_iota(jnp.int32, (kmean_ref.shape[0], blk), 0)
    back = (i - blk_idx).astype(F32) * float(blk)
    kmean = kmean_ref[...]

    qt, bias, diag = [], [], []
    for g in heads:
        slope = slopes_ref[kv * Q_PER_KV + g]
        q_t = (q_ref[:, g * HEAD_DIM:(g + 1) * HEAD_DIM] * (HEAD_DIM ** -0.5)).T
        gate = _dot(kmean, q_t, precision=HIGHEST)
        sel = _top_blocks(gate, i, axis=0)
        mval_ref[:, g * blk:(g + 1) * blk] = jnp.where(sel > 0.0, -slope * back, NEG_INF)
        bias.append(-slope * rel)
        diag.append(jnp.where(rel >= 0, -slope * rel, NEG_INF))
        qt.append(q_t.astype(BF16))
    qt = jnp.concatenate(qt, axis=1)
    bias = jnp.concatenate(bias, axis=1)

    s = _dot(kb_ref[i], qt) + jnp.concatenate(diag, axis=1)
    m0 = jnp.max(s, axis=0, keepdims=True)
    p = jnp.exp(s - m0)
    l0 = jnp.sum(p, axis=0, keepdims=True)
    acc0 = _dot(vt_ref[i], p.astype(BF16))

    def past_pair(t, carry):
        m, l, acc = carry
        js = (2 * t, 2 * t + 1)
        ss = [_dot(kb_ref[j], qt) + bias for j in js]
        masks = [mval_ref[pl.ds(j, 1), :] for j in js]
        m_new = m
        for s, mask in zip(ss, masks):
            m_new = jnp.maximum(m_new, jnp.max(s, axis=0, keepdims=True) + mask)
        l = jnp.exp(m - m_new) * l
        acc = jnp.exp(m - m_new) * acc
        for j, s, mask in zip(js, ss, masks):
            p = jnp.exp(s - (m_new - mask))
            l = l + jnp.sum(p, axis=0, keepdims=True)
            acc = acc + _dot(vt_ref[j], p.astype(BF16))
        return m_new, l, acc

    _, l, acc = lax.fori_loop(0, (i + 1) // 2, past_pair, (m0, l0, acc0))
    o_t = acc / l
    for g in heads:
        c = slice(g * HEAD_DIM, (g + 1) * HEAD_DIM)
        o_ref[:, c] = (o_t[:, g * blk:(g + 1) * blk].T * _silu(g_ref[:, c])).astype(o_ref.dtype)


def moba_prompt(proj, slopes, *, batch, seq):
    assert seq % MOBA_BLOCK == 0
    nb = seq // MOBA_BLOCK
    nbp = -(-nb // SUBLANES) * SUBLANES
    gw = Q_PER_KV * HEAD_DIM
    kcol = ATT_HEADS
    vcol = ATT_HEADS + KV_HEADS
    gcol = (ATT_HEADS + 2 * KV_HEADS) * HEAD_DIM // gw
    assert gcol * gw == (ATT_HEADS + 2 * KV_HEADS) * HEAD_DIM
    return pl.pallas_call(
        functools.partial(_moba_prompt_kernel, n_blocks=nb),
        out_shape=jax.ShapeDtypeStruct((batch * seq, ATT_HEADS * HEAD_DIM), BF16),
        grid=(batch, KV_HEADS, nb),
        in_specs=[pl.BlockSpec(memory_space=pltpu.SMEM),
                  pl.BlockSpec((MOBA_BLOCK, gw), lambda b, kv, i: (b * nb + i, kv)),
                  pl.BlockSpec((seq, HEAD_DIM), lambda b, kv, i: (b, kcol + kv)),
                  pl.BlockSpec((seq, HEAD_DIM), lambda b, kv, i: (b, vcol + kv)),
                  pl.BlockSpec((MOBA_BLOCK, gw), lambda b, kv, i: (b * nb + i, gcol + kv))],
        out_specs=pl.BlockSpec((MOBA_BLOCK, gw), lambda b, kv, i: (b * nb + i, kv)),
        scratch_shapes=[pltpu.VMEM((nbp, HEAD_DIM), F32),
                        pltpu.VMEM((nb, MOBA_BLOCK, HEAD_DIM), BF16),
                        pltpu.VMEM((nb, HEAD_DIM, MOBA_BLOCK), BF16),
                        pltpu.VMEM((nbp, Q_PER_KV * MOBA_BLOCK), F32)],
        compiler_params=_cparams("parallel", "parallel", "arbitrary"),
        name="moba_prompt",
    )(slopes, proj, proj, proj, proj)


def _page_specs(layer):
    return [pl.BlockSpec((None, None, PAGE_ROWS, HEAD_DIM),
                         lambda b, s, pt, p=p: (layer, pt[b, s * PAGES_PER_STEP + p], 0, 0))
            for p in range(PAGES_PER_STEP)]


def _kmean_kernel(pt_ref, *refs):
    del pt_ref
    pages, o_ref = refs[:PAGES_PER_STEP], refs[PAGES_PER_STEP]
    for bb in range(BLOCKS_PER_STEP):
        tot = None
        for pg in range(PAGES_PER_BLOCK):
            page = pages[bb * PAGES_PER_BLOCK + pg][...]
            part = jnp.sum(page.reshape(PAGE_ROWS // SUBLANES, SUBLANES, HEAD_DIM), axis=0)
            tot = part if tot is None else tot + part
        per_kv = tot[0:KV_HEADS]
        for par in range(1, SUBLANES // KV_HEADS):
            per_kv = per_kv + tot[par * KV_HEADS:(par + 1) * KV_HEADS]
        o_ref[bb] = per_kv * (1.0 / MOBA_BLOCK)


def paged_block_means(k_pools, page_table, *, layer, n_blocks):
    db = page_table.shape[0]
    assert n_blocks % BLOCKS_PER_STEP == 0 and SUBLANES % KV_HEADS == 0
    return pl.pallas_call(
        _kmean_kernel,
        out_shape=jax.ShapeDtypeStruct((db, n_blocks, KV_HEADS, HEAD_DIM), F32),
        grid_spec=pltpu.PrefetchScalarGridSpec(
            num_scalar_prefetch=1,
            grid=(db, n_blocks // BLOCKS_PER_STEP),
            in_specs=_page_specs(layer),
            out_specs=pl.BlockSpec((None, BLOCKS_PER_STEP, KV_HEADS, HEAD_DIM), lambda b, s, pt: (b, s, 0, 0))),
        compiler_params=_cparams("parallel", "arbitrary"),
        name="paged_block_means",
    )(page_table, *([k_pools] * PAGES_PER_STEP))


def _moba_sample_kernel(pt_ref, q_ref, kmean_ref, slope_ref, knew_ref, vnew_ref, *refs,
                        n_blocks, past_len, dec_seq):
    del pt_ref
    k_pages, v_pages = refs[:PAGES_PER_STEP], refs[PAGES_PER_STEP:2 * PAGES_PER_STEP]
    o_ref, m_ref, l_ref, acc_ref, sel_ref, base_ref = refs[2 * PAGES_PER_STEP:]
    step = pl.program_id(1)
    rows = q_ref.shape[0]
    rpk = rows // KV_HEADS
    q = q_ref[...] * (HEAD_DIM ** -0.5)
    qb = q.astype(BF16)
    slope = slope_ref[:, 0:1]
    lane = lax.broadcasted_iota(jnp.int32, (rows, LANES), 1)
    t_row = jnp.bitwise_and(lax.broadcasted_iota(jnp.int32, (rows, LANES), 0), dec_seq - 1)

    def per_head(fn):
        return jnp.concatenate([fn(h, slice(h * rpk, (h + 1) * rpk)) for h in range(KV_HEADS)], axis=0)

    def page_head(page_ref, h):
        return page_ref[pl.ds(h, PAGE_SIZE, stride=KV_HEADS), :].astype(BF16)

    @pl.when(step == 0)
    def _():
        gate = per_head(lambda h, r: _dot_nt(q[r, :], kmean_ref[h], precision=HIGHEST))
        sel_ref[...] = _top_blocks(gate, n_blocks)
        base_ref[...] = -slope * (past_len + t_row - lane).astype(F32)
        dist = (t_row - lane).astype(F32)
        s = (per_head(lambda h, r: _dot_nt(qb[r, :], knew_ref[h].astype(BF16)))
             + jnp.where(dist >= 0, -slope * dist, NEG_INF))
        m = jnp.max(s, axis=1, keepdims=True)
        p = jnp.exp(s - m)
        pb = p.astype(BF16)
        m_ref[...] = jnp.broadcast_to(m, m_ref.shape)
        l_ref[...] = jnp.broadcast_to(jnp.sum(p, axis=1, keepdims=True), l_ref.shape)
        acc_ref[...] = per_head(lambda h, r: _dot(pb[r, :], vnew_ref[h].astype(BF16)))

    sel = sel_ref[...]
    base = base_ref[...]
    m, l, acc = m_ref[:, 0:1], l_ref[:, 0:1], acc_ref[...]
    scores, offsets = [], []
    m_new = m
    for bb in range(BLOCKS_PER_STEP):
        blk_id = step * BLOCKS_PER_STEP + bb
        chosen = jnp.sum(jnp.where(lane == blk_id, sel, 0.0), axis=1, keepdims=True) > 0.0
        for pg in range(PAGES_PER_BLOCK):
            page_pos = (blk_id * MOBA_BLOCK + pg * PAGE_SIZE).astype(F32)
            mcol = jnp.where(chosen, slope * page_pos, NEG_INF)
            k_page = k_pages[bb * PAGES_PER_BLOCK + pg]
            s = per_head(lambda h, r: _dot_nt(qb[r, :], page_head(k_page, h))) + base
            m_new = jnp.maximum(m_new, jnp.max(s, axis=1, keepdims=True) + mcol)
            scores.append(s)
            offsets.append(mcol)
    alpha = jnp.exp(m - m_new)
    l, acc, m = alpha * l, alpha * acc, m_new
    for pg, (s, mcol) in enumerate(zip(scores, offsets)):
        p = jnp.exp(s - (m - mcol))
        pb = p.astype(BF16)
        l = l + jnp.sum(p, axis=1, keepdims=True)
        acc = acc + per_head(lambda h, r: _dot(pb[r, :], page_head(v_pages[pg], h)))
    m_ref[...] = jnp.broadcast_to(m, m_ref.shape)
    l_ref[...] = jnp.broadcast_to(l, l_ref.shape)
    acc_ref[...] = acc

    @pl.when(step == n_blocks // BLOCKS_PER_STEP - 1)
    def _():
        o_ref[...] = acc / l


def moba_sample(q_rows, kmean_t, slope_rows, k_new_pad, v_new_pad, k_pools, v_pools, page_table,
                *, layer, n_blocks, past_len, dec_seq):
    db, rows, _ = q_rows.shape
    rpk = rows // KV_HEADS
    for n in (KV_HEADS, dec_seq, rpk):
        assert n & (n - 1) == 0
    assert n_blocks % BLOCKS_PER_STEP == 0 and dec_seq * KV_HEADS <= LANES
    per_b = lambda *shape: pl.BlockSpec((None,) + shape, lambda b, s, pt: (b,) + (0,) * len(shape))
    return pl.pallas_call(
        functools.partial(_moba_sample_kernel, n_blocks=n_blocks, past_len=past_len, dec_seq=dec_seq),
        out_shape=jax.ShapeDtypeStruct((db, rows, HEAD_DIM), F32),
        grid_spec=pltpu.PrefetchScalarGridSpec(
            num_scalar_prefetch=1,
            grid=(db, n_blocks // BLOCKS_PER_STEP),
            in_specs=[per_b(rows, HEAD_DIM),
                      per_b(KV_HEADS, LANES, HEAD_DIM),
                      pl.BlockSpec((rows, LANES), lambda b, s, pt: (0, 0)),
                      per_b(KV_HEADS, LANES, HEAD_DIM),
                      per_b(KV_HEADS, LANES, HEAD_DIM)] + _page_specs(layer) + _page_specs(layer),
            out_specs=per_b(rows, HEAD_DIM),
            scratch_shapes=[pltpu.VMEM((rows, LANES), F32), pltpu.VMEM((rows, LANES), F32),
                            pltpu.VMEM((rows, HEAD_DIM), F32), pltpu.VMEM((rows, LANES), F32),
                            pltpu.VMEM((rows, PAGE_SIZE), F32)]),
        compiler_params=_cparams("parallel", "arbitrary"),
        name="moba_sample",
    )(page_table, q_rows, kmean_t, slope_rows, k_new_pad, v_new_pad,
      *([k_pools] * PAGES_PER_STEP), *([v_pools] * PAGES_PER_STEP))


def _gate_kernel(o_ref, g_ref, y_ref):
    y_ref[...] = (o_ref[...] * _silu(g_ref[...])).astype(y_ref.dtype)


def silu_gate(o, proj, *, col0, tn=1024):
    m, n = o.shape
    assert col0 % tn == 0 and n % tn == 0
    cb = col0 // tn
    return pl.pallas_call(
        _gate_kernel,
        out_shape=jax.ShapeDtypeStruct((m, n), BF16),
        grid=(n // tn,),
        in_specs=[pl.BlockSpec((m, tn), lambda j: (0, j)),
                  pl.BlockSpec((m, tn), lambda j: (0, cb + j))],
        out_specs=pl.BlockSpec((m, tn), lambda j: (0, j)),
        compiler_params=_cparams("parallel"),
        name="silu_gate",
    )(o, proj)


def _cross_layer_kernel(x_ref, nw_ref, wq_ref, k_ref, v_ref, wo_ref, o_ref, *, slabs):
    width = CA_HEADS * CA_HEAD_DIM
    x = x_ref[...]
    rows = x.shape[0] // slabs
    ms = jnp.mean(x * x, axis=-1, keepdims=True)
    xn = ((x * lax.rsqrt(ms + NORM_EPS)) * nw_ref[...]).astype(BF16)
    qg = _dot(xn, wq_ref[...])
    gated_rows = []
    for s_idx in range(slabs):
        r = slice(s_idx * rows, (s_idx + 1) * rows)
        gated = []
        for h in range(CA_HEADS):
            c = slice(h * CA_HEAD_DIM, (h + 1) * CA_HEAD_DIM)
            q = (qg[r, c] * (CA_HEAD_DIM ** -0.5)).astype(BF16)
            s = _dot_nt(q, k_ref[s_idx, :, c].astype(BF16))
            m = jnp.max(s, axis=1, keepdims=True)
            p = jnp.exp(s - m)
            l = jnp.sum(p, axis=1, keepdims=True)
            o = _dot(p.astype(BF16), v_ref[s_idx, :, c].astype(BF16)) / l
            g = qg[r, width + h * CA_HEAD_DIM:width + (h + 1) * CA_HEAD_DIM]
            gated.append(o * _silu(g))
        gated_rows.append(jnp.concatenate(gated, axis=1))
    og = gated_rows[0] if slabs == 1 else jnp.concatenate(gated_rows, axis=0)
    o_ref[...] = x + _dot(og.astype(BF16), wo_ref[...])


def cross_layer(x, norm_w, w_q, mem_k, mem_v, w_out, layer, *, batch, seq, tm=512):
    width = CA_HEADS * CA_HEAD_DIM
    d = x.shape[1]
    n_mem = mem_k.shape[1]
    slabs = batch if batch * seq <= tm else 1
    tm = batch * seq if slabs > 1 else _tile(seq, tm)
    nt = batch * seq // tm // (batch // slabs)
    const = lambda i: (0, 0)
    return pl.pallas_call(
        functools.partial(_cross_layer_kernel, slabs=slabs),
        out_shape=jax.ShapeDtypeStruct((batch * seq, d), F32),
        grid=(batch * seq // tm,),
        in_specs=[pl.BlockSpec((tm, d), lambda i: (i, 0)),
                  pl.BlockSpec((1, d), const),
                  _layer_spec(layer, (d, 2 * width), const),
                  pl.BlockSpec((slabs, n_mem, width), lambda i: (i // nt, 0, 0)),
                  pl.BlockSpec((slabs, n_mem, width), lambda i: (i // nt, 0, 0)),
                  _layer_spec(layer, (width, d), const)],
        out_specs=pl.BlockSpec((tm, d), lambda i: (i, 0)),
        compiler_params=_cparams("parallel"),
        name="cross_layer",
    )(x, norm_w.reshape(1, d), w_q, mem_k, mem_v, w_out)


def _alibi_slopes():
    return np.array([2.0 ** (-8.0 * (h + 1) / ATT_HEADS) for h in range(ATT_HEADS)], dtype=np.float32)


def _pad_rows(a, n, front=False):
    extra = n - a.shape[-2]
    pad = [(0, 0)] * a.ndim
    pad[-2] = (extra, 0) if front else (0, extra)
    return jnp.pad(a, pad)


def _pad_lanes(a):
    pad = [(0, 0)] * a.ndim
    pad[-1] = (0, LANES - a.shape[-1])
    return jnp.pad(a, pad)


def _even_layer(x, w, layer, conv_state, ssm_state, sc_state, *, batch, seq):
    d = x.shape[1]
    inner = d
    conv_ch = inner + 2 * SSD_GROUPS * SSD_STATE
    xn = rmsnorm(x, w["norm_mix"], out_dtype=BF16)
    proj = matmul(xn, w["w_in"], layer, n=inner + conv_ch)
    dt_raw = matmul(xn, w["w_in"], layer, col0=inner + conv_ch, n=LANES)
    new_conv = proj.reshape(batch, seq, -1)[:, seq - (SSD_CONV - 1):, inner:inner + conv_ch]

    y, h_fin = ssd_mixer(proj, dt_raw, _pad_rows(conv_state, SUBLANES, front=True),
                         ssm_state.reshape(batch, inner, SSD_STATE), w, batch=batch, seq=seq)
    if y.shape[0] != batch * seq:
        y = y.reshape(batch, -1, inner)[:, :seq].reshape(batch * seq, inner)

    y_sc, sc_tail = sc_mixer(xn, w["w_sc"], layer, _pad_rows(sc_state, SUBLANES, front=True), w["sc_w"],
                             batch=batch, seq=seq)
    x = matmul_residual([(y, w["w_out"], 0), (y_sc, w["w_out"], 1)], x, layer)
    return (x, new_conv, h_fin.reshape(batch, inner // SSD_HEADDIM, SSD_HEADDIM, SSD_STATE),
            sc_tail[:, SUBLANES - (SC_WIDTH - 1):, :])


def _odd_prompt(x, w, slopes, *, layer, batch, seq):
    att_q = ATT_HEADS * HEAD_DIM
    proj, k, v = norm_matmul(x, w["norm_mix"], w["w_in"], layer, tap_col0=att_q)
    og = moba_prompt(proj, slopes, batch=batch, seq=seq)
    return (matmul_residual([(og, w["w_out"], 0)], x, layer),
            k.reshape(batch, seq, KV_HEADS, HEAD_DIM), v.reshape(batch, seq, KV_HEADS, HEAD_DIM))


def _odd_sample(x, w, k_pools, v_pools, page_table, slope_rows, *, layer, batch, seq, past_len):
    assert past_len % MOBA_BLOCK == 0 and seq <= MOBA_BLOCK and seq <= LANES
    n_blocks = past_len // MOBA_BLOCK
    assert MOBA_TOPK <= n_blocks <= LANES
    att_q, att_kv = ATT_HEADS * HEAD_DIM, KV_HEADS * HEAD_DIM
    proj, k_new, v_new = norm_matmul(x, w["norm_mix"], w["w_in"], layer, tap_col0=att_q)
    q_rows = proj[:, :att_q].reshape(batch, seq, KV_HEADS, Q_PER_KV, HEAD_DIM).transpose(0, 2, 3, 1, 4)
    q_rows = q_rows.reshape(batch, ATT_HEADS * seq, HEAD_DIM)
    kmean = paged_block_means(k_pools, page_table, layer=layer, n_blocks=n_blocks)
    new_keys = lambda a: _pad_rows(a.reshape(batch, seq, KV_HEADS, HEAD_DIM).transpose(0, 2, 1, 3), LANES)
    o = moba_sample(q_rows, _pad_rows(kmean.transpose(0, 2, 1, 3), LANES), slope_rows,
                    new_keys(k_new), new_keys(v_new),
                    k_pools, v_pools, page_table, layer=layer, n_blocks=n_blocks, past_len=past_len, dec_seq=seq)
    o = o.reshape(batch, KV_HEADS, Q_PER_KV, seq, HEAD_DIM).transpose(0, 3, 1, 2, 4).reshape(batch * seq, att_q)
    og = silu_gate(o, proj, col0=att_q + 2 * att_kv)
    return (matmul_residual([(og, w["w_out"], 0)], x, layer),
            k_new.reshape(batch, seq, KV_HEADS, HEAD_DIM), v_new.reshape(batch, seq, KV_HEADS, HEAD_DIM))


def _cross_layer(x, w, layer, mem_k, mem_v, *, batch, seq):
    return cross_layer(x, w["norm_cross"], w["ca_w_q"], mem_k, mem_v, w["ca_w_out"], layer, batch=batch, seq=seq)


def kernel(x_prompt, x_sample, mem_prompt, cache_k, cache_v, page_table, state_conv, state_ssm, state_sc,
           cache_mem_k, cache_mem_v, norm_mix_w, norm_cross_w, norm_mem_w, final_norm_w,
           ev_w_in, ev_conv_w, ev_conv_b, ev_dt_bias, ev_a_log, ev_d_skip, ev_norm_w, ev_sc_w, ev_w_out,
           od_w_in, od_w_out, ca_w_q, ca_w_kv, ca_w_out):
    bp, sp, d = x_prompt.shape
    bs, ss, _ = x_sample.shape
    depth = norm_mix_w.shape[0]
    n_mem = mem_prompt.shape[1]
    n_heads = ev_dt_bias.shape[1]
    inner = n_heads * SSD_HEADDIM
    conv_ch = inner + 2 * SSD_GROUPS * SSD_STATE
    past_len = page_table.shape[1] * PAGE_SIZE
    ca_w = CA_HEADS * CA_HEAD_DIM
    assert inner == d and n_heads <= LANES

    slopes = jnp.asarray(_alibi_slopes())
    slope_rows = jnp.asarray(np.repeat(_alibi_slopes(), ss)[:, None] * np.ones((1, LANES), np.float32))
    expand_np = np.zeros((LANES, inner), np.float32)
    for hd in range(n_heads):
        expand_np[hd, hd * SSD_HEADDIM:(hd + 1) * SSD_HEADDIM] = 1.0
    expand = jnp.asarray(expand_np, dtype=BF16)

    xp = x_prompt.reshape(bp * sp, d)
    xs = x_sample.reshape(bs * ss, d)
    mem = mem_prompt.reshape(bp * n_mem, d)
    k_pools = cache_k.reshape(cache_k.shape[0], cache_k.shape[1], PAGE_ROWS, HEAD_DIM)
    v_pools = cache_v.reshape(cache_v.shape[0], cache_v.shape[1], PAGE_ROWS, HEAD_DIM)

    dt0 = inner + conv_ch
    ev_w_in_b, ev_w_sc_b = cast_split(ev_w_in, dt0 + LANES, dt0 + n_heads)
    ev_w_out_b = ev_w_out.astype(BF16)
    od_w_in_b, od_w_out_b = od_w_in.astype(BF16), od_w_out.astype(BF16)
    ca_w_q_b, ca_w_kv_b, ca_w_out_b = ca_w_q.astype(BF16), ca_w_kv.astype(BF16), ca_w_out.astype(BF16)

    pk, pv, sk, sv = [], [], [], []
    pconv, pssm, psc, sconv, sssm, ssc = [], [], [], [], [], []
    pmk, pmv = [], []
    for l in range(depth):
        i = l // 2
        if l % 2 == 0:
            w = {
                "norm_mix": norm_mix_w[l],
                "w_in": ev_w_in_b,
                "w_sc": ev_w_sc_b,
                "w_out": ev_w_out_b,
                "conv_w": ev_conv_w[i],
                "conv_b": ev_conv_b[i].reshape(1, conv_ch),
                "dt_bias": _pad_lanes(ev_dt_bias[i].reshape(1, n_heads)),
                "a_log": _pad_lanes(ev_a_log[i].reshape(1, n_heads)),
                "d_skip": jnp.repeat(ev_d_skip[i], SSD_HEADDIM).reshape(1, inner),
                "norm_w": ev_norm_w[i].reshape(1, inner),
                "expand": expand,
                "sc_w": ev_sc_w[i],
            }
            xp, c1, s1, q1 = _even_layer(xp, w, i, jnp.zeros((bp, SSD_CONV - 1, conv_ch), F32),
                                         jnp.zeros((bp, n_heads, SSD_HEADDIM, SSD_STATE), F32),
                                         jnp.zeros((bp, SC_WIDTH - 1, d), F32), batch=bp, seq=sp)
            xs, c2, s2, q2 = _even_layer(xs, w, i, state_conv[i], state_ssm[i], state_sc[i], batch=bs, seq=ss)
            pconv.append(c1); pssm.append(s1); psc.append(q1)
            sconv.append(c2); sssm.append(s2); ssc.append(q2)
        else:
            w = {"norm_mix": norm_mix_w[l], "w_in": od_w_in_b, "w_out": od_w_out_b}
            xp, k1, v1 = _odd_prompt(xp, w, slopes, layer=i, batch=bp, seq=sp)
            xs, k2, v2 = _odd_sample(xs, w, k_pools, v_pools, page_table, slope_rows,
                                     layer=i, batch=bs, seq=ss, past_len=past_len)
            pk.append(k1); pv.append(v1); sk.append(k2); sv.append(v2)
        wc = {"norm_cross": norm_cross_w[l], "ca_w_q": ca_w_q_b, "ca_w_out": ca_w_out_b}
        mkv = norm_matmul(mem, norm_mem_w[l], ca_w_kv_b, l)
        mk = mkv[:, :ca_w].reshape(bp, n_mem, ca_w)
        mv = mkv[:, ca_w:].reshape(bp, n_mem, ca_w)
        pmk.append(mk.reshape(bp, n_mem, CA_HEADS, CA_HEAD_DIM))
        pmv.append(mv.reshape(bp, n_mem, CA_HEADS, CA_HEAD_DIM))
        xp = _cross_layer(xp, wc, l, mk, mv, batch=bp, seq=sp)
        xs = _cross_layer(xs, wc, l, cache_mem_k[l].reshape(bs, n_mem, ca_w), cache_mem_v[l].reshape(bs, n_mem, ca_w),
                          batch=bs, seq=ss)
    y_prompt = rmsnorm(xp, final_norm_w).reshape(bp, sp, d)
    y_sample = rmsnorm(xs, final_norm_w).reshape(bs, ss, d)
    return (y_prompt, y_sample,
            jnp.stack(pk), jnp.stack(pv), jnp.stack(pconv), jnp.stack(pssm), jnp.stack(psc),
            jnp.stack(pmk), jnp.stack(pmv),
            jnp.stack(sk), jnp.stack(sv), jnp.stack(sconv), jnp.stack(sssm), jnp.stack(ssc))
```

```python
import functools
import math

import numpy as np
import jax
import jax.numpy as jnp
from jax import lax
from jax.experimental import pallas as pl
from jax.experimental.pallas import tpu as pltpu

F32 = jnp.float32
BF16 = jnp.bfloat16
HIGHEST = lax.Precision.HIGHEST

NORM_EPS = 1e-5
NEG_INF = -1e30

LANES = 128
SUBLANES = 8
VMEM_LIMIT = 48 * 1024 * 1024

SSD_HEADDIM = 64
SSD_GROUPS = 4
SSD_STATE = 128
SSD_CONV = 4
SSD_CHUNK = 128
SC_WIDTH = 3
ATT_HEADS = 16
KV_HEADS = 4
HEAD_DIM = 128
Q_PER_KV = ATT_HEADS // KV_HEADS
MOBA_BLOCK = 256
MOBA_TOPK = 3
PAGE_SIZE = 128
PAGES_PER_BLOCK = MOBA_BLOCK // PAGE_SIZE
PAGE_ROWS = PAGE_SIZE * KV_HEADS
BLOCKS_PER_STEP = 8
PAGES_PER_STEP = BLOCKS_PER_STEP * PAGES_PER_BLOCK
CA_HEADS = 4
CA_HEAD_DIM = 128


def _cparams(*sem):
    return pltpu.CompilerParams(dimension_semantics=sem, vmem_limit_bytes=VMEM_LIMIT)


def _tile(n, pref):
    if n <= pref:
        return n
    t = pref
    while n % t:
        t //= 2
    return t


def _silu(x):
    return x / (1.0 + jnp.exp(-x))


def _dot(a, b, **kw):
    return jnp.dot(a, b, preferred_element_type=F32, **kw)


def _bf16_pieces(x):
    hi = x.astype(BF16)
    rest = x - hi.astype(F32)
    mid = rest.astype(BF16)
    lo = (rest - mid.astype(F32)).astype(BF16)
    return hi, mid, lo


def _dot_exact_left(x, sel):
    return sum(_dot(p, sel) for p in _bf16_pieces(x))


def _dot_exact_right(sel, x):
    return sum(_dot(sel, p) for p in _bf16_pieces(x))


def _dot_nt(a, b, **kw):
    return lax.dot_general(a, b, (((1,), (1,)), ((), ())), preferred_element_type=F32, **kw)


def _norm_mm_kernel(x_ref, nw_ref, w_ref, o_ref, xn_ref):
    @pl.when(pl.program_id(1) == 0)
    def _():
        x = x_ref[...]
        ms = jnp.mean(x * x, axis=-1, keepdims=True)
        xn_ref[...] = ((x * lax.rsqrt(ms + NORM_EPS)) * nw_ref[...]).astype(BF16)

    o_ref[...] = _dot(xn_ref[...], w_ref[...]).astype(o_ref.dtype)


def _norm_mm_tap_kernel(x_ref, nw_ref, w_ref, o_ref, t0_ref, t1_ref, xn_ref, *, tap_tile):
    _norm_mm_kernel(x_ref, nw_ref, w_ref, o_ref, xn_ref)

    @pl.when(pl.program_id(1) == tap_tile)
    def _():
        heads, hd = t0_ref.shape[1], t0_ref.shape[2]
        for h in range(heads):
            t0_ref[:, h, :] = o_ref[:, h * hd:(h + 1) * hd]
            t1_ref[:, h, :] = o_ref[:, (heads + h) * hd:(heads + h + 1) * hd]


def _layer_spec(layer, block, index, **kw):
    return pl.BlockSpec((None,) + block, lambda *g: (layer,) + index(*g), **kw)


def norm_matmul(x, norm_w, w, layer, *, tm=1024, tn=1024, out_dtype=F32, tap_col0=None):
    m, k = x.shape
    n = w.shape[2]
    tm, tn = _tile(m, tm), _tile(n, tn)
    in_specs = [pl.BlockSpec((tm, k), lambda i, j: (i, 0)),
                pl.BlockSpec((1, k), lambda i, j: (0, 0)),
                _layer_spec(layer, (k, tn), lambda i, j: (0, j))]
    out_spec = pl.BlockSpec((tm, tn), lambda i, j: (i, j))
    out_shape = jax.ShapeDtypeStruct((m, n), out_dtype)
    if tap_col0 is None:
        body, out_specs, out_shapes = _norm_mm_kernel, out_spec, out_shape
    else:
        assert tap_col0 % tn == 0 and tn % (2 * LANES) == 0
        body = functools.partial(_norm_mm_tap_kernel, tap_tile=tap_col0 // tn)
        tap_heads = tn // 2 // HEAD_DIM
        tap_spec = pl.BlockSpec((tm, tap_heads, HEAD_DIM), lambda i, j: (i, 0, 0))
        tap_shape = jax.ShapeDtypeStruct((m, tap_heads, HEAD_DIM), out_dtype)
        out_specs, out_shapes = (out_spec, tap_spec, tap_spec), (out_shape, tap_shape, tap_shape)
    return pl.pallas_call(
        body,
        out_shape=out_shapes,
        grid=(m // tm, n // tn),
        in_specs=in_specs,
        out_specs=out_specs,
        scratch_shapes=[pltpu.VMEM((tm, k), BF16)],
        compiler_params=_cparams("parallel", "arbitrary"),
        name="norm_matmul",
    )(x, norm_w.reshape(1, k), w)


def _mm_res_kernel(*refs, n_pairs):
    res_ref, o_ref = refs[2 * n_pairs], refs[2 * n_pairs + 1]
    acc = res_ref[...]
    for p in range(n_pairs):
        acc = acc + _dot(refs[2 * p][...], refs[2 * p + 1][...])
    o_ref[...] = acc


def matmul_residual(pairs, res, layer, *, tm=512):
    m, n = res.shape
    tm = _tile(m, tm)
    in_specs, args = [], []
    for a, w, rb in pairs:
        k = a.shape[1]
        assert w.shape[1] % k == 0 and w.shape[2] == n
        in_specs += [pl.BlockSpec((tm, k), lambda i: (i, 0)),
                     _layer_spec(layer, (k, n), lambda i, rb=rb: (rb, 0), pipeline_mode=pl.Buffered(1))]
        args += [a, w]
    in_specs.append(pl.BlockSpec((tm, n), lambda i: (i, 0)))
    return pl.pallas_call(
        functools.partial(_mm_res_kernel, n_pairs=len(pairs)),
        out_shape=jax.ShapeDtypeStruct((m, n), F32),
        grid=(m // tm,),
        in_specs=in_specs,
        out_specs=pl.BlockSpec((tm, n), lambda i: (i, 0)),
        compiler_params=_cparams("parallel"),
        name="matmul_residual",
    )(*args, res)


def _mm_kernel(a_ref, w_ref, o_ref):
    o_ref[...] = _dot(a_ref[...], w_ref[...]).astype(o_ref.dtype)


def matmul(a, w, layer, *, col0=0, n=None, tm=1024, tn=1024, out_dtype=F32):
    m, k = a.shape
    n = w.shape[2] - col0 if n is None else n
    tm, tn = _tile(m, tm), _tile(n, tn)
    assert col0 % tn == 0 and col0 + n <= w.shape[2]
    cb0 = col0 // tn
    return pl.pallas_call(
        _mm_kernel,
        out_shape=jax.ShapeDtypeStruct((m, n), out_dtype),
        grid=(m // tm, n // tn),
        in_specs=[pl.BlockSpec((tm, k), lambda i, j: (i, 0)),
                  _layer_spec(layer, (k, tn), lambda i, j: (0, cb0 + j))],
        out_specs=pl.BlockSpec((tm, tn), lambda i, j: (i, j)),
        compiler_params=_cparams("parallel", "parallel"),
        name="matmul",
    )(a, w)


def _rmsnorm_kernel(x_ref, nw_ref, o_ref):
    x = x_ref[...]
    ms = jnp.mean(x * x, axis=-1, keepdims=True)
    o_ref[...] = ((x * lax.rsqrt(ms + NORM_EPS)) * nw_ref[...]).astype(o_ref.dtype)


def rmsnorm(x, norm_w, *, tm=512, out_dtype=F32):
    m, k = x.shape
    tm = _tile(m, tm)
    return pl.pallas_call(
        _rmsnorm_kernel,
        out_shape=jax.ShapeDtypeStruct((m, k), out_dtype),
        grid=(m // tm,),
        in_specs=[pl.BlockSpec((tm, k), lambda i: (i, 0)),
                  pl.BlockSpec((1, k), lambda i: (0, 0))],
        out_specs=pl.BlockSpec((tm, k), lambda i: (i, 0)),
        compiler_params=_cparams("parallel"),
        name="rmsnorm",
    )(x, norm_w.reshape(1, k))


def _rows_padded(ref, q):
    v = ref[...]
    if v.shape[0] < q:
        v = jnp.concatenate([v, jnp.zeros((q - v.shape[0], v.shape[1]), v.dtype)], axis=0)
    return v


def _causal_conv_chunk(x_ref, buf_ref, cw_ref, cb_ref, q):
    x = _rows_padded(x_ref, q)
    buf_ref[SUBLANES:SUBLANES + q, :] = x
    y = cw_ref[SSD_CONV - 1:SSD_CONV, :] * x
    for k in range(1, SSD_CONV):
        y = y + cw_ref[SSD_CONV - 1 - k:SSD_CONV - k, :] * buf_ref[SUBLANES - k:SUBLANES - k + q, :]
    if cb_ref is not None:
        y = y + cb_ref[...]
    buf_ref[0:SUBLANES, :] = x[q - SUBLANES:q, :]
    return y


def _ssd_kernel(z_ref, xs_ref, b_ref, c_ref, dt_ref, cs_xs_ref, cs_b_ref, cs_c_ref, h0_ref,
                cw_xs_ref, cw_b_ref, cw_c_ref, cb_xs_ref, cb_b_ref, cb_c_ref,
                dtb_ref, alog_ref, dskip_ref, nw_ref, e_ref,
                y_ref, hfin_ref,
                ht_ref, buf_xs, buf_b, buf_c, *, q, n_valid, n_chunks):
    c = pl.program_id(1)
    n_heads = e_ref.shape[1] // SSD_HEADDIM
    gw = (n_heads // SSD_GROUPS) * SSD_HEADDIM

    @pl.when(c == 0)
    def _():
        ht_ref[...] = h0_ref[...].T
        buf_xs[0:SUBLANES, :] = cs_xs_ref[...]
        buf_b[0:SUBLANES, :] = cs_b_ref[...]
        buf_c[0:SUBLANES, :] = cs_c_ref[...]

    xs = _silu(_causal_conv_chunk(xs_ref, buf_xs, cw_xs_ref, cb_xs_ref, q))
    bm = _silu(_causal_conv_chunk(b_ref, buf_b, cw_b_ref, cb_b_ref, q))
    cm = _silu(_causal_conv_chunk(c_ref, buf_c, cw_c_ref, cb_c_ref, q))

    row = lax.broadcasted_iota(jnp.int32, (q, q), 0)
    col = lax.broadcasted_iota(jnp.int32, (q, q), 1)
    tril = row >= col

    dtv = _rows_padded(dt_ref, q) + dtb_ref[...]
    dt = jnp.maximum(dtv, 0.0) + jnp.log(1.0 + jnp.exp(-jnp.abs(dtv)))
    if n_valid < q:
        dt = jnp.where(lax.broadcasted_iota(jnp.int32, dt.shape, 0) < n_valid, dt, 0.0)
    a = dt * (-jnp.exp(alog_ref[...]))
    a_cs = _dot_exact_right(jnp.where(tril, 1.0, 0.0).astype(BF16), a)
    a_cs_t = a_cs.T
    expand = e_ref[...]
    dt_full = _dot_exact_left(dt, expand)
    acs_full = _dot_exact_left(a_cs, expand)
    tot_full = acs_full[q - 1:q, :]
    xr = xs * dt_full
    xr_dec = (xr * jnp.exp(tot_full - acs_full)).astype(BF16)
    exp_acs = jnp.exp(acs_full)
    exp_tot = jnp.exp(tot_full)

    lane = lax.broadcasted_iota(jnp.int32, (q, LANES), 1)
    lo_half = lane < SSD_HEADDIM
    heads_per_tile = LANES // SSD_HEADDIM

    y = dskip_ref[...] * xs
    y_parts = []
    for g in range(SSD_GROUPS):
        bg = bm[:, g * SSD_STATE:(g + 1) * SSD_STATE]
        cg = cm[:, g * SSD_STATE:(g + 1) * SSD_STATE].astype(BF16)
        bg_t = bg.T.astype(BF16)
        cb = _dot(cg, bg_t)
        ht_g = ht_ref[:, g * gw:(g + 1) * gw]
        y_g = _dot(cg, ht_g.astype(BF16)) * exp_acs[:, g * gw:(g + 1) * gw]
        tiles = []
        for t in range(gw // LANES):
            base = g * gw + t * LANES
            xr_t = xr[:, base:base + LANES]
            acc = None
            for e in range(heads_per_tile):
                h = base // SSD_HEADDIM + e
                seg = a_cs[:, h:h + 1] - a_cs_t[h:h + 1, :]
                lmat = jnp.where(tril, jnp.exp(seg), 0.0)
                in_head = lo_half if e == 0 else jnp.logical_not(lo_half)
                part = _dot((cb * lmat).astype(BF16), jnp.where(in_head, xr_t, 0.0).astype(BF16))
                acc = part if acc is None else acc + part
            tiles.append(acc)
        y_g = y_g + jnp.concatenate(tiles, axis=1)
        states = _dot(bg_t, xr_dec[:, g * gw:(g + 1) * gw])
        ht_ref[:, g * gw:(g + 1) * gw] = exp_tot[:, g * gw:(g + 1) * gw] * ht_g + states
        y_parts.append(y_g)
    y = y + jnp.concatenate(y_parts, axis=1)

    y = y * _silu(_rows_padded(z_ref, q))
    for g in range(SSD_GROUPS):
        yg = y[:, g * gw:(g + 1) * gw]
        ms = jnp.mean(yg * yg, axis=-1, keepdims=True)
        y_ref[:, g * gw:(g + 1) * gw] = ((yg * lax.rsqrt(ms + NORM_EPS)) * nw_ref[:, g * gw:(g + 1) * gw]).astype(y_ref.dtype)

    @pl.when(c == n_chunks - 1)
    def _():
        hfin_ref[...] = ht_ref[...].T


def ssd_mixer(proj, dt_raw, conv_state8, h0, wts, *, batch, seq):
    q = SSD_CHUNK
    n_valid = min(seq, q)
    assert seq % n_valid == 0 and n_valid % SUBLANES == 0
    nc = seq // n_valid
    assert n_valid == q or nc == 1
    inner = h0.shape[1]
    gn = SSD_GROUPS * SSD_STATE
    assert inner % gn == 0 and (inner // gn) * gn == inner
    kb = inner // gn
    rows = lambda b, c: b * nc + c
    full = lambda b, c: (0, 0)
    in_specs = [
        pl.BlockSpec((n_valid, inner), lambda b, c: (rows(b, c), 0)),
        pl.BlockSpec((n_valid, inner), lambda b, c: (rows(b, c), 1)),
        pl.BlockSpec((n_valid, gn), lambda b, c: (rows(b, c), 2 * kb)),
        pl.BlockSpec((n_valid, gn), lambda b, c: (rows(b, c), 2 * kb + 1)),
        pl.BlockSpec((n_valid, LANES), lambda b, c: (rows(b, c), 0)),
        pl.BlockSpec((None, SUBLANES, inner), lambda b, c: (b, 0, 0)),
        pl.BlockSpec((None, SUBLANES, gn), lambda b, c: (b, 0, kb)),
        pl.BlockSpec((None, SUBLANES, gn), lambda b, c: (b, 0, kb + 1)),
        pl.BlockSpec((None, inner, SSD_STATE), lambda b, c: (b, 0, 0)),
        pl.BlockSpec((SSD_CONV, inner), full),
        pl.BlockSpec((SSD_CONV, gn), lambda b, c: (0, kb)),
        pl.BlockSpec((SSD_CONV, gn), lambda b, c: (0, kb + 1)),
        pl.BlockSpec((1, inner), full),
        pl.BlockSpec((1, gn), lambda b, c: (0, kb)),
        pl.BlockSpec((1, gn), lambda b, c: (0, kb + 1)),
        pl.BlockSpec((1, LANES), full),
        pl.BlockSpec((1, LANES), full),
        pl.BlockSpec((1, inner), full),
        pl.BlockSpec((1, inner), full),
        pl.BlockSpec((LANES, inner), full),
    ]
    cw, cb = wts["conv_w"], wts["conv_b"]
    y, hfin = pl.pallas_call(
        functools.partial(_ssd_kernel, q=q, n_valid=n_valid, n_chunks=nc),
        out_shape=(jax.ShapeDtypeStruct((batch * nc * q, inner), BF16),
                   jax.ShapeDtypeStruct((batch, inner, SSD_STATE), F32)),
        grid=(batch, nc),
        in_specs=in_specs,
        out_specs=(pl.BlockSpec((q, inner), lambda b, c: (rows(b, c), 0)),
                   pl.BlockSpec((None, inner, SSD_STATE), lambda b, c: (b, 0, 0))),
        scratch_shapes=[pltpu.VMEM((SSD_STATE, inner), F32),
                        pltpu.VMEM((q + SUBLANES, inner), F32),
                        pltpu.VMEM((q + SUBLANES, gn), F32),
                        pltpu.VMEM((q + SUBLANES, gn), F32)],
        compiler_params=_cparams("parallel", "arbitrary"),
        name="ssd_mixer",
    )(proj, proj, proj, proj, dt_raw, conv_state8, conv_state8, conv_state8, h0,
      cw, cw, cw, cb, cb, cb, wts["dt_bias"], wts["a_log"], wts["d_skip"], wts["norm_w"], wts["expand"])
    return y, hfin


def _sc_mixer_kernel(xn_ref, wb_ref, wc_ref, wx_ref, wg_ref, st_ref, w_ref, y_ref, last_ref, buf_ref, *, tq, slabs):
    if slabs == 1:
        @pl.when(pl.program_id(2) == 0)
        def _():
            buf_ref[0:SUBLANES, :] = st_ref[0]

    xn = xn_ref[...]
    prod = _dot(xn, wc_ref[...]) * _dot(xn, wx_ref[...])
    us = []
    for s in range(slabs):
        if slabs > 1:
            buf_ref[0:SUBLANES, :] = st_ref[s]
        ps = prod[s * tq:(s + 1) * tq, :]
        buf_ref[SUBLANES:SUBLANES + tq, :] = ps
        u = w_ref[SC_WIDTH - 1:SC_WIDTH, :] * ps
        for k in range(1, SC_WIDTH):
            u = u + w_ref[SC_WIDTH - 1 - k:SC_WIDTH - k, :] * buf_ref[SUBLANES - k:SUBLANES - k + tq, :]
        tail = ps[tq - SUBLANES:tq, :]
        buf_ref[0:SUBLANES, :] = tail
        last_ref[s] = tail
        us.append(u)
    u = us[0] if slabs == 1 else jnp.concatenate(us, axis=0)
    y_ref[...] = (_dot(xn, wb_ref[...]) * u * _silu(_dot(xn, wg_ref[...]))).astype(y_ref.dtype)


def sc_mixer(xn, w_sc, layer, state8, w, *, batch, seq, tq=1024, tc=512):
    k = xn.shape[1]
    dim = w.shape[1]
    slabs = batch if batch * seq <= tq else 1
    tq = seq if slabs > 1 else _tile(seq, tq)
    nt = seq // tq
    nb = batch // slabs
    ncb = dim // tc
    assert dim % tc == 0 and tq % SUBLANES == 0 and w_sc.shape[2] == 4 * dim

    def part(p):
        return _layer_spec(layer, (k, tc), lambda j, b, t: (0, p * ncb + j))

    return pl.pallas_call(
        functools.partial(_sc_mixer_kernel, tq=tq, slabs=slabs),
        out_shape=(jax.ShapeDtypeStruct((batch * seq, dim), BF16),
                   jax.ShapeDtypeStruct((batch, SUBLANES, dim), F32)),
        grid=(ncb, nb, nt),
        in_specs=[pl.BlockSpec((slabs * tq, k), lambda j, b, t: (b * nt + t, 0)),
                  part(0), part(1), part(2), part(3),
                  pl.BlockSpec((slabs, SUBLANES, tc), lambda j, b, t: (b, 0, j)),
                  pl.BlockSpec((SC_WIDTH, tc), lambda j, b, t: (0, j))],
        out_specs=(pl.BlockSpec((slabs * tq, tc), lambda j, b, t: (b * nt + t, j)),
                   pl.BlockSpec((slabs, SUBLANES, tc), lambda j, b, t: (b, 0, j))),
        scratch_shapes=[pltpu.VMEM((tq + SUBLANES, tc), F32)],
        compiler_params=_cparams("parallel", "parallel", "arbitrary"),
        name="sc_mixer",
    )(xn, w_sc, w_sc, w_sc, w_sc, state8, w)


def _top_blocks(gate, n_valid, axis=1):
    idx = lax.broadcasted_iota(jnp.int32, gate.shape, axis).astype(F32)
    g = jnp.where(idx < jnp.asarray(n_valid, F32), gate, NEG_INF)
    sel = jnp.zeros(gate.shape, F32)
    for _ in range(MOBA_TOPK):
        m = jnp.max(g, axis=axis, keepdims=True)
        first = jnp.min(jnp.where(g == m, idx, float(gate.shape[axis])), axis=axis, keepdims=True)
        pick = idx == first
        sel = jnp.where(pick & (m > 0.5 * NEG_INF), 1.0, sel)
        g = jnp.where(pick, -jnp.inf, g)
    return sel


def _moba_prompt_kernel(slopes_ref, q_ref, k_ref, v_ref, g_ref, o_ref, kmean_ref, kb_ref, vt_ref, mval_ref, *, n_blocks):
    kv = pl.program_id(1)
    i = pl.program_id(2)
    blk = MOBA_BLOCK
    heads = range(Q_PER_KV)

    @pl.when(i == 0)
    def _():
        kmean_ref[...] = jnp.zeros(kmean_ref.shape, F32)
        for j in range(n_blocks):
            kj = k_ref[j * blk:(j + 1) * blk, :]
            kmean_ref[j:j + 1, :] = jnp.sum(kj, axis=0, keepdims=True) * (1.0 / blk)
            kb_ref[j] = kj.astype(BF16)
            vt_ref[j] = v_ref[j * blk:(j + 1) * blk, :].T.astype(BF16)

    rel = (lax.broadcasted_iota(jnp.int32, (blk, blk), 1)
           - lax.broadcasted_iota(jnp.int32, (blk, blk), 0)).astype(F32)
    blk_idx = lax.broadcasted_iota(jnp.int32, (kmean_ref.shape[0], blk), 0)
    back = (i - blk_idx).astype(F32) * float(blk)
    kmean = kmean_ref[...]

    qt, bias, diag = [], [], []
    for g in heads:
        slope = slopes_ref[kv * Q_PER_KV + g]
        q_t = (q_ref[:, g * HEAD_DIM:(g + 1) * HEAD_DIM] * (HEAD_DIM ** -0.5)).T
        gate = _dot(kmean, q_t, precision=HIGHEST)
        sel = _top_blocks(gate, i, axis=0)
        mval_ref[:, g * blk:(g + 1) * blk] = jnp.where(sel > 0.0, -slope * back, NEG_INF)
        bias.append(-slope * rel)
        diag.append(jnp.where(rel >= 0, -slope * rel, NEG_INF))
        qt.append(q_t.astype(BF16))
    qt = jnp.concatenate(qt, axis=1)
    bias = jnp.concatenate(bias, axis=1)

    s = _dot(kb_ref[i], qt) + jnp.concatenate(diag, axis=1)
    m0 = jnp.max(s, axis=0, keepdims=True)
    p = jnp.exp(s - m0)
    l0 = jnp.sum(p, axis=0, keepdims=True)
    acc0 = _dot(vt_ref[i], p.astype(BF16))

    def past_pair(t, carry):
        m, l, acc = carry
        js = (2 * t, 2 * t + 1)
        ss = [_dot(kb_ref[j], qt) + bias for j in js]
        masks = [mval_ref[pl.ds(j, 1), :] for j in js]
        m_new = m
        for s, mask in zip(ss, masks):
            m_new = jnp.maximum(m_new, jnp.max(s, axis=0, keepdims=True) + mask)
        l = jnp.exp(m - m_new) * l
        acc = jnp.exp(m - m_new) * acc
        for j, s, mask in zip(js, ss, masks):
            p = jnp.exp(s - (m_new - mask))
            l = l + jnp.sum(p, axis=0, keepdims=True)
            acc = acc + _dot(vt_ref[j], p.astype(BF16))
        return m_new, l, acc

    _, l, acc = lax.fori_loop(0, (i + 1) // 2, past_pair, (m0, l0, acc0))
    o_t = acc / l
    for g in heads:
        c = slice(g * HEAD_DIM, (g + 1) * HEAD_DIM)
        o_ref[:, c] = (o_t[:, g * blk:(g + 1) * blk].T * _silu(g_ref[:, c])).astype(o_ref.dtype)


def moba_prompt(proj, slopes, *, batch, seq):
    assert seq % MOBA_BLOCK == 0
    nb = seq // MOBA_BLOCK
    nbp = -(-nb // SUBLANES) * SUBLANES
    gw = Q_PER_KV * HEAD_DIM
    kcol = ATT_HEADS
    vcol = ATT_HEADS + KV_HEADS
    gcol = (ATT_HEADS + 2 * KV_HEADS) * HEAD_DIM // gw
    assert gcol * gw == (ATT_HEADS + 2 * KV_HEADS) * HEAD_DIM
    return pl.pallas_call(
        functools.partial(_moba_prompt_kernel, n_blocks=nb),
        out_shape=jax.ShapeDtypeStruct((batch * seq, ATT_HEADS * HEAD_DIM), BF16),
        grid=(batch, KV_HEADS, nb),
        in_specs=[pl.BlockSpec(memory_space=pltpu.SMEM),
                  pl.BlockSpec((MOBA_BLOCK, gw), lambda b, kv, i: (b * nb + i, kv)),
                  pl.BlockSpec((seq, HEAD_DIM), lambda b, kv, i: (b, kcol + kv)),
                  pl.BlockSpec((seq, HEAD_DIM), lambda b, kv, i: (b, vcol + kv)),
                  pl.BlockSpec((MOBA_BLOCK, gw), lambda b, kv, i: (b * nb + i, gcol + kv))],
        out_specs=pl.BlockSpec((MOBA_BLOCK, gw), lambda b, kv, i: (b * nb + i, kv)),
        scratch_shapes=[pltpu.VMEM((nbp, HEAD_DIM), F32),
                        pltpu.VMEM((nb, MOBA_BLOCK, HEAD_DIM), BF16),
                        pltpu.VMEM((nb, HEAD_DIM, MOBA_BLOCK), BF16),
                        pltpu.VMEM((nbp, Q_PER_KV * MOBA_BLOCK), F32)],
        compiler_params=_cparams("parallel", "parallel", "arbitrary"),
        name="moba_prompt",
    )(slopes, proj, proj, proj, proj)


def _page_specs(layer):
    return [pl.BlockSpec((None, None, PAGE_ROWS, HEAD_DIM),
                         lambda b, s, pt, p=p: (layer, pt[b, s * PAGES_PER_STEP + p], 0, 0))
            for p in range(PAGES_PER_STEP)]


def _kmean_kernel(pt_ref, *refs):
    del pt_ref
    pages, o_ref = refs[:PAGES_PER_STEP], refs[PAGES_PER_STEP]
    for bb in range(BLOCKS_PER_STEP):
        tot = None
        for pg in range(PAGES_PER_BLOCK):
            page = pages[bb * PAGES_PER_BLOCK + pg][...]
            part = jnp.sum(page.reshape(PAGE_ROWS // SUBLANES, SUBLANES, HEAD_DIM), axis=0)
            tot = part if tot is None else tot + part
        per_kv = tot[0:KV_HEADS]
        for par in range(1, SUBLANES // KV_HEADS):
            per_kv = per_kv + tot[par * KV_HEADS:(par + 1) * KV_HEADS]
        o_ref[bb] = per_kv * (1.0 / MOBA_BLOCK)


def paged_block_means(k_pools, page_table, *, layer, n_blocks):
    db = page_table.shape[0]
    assert n_blocks % BLOCKS_PER_STEP == 0 and SUBLANES % KV_HEADS == 0
    return pl.pallas_call(
        _kmean_kernel,
        out_shape=jax.ShapeDtypeStruct((db, n_blocks, KV_HEADS, HEAD_DIM), F32),
        grid_spec=pltpu.PrefetchScalarGridSpec(
            num_scalar_prefetch=1,
            grid=(db, n_blocks // BLOCKS_PER_STEP),
            in_specs=_page_specs(layer),
            out_specs=pl.BlockSpec((None, BLOCKS_PER_STEP, KV_HEADS, HEAD_DIM), lambda b, s, pt: (b, s, 0, 0))),
        compiler_params=_cparams("parallel", "arbitrary"),
        name="paged_block_means",
    )(page_table, *([k_pools] * PAGES_PER_STEP))


def _moba_sample_kernel(pt_ref, q_ref, kmean_ref, slope_ref, knew_ref, vnew_ref, *refs,
                        n_blocks, past_len, dec_seq):
    del pt_ref
    k_pages, v_pages = refs[:PAGES_PER_STEP], refs[PAGES_PER_STEP:2 * PAGES_PER_STEP]
    o_ref, m_ref, l_ref, acc_ref, sel_ref, base_ref = refs[2 * PAGES_PER_STEP:]
    step = pl.program_id(1)
    rows = q_ref.shape[0]
    rpk = rows // KV_HEADS
    q = q_ref[...] * (HEAD_DIM ** -0.5)
    qb = q.astype(BF16)
    slope = slope_ref[:, 0:1]
    lane = lax.broadcasted_iota(jnp.int32, (rows, LANES), 1)
    t_row = jnp.bitwise_and(lax.broadcasted_iota(jnp.int32, (rows, LANES), 0), dec_seq - 1)

    def per_head(fn):
        return jnp.concatenate([fn(h, slice(h * rpk, (h + 1) * rpk)) for h in range(KV_HEADS)], axis=0)

    def page_head(page_ref, h):
        return page_ref[pl.ds(h, PAGE_SIZE, stride=KV_HEADS), :].astype(BF16)

    @pl.when(step == 0)
    def _():
        gate = per_head(lambda h, r: _dot_nt(q[r, :], kmean_ref[h], precision=HIGHEST))
        sel_ref[...] = _top_blocks(gate, n_blocks)
        base_ref[...] = -slope * (past_len + t_row - lane).astype(F32)
        dist = (t_row - lane).astype(F32)
        s = (per_head(lambda h, r: _dot_nt(qb[r, :], knew_ref[h].astype(BF16)))
             + jnp.where(dist >= 0, -slope * dist, NEG_INF))
        m = jnp.max(s, axis=1, keepdims=True)
        p = jnp.exp(s - m)
        pb = p.astype(BF16)
        m_ref[...] = jnp.broadcast_to(m, m_ref.shape)
        l_ref[...] = jnp.broadcast_to(jnp.sum(p, axis=1, keepdims=True), l_ref.shape)
        acc_ref[...] = per_head(lambda h, r: _dot(pb[r, :], vnew_ref[h].astype(BF16)))

    sel = sel_ref[...]
    base = base_ref[...]
    m, l, acc = m_ref[:, 0:1], l_ref[:, 0:1], acc_ref[...]
    scores, offsets = [], []
    m_new = m
    for bb in range(BLOCKS_PER_STEP):
        blk_id = step * BLOCKS_PER_STEP + bb
        chosen = jnp.sum(jnp.where(lane == blk_id, sel, 0.0), axis=1, keepdims=True) > 0.0
        for pg in range(PAGES_PER_BLOCK):
            page_pos = (blk_id * MOBA_BLOCK + pg * PAGE_SIZE).astype(F32)
            mcol = jnp.where(chosen, slope * page_pos, NEG_INF)
            k_page = k_pages[bb * PAGES_PER_BLOCK + pg]
            s = per_head(lambda h, r: _dot_nt(qb[r, :], page_head(k_page, h))) + base
            m_new = jnp.maximum(m_new, jnp.max(s, axis=1, keepdims=True) + mcol)
            scores.append(s)
            offsets.append(mcol)
    alpha = jnp.exp(m - m_new)
    l, acc, m = alpha * l, alpha * acc, m_new
    for pg, (s, mcol) in enumerate(zip(scores, offsets)):
        p = jnp.exp(s - (m - mcol))
        pb = p.astype(BF16)
        l = l + jnp.sum(p, axis=1, keepdims=True)
        acc = acc + per_head(lambda h, r: _dot(pb[r, :], page_head(v_pages[pg], h)))
    m_ref[...] = jnp.broadcast_to(m, m_ref.shape)
    l_ref[...] = jnp.broadcast_to(l, l_ref.shape)
    acc_ref[...] = acc

    @pl.when(step == n_blocks // BLOCKS_PER_STEP - 1)
    def _():
        o_ref[...] = acc / l


def moba_sample(q_rows, kmean_t, slope_rows, k_new_pad, v_new_pad, k_pools, v_pools, page_table,
                *, layer, n_blocks, past_len, dec_seq):
    db, rows, _ = q_rows.shape
    rpk = rows // KV_HEADS
    for n in (KV_HEADS, dec_seq, rpk):
        assert n & (n - 1) == 0
    assert n_blocks % BLOCKS_PER_STEP == 0 and dec_seq * KV_HEADS <= LANES
    per_b = lambda *shape: pl.BlockSpec((None,) + shape, lambda b, s, pt: (b,) + (0,) * len(shape))
    return pl.pallas_call(
        functools.partial(_moba_sample_kernel, n_blocks=n_blocks, past_len=past_len, dec_seq=dec_seq),
        out_shape=jax.ShapeDtypeStruct((db, rows, HEAD_DIM), F32),
        grid_spec=pltpu.PrefetchScalarGridSpec(
            num_scalar_prefetch=1,
            grid=(db, n_blocks // BLOCKS_PER_STEP),
            in_specs=[per_b(rows, HEAD_DIM),
                      per_b(KV_HEADS, LANES, HEAD_DIM),
                      pl.BlockSpec((rows, LANES), lambda b, s, pt: (0, 0)),
                      per_b(KV_HEADS, LANES, HEAD_DIM),
                      per_b(KV_HEADS, LANES, HEAD_DIM)] + _page_specs(layer) + _page_specs(layer),
            out_specs=per_b(rows, HEAD_DIM),
            scratch_shapes=[pltpu.VMEM((rows, LANES), F32), pltpu.VMEM((rows, LANES), F32),
                            pltpu.VMEM((rows, HEAD_DIM), F32), pltpu.VMEM((rows, LANES), F32),
                            pltpu.VMEM((rows, PAGE_SIZE), F32)]),
        compiler_params=_cparams("parallel", "arbitrary"),
        name="moba_sample",
    )(page_table, q_rows, kmean_t, slope_rows, k_new_pad, v_new_pad,
      *([k_pools] * PAGES_PER_STEP), *([v_pools] * PAGES_PER_STEP))


def _gate_kernel(o_ref, g_ref, y_ref):
    y_ref[...] = (o_ref[...] * _silu(g_ref[...])).astype(y_ref.dtype)


def silu_gate(o, proj, *, col0, tn=1024):
    m, n = o.shape
    assert col0 % tn == 0 and n % tn == 0
    cb = col0 // tn
    return pl.pallas_call(
        _gate_kernel,
        out_shape=jax.ShapeDtypeStruct((m, n), BF16),
        grid=(n // tn,),
        in_specs=[pl.BlockSpec((m, tn), lambda j: (0, j)),
                  pl.BlockSpec((m, tn), lambda j: (0, cb + j))],
        out_specs=pl.BlockSpec((m, tn), lambda j: (0, j)),
        compiler_params=_cparams("parallel"),
        name="silu_gate",
    )(o, proj)


def _cross_layer_kernel(x_ref, nw_ref, wq_ref, k_ref, v_ref, wo_ref, o_ref, *, slabs):
    width = CA_HEADS * CA_HEAD_DIM
    x = x_ref[...]
    rows = x.shape[0] // slabs
    ms = jnp.mean(x * x, axis=-1, keepdims=True)
    xn = ((x * lax.rsqrt(ms + NORM_EPS)) * nw_ref[...]).astype(BF16)
    qg = _dot(xn, wq_ref[...])
    gated_rows = []
    for s_idx in range(slabs):
        r = slice(s_idx * rows, (s_idx + 1) * rows)
        gated = []
        for h in range(CA_HEADS):
            c = slice(h * CA_HEAD_DIM, (h + 1) * CA_HEAD_DIM)
            q = (qg[r, c] * (CA_HEAD_DIM ** -0.5)).astype(BF16)
            s = _dot_nt(q, k_ref[s_idx, :, c].astype(BF16))
            m = jnp.max(s, axis=1, keepdims=True)
            p = jnp.exp(s - m)
            l = jnp.sum(p, axis=1, keepdims=True)
            o = _dot(p.astype(BF16), v_ref[s_idx, :, c].astype(BF16)) / l
            g = qg[r, width + h * CA_HEAD_DIM:width + (h + 1) * CA_HEAD_DIM]
            gated.append(o * _silu(g))
        gated_rows.append(jnp.concatenate(gated, axis=1))
    og = gated_rows[0] if slabs == 1 else jnp.concatenate(gated_rows, axis=0)
    o_ref[...] = x + _dot(og.astype(BF16), wo_ref[...])


def cross_layer(x, norm_w, w_q, mem_k, mem_v, w_out, layer, *, batch, seq, tm=512):
    width = CA_HEADS * CA_HEAD_DIM
    d = x.shape[1]
    n_mem = mem_k.shape[1]
    slabs = batch if batch * seq <= tm else 1
    tm = batch * seq if slabs > 1 else _tile(seq, tm)
    nt = batch * seq // tm // (batch // slabs)
    const = lambda i: (0, 0)
    return pl.pallas_call(
        functools.partial(_cross_layer_kernel, slabs=slabs),
        out_shape=jax.ShapeDtypeStruct((batch * seq, d), F32),
        grid=(batch * seq // tm,),
        in_specs=[pl.BlockSpec((tm, d), lambda i: (i, 0)),
                  pl.BlockSpec((1, d), const),
                  _layer_spec(layer, (d, 2 * width), const),
                  pl.BlockSpec((slabs, n_mem, width), lambda i: (i // nt, 0, 0)),
                  pl.BlockSpec((slabs, n_mem, width), lambda i: (i // nt, 0, 0)),
                  _layer_spec(layer, (width, d), const)],
        out_specs=pl.BlockSpec((tm, d), lambda i: (i, 0)),
        compiler_params=_cparams("parallel"),
        name="cross_layer",
    )(x, norm_w.reshape(1, d), w_q, mem_k, mem_v, w_out)


def _alibi_slopes():
    return np.array([2.0 ** (-8.0 * (h + 1) / ATT_HEADS) for h in range(ATT_HEADS)], dtype=np.float32)


def _pad_rows(a, n, front=False):
    extra = n - a.shape[-2]
    pad = [(0, 0)] * a.ndim
    pad[-2] = (extra, 0) if front else (0, extra)
    return jnp.pad(a, pad)


def _pad_lanes(a):
    pad = [(0, 0)] * a.ndim
    pad[-1] = (0, LANES - a.shape[-1])
    return jnp.pad(a, pad)


def _even_layer(x, w, layer, conv_state, ssm_state, sc_state, *, batch, seq):
    d = x.shape[1]
    inner = d
    conv_ch = inner + 2 * SSD_GROUPS * SSD_STATE
    xn = rmsnorm(x, w["norm_mix"], out_dtype=BF16)
    proj = matmul(xn, w["w_in"], layer, n=inner + conv_ch)
    dt_raw = matmul(xn, w["w_in"], layer, col0=inner + conv_ch, n=LANES)
    new_conv = proj.reshape(batch, seq, -1)[:, seq - (SSD_CONV - 1):, inner:inner + conv_ch]

    y, h_fin = ssd_mixer(proj, dt_raw, _pad_rows(conv_state, SUBLANES, front=True),
                         ssm_state.reshape(batch, inner, SSD_STATE), w, batch=batch, seq=seq)
    if y.shape[0] != batch * seq:
        y = y.reshape(batch, -1, inner)[:, :seq].reshape(batch * seq, inner)

    y_sc, sc_tail = sc_mixer(xn, w["w_sc"], layer, _pad_rows(sc_state, SUBLANES, front=True), w["sc_w"],
                             batch=batch, seq=seq)
    x = matmul_residual([(y, w["w_out"], 0), (y_sc, w["w_out"], 1)], x, layer)
    return (x, new_conv, h_fin.reshape(batch, inner // SSD_HEADDIM, SSD_HEADDIM, SSD_STATE),
            sc_tail[:, SUBLANES - (SC_WIDTH - 1):, :])


def _odd_prompt(x, w, slopes, *, layer, batch, seq):
    att_q = ATT_HEADS * HEAD_DIM
    proj, k, v = norm_matmul(x, w["norm_mix"], w["w_in"], layer, tap_col0=att_q)
    og = moba_prompt(proj, slopes, batch=batch, seq=seq)
    return (matmul_residual([(og, w["w_out"], 0)], x, layer),
            k.reshape(batch, seq, KV_HEADS, HEAD_DIM), v.reshape(batch, seq, KV_HEADS, HEAD_DIM))


def _odd_sample(x, w, k_pools, v_pools, page_table, slope_rows, *, layer, batch, seq, past_len):
    assert past_len % MOBA_BLOCK == 0 and seq <= MOBA_BLOCK and seq <= LANES
    n_blocks = past_len // MOBA_BLOCK
    assert MOBA_TOPK <= n_blocks <= LANES
    att_q, att_kv = ATT_HEADS * HEAD_DIM, KV_HEADS * HEAD_DIM
    proj, k_new, v_new = norm_matmul(x, w["norm_mix"], w["w_in"], layer, tap_col0=att_q)
    q_rows = proj[:, :att_q].reshape(batch, seq, KV_HEADS, Q_PER_KV, HEAD_DIM).transpose(0, 2, 3, 1, 4)
    q_rows = q_rows.reshape(batch, ATT_HEADS * seq, HEAD_DIM)
    kmean = paged_block_means(k_pools, page_table, layer=layer, n_blocks=n_blocks)
    new_keys = lambda a: _pad_rows(a.reshape(batch, seq, KV_HEADS, HEAD_DIM).transpose(0, 2, 1, 3), LANES)
    o = moba_sample(q_rows, _pad_rows(kmean.transpose(0, 2, 1, 3), LANES), slope_rows,
                    new_keys(k_new), new_keys(v_new),
                    k_pools, v_pools, page_table, layer=layer, n_blocks=n_blocks, past_len=past_len, dec_seq=seq)
    o = o.reshape(batch, KV_HEADS, Q_PER_KV, seq, HEAD_DIM).transpose(0, 3, 1, 2, 4).reshape(batch * seq, att_q)
    og = silu_gate(o, proj, col0=att_q + 2 * att_kv)
    return (matmul_residual([(og, w["w_out"], 0)], x, layer),
            k_new.reshape(batch, seq, KV_HEADS, HEAD_DIM), v_new.reshape(batch, seq, KV_HEADS, HEAD_DIM))


def _cross_layer(x, w, layer, mem_k, mem_v, *, batch, seq):
    return cross_layer(x, w["norm_cross"], w["ca_w_q"], mem_k, mem_v, w["ca_w_out"], layer, batch=batch, seq=seq)


def kernel(x_prompt, x_sample, mem_prompt, cache_k, cache_v, page_table, state_conv, state_ssm, state_sc,
           cache_mem_k, cache_mem_v, norm_mix_w, norm_cross_w, norm_mem_w, final_norm_w,
           ev_w_in, ev_conv_w, ev_conv_b, ev_dt_bias, ev_a_log, ev_d_skip, ev_norm_w, ev_sc_w, ev_w_out,
           od_w_in, od_w_out, ca_w_q, ca_w_kv, ca_w_out):
    bp, sp, d = x_prompt.shape
    bs, ss, _ = x_sample.shape
    depth = norm_mix_w.shape[0]
    n_mem = mem_prompt.shape[1]
    n_heads = ev_dt_bias.shape[1]
    inner = n_heads * SSD_HEADDIM
    conv_ch = inner + 2 * SSD_GROUPS * SSD_STATE
    past_len = page_table.shape[1] * PAGE_SIZE
    ca_w = CA_HEADS * CA_HEAD_DIM
    assert inner == d and n_heads <= LANES

    slopes = jnp.asarray(_alibi_slopes())
    slope_rows = jnp.asarray(np.repeat(_alibi_slopes(), ss)[:, None] * np.ones((1, LANES), np.float32))
    expand_np = np.zeros((LANES, inner), np.float32)
    for hd in range(n_heads):
        expand_np[hd, hd * SSD_HEADDIM:(hd + 1) * SSD_HEADDIM] = 1.0
    expand = jnp.asarray(expand_np, dtype=BF16)

    xp = x_prompt.reshape(bp * sp, d)
    xs = x_sample.reshape(bs * ss, d)
    mem = mem_prompt.reshape(bp * n_mem, d)
    k_pools = cache_k.reshape(cache_k.shape[0], cache_k.shape[1], PAGE_ROWS, HEAD_DIM)
    v_pools = cache_v.reshape(cache_v.shape[0], cache_v.shape[1], PAGE_ROWS, HEAD_DIM)

    dt0 = inner + conv_ch
    ev_w_in_b = ev_w_in.astype(BF16)
    ev_w_sc_b = ev_w_in_b[:, :, dt0 + n_heads:]
    ev_w_out_b = ev_w_out.astype(BF16)
    od_w_in_b, od_w_out_b = od_w_in.astype(BF16), od_w_out.astype(BF16)
    ca_w_q_b, ca_w_kv_b, ca_w_out_b = ca_w_q.astype(BF16), ca_w_kv.astype(BF16), ca_w_out.astype(BF16)

    pk, pv, sk, sv = [], [], [], []
    pconv, pssm, psc, sconv, sssm, ssc = [], [], [], [], [], []
    pmk, pmv = [], []
    for l in range(depth):
        i = l // 2
        if l % 2 == 0:
            w = {
                "norm_mix": norm_mix_w[l],
                "w_in": ev_w_in_b,
                "w_sc": ev_w_sc_b,
                "w_out": ev_w_out_b,
                "conv_w": ev_conv_w[i],
                "conv_b": ev_conv_b[i].reshape(1, conv_ch),
                "dt_bias": _pad_lanes(ev_dt_bias[i].reshape(1, n_heads)),
                "a_log": _pad_lanes(ev_a_log[i].reshape(1, n_heads)),
                "d_skip": jnp.repeat(ev_d_skip[i], SSD_HEADDIM).reshape(1, inner),
                "norm_w": ev_norm_w[i].reshape(1, inner),
                "expand": expand,
                "sc_w": ev_sc_w[i],
            }
            xp, c1, s1, q1 = _even_layer(xp, w, i, jnp.zeros((bp, SSD_CONV - 1, conv_ch), F32),
                                         jnp.zeros((bp, n_heads, SSD_HEADDIM, SSD_STATE), F32),
                                         jnp.zeros((bp, SC_WIDTH - 1, d), F32), batch=bp, seq=sp)
            xs, c2, s2, q2 = _even_layer(xs, w, i, state_conv[i], state_ssm[i], state_sc[i], batch=bs, seq=ss)
            pconv.append(c1); pssm.append(s1); psc.append(q1)
            sconv.append(c2); sssm.append(s2); ssc.append(q2)
        else:
            w = {"norm_mix": norm_mix_w[l], "w_in": od_w_in_b, "w_out": od_w_out_b}
            xp, k1, v1 = _odd_prompt(xp, w, slopes, layer=i, batch=bp, seq=sp)
            xs, k2, v2 = _odd_sample(xs, w, k_pools, v_pools, page_table, slope_rows,
                                     layer=i, batch=bs, seq=ss, past_len=past_len)
            pk.append(k1); pv.append(v1); sk.append(k2); sv.append(v2)
        wc = {"norm_cross": norm_cross_w[l], "ca_w_q": ca_w_q_b, "ca_w_out": ca_w_out_b}
        mkv = norm_matmul(mem, norm_mem_w[l], ca_w_kv_b, l)
        mk = mkv[:, :ca_w].reshape(bp, n_mem, ca_w)
        mv = mkv[:, ca_w:].reshape(bp, n_mem, ca_w)
        pmk.append(mk.reshape(bp, n_mem, CA_HEADS, CA_HEAD_DIM))
        pmv.append(mv.reshape(bp, n_mem, CA_HEADS, CA_HEAD_DIM))
        xp = _cross_layer(xp, wc, l, mk, mv, batch=bp, seq=sp)
        xs = _cross_layer(xs, wc, l, cache_mem_k[l].reshape(bs, n_mem, ca_w), cache_mem_v[l].reshape(bs, n_mem, ca_w),
                          batch=bs, seq=ss)
    y_prompt = rmsnorm(xp, final_norm_w).reshape(bp, sp, d)
    y_sample = rmsnorm(xs, final_norm_w).reshape(bs, ss, d)
    return (y_prompt, y_sample,
            jnp.stack(pk), jnp.stack(pv), jnp.stack(pconv), jnp.stack(pssm), jnp.stack(psc),
            jnp.stack(pmk), jnp.stack(pmv),
            jnp.stack(sk), jnp.stack(sv), jnp.stack(sconv), jnp.stack(sssm), jnp.stack(ssc))
```

```python
import functools
import math

import numpy as np
import jax
import jax.numpy as jnp
from jax import lax
from jax.experimental import pallas as pl
from jax.experimental.pallas import tpu as pltpu

F32 = jnp.float32
BF16 = jnp.bfloat16
HIGHEST = lax.Precision.HIGHEST

NORM_EPS = 1e-5
NEG_INF = -1e30

LANES = 128
SUBLANES = 8
VMEM_LIMIT = 48 * 1024 * 1024

SSD_HEADDIM = 64
SSD_GROUPS = 4
SSD_STATE = 128
SSD_CONV = 4
SSD_CHUNK = 128
SC_WIDTH = 3
ATT_HEADS = 16
KV_HEADS = 4
HEAD_DIM = 128
Q_PER_KV = ATT_HEADS // KV_HEADS
MOBA_BLOCK = 256
MOBA_TOPK = 3
PAGE_SIZE = 128
PAGES_PER_BLOCK = MOBA_BLOCK // PAGE_SIZE
PAGE_ROWS = PAGE_SIZE * KV_HEADS
BLOCKS_PER_STEP = 8
PAGES_PER_STEP = BLOCKS_PER_STEP * PAGES_PER_BLOCK
CA_HEADS = 4
CA_HEAD_DIM = 128


def _cparams(*sem):
    return pltpu.CompilerParams(dimension_semantics=sem, vmem_limit_bytes=VMEM_LIMIT)


def _tile(n, pref):
    if n <= pref:
        return n
    t = pref
    while n % t:
        t //= 2
    return t


def _silu(x):
    return x / (1.0 + jnp.exp(-x))


def _dot(a, b, **kw):
    return jnp.dot(a, b, preferred_element_type=F32, **kw)


def _bf16_pieces(x):
    hi = x.astype(BF16)
    rest = x - hi.astype(F32)
    mid = rest.astype(BF16)
    lo = (rest - mid.astype(F32)).astype(BF16)
    return hi, mid, lo


def _dot_exact_left(x, sel):
    return sum(_dot(p, sel) for p in _bf16_pieces(x))


def _dot_exact_right(sel, x):
    return sum(_dot(sel, p) for p in _bf16_pieces(x))


def _dot_nt(a, b, **kw):
    return lax.dot_general(a, b, (((1,), (1,)), ((), ())), preferred_element_type=F32, **kw)


def _norm_mm_kernel(x_ref, nw_ref, w_ref, o_ref, xn_ref):
    @pl.when(pl.program_id(1) == 0)
    def _():
        x = x_ref[...]
        ms = jnp.mean(x * x, axis=-1, keepdims=True)
        xn_ref[...] = ((x * lax.rsqrt(ms + NORM_EPS)) * nw_ref[...]).astype(BF16)

    o_ref[...] = _dot(xn_ref[...], w_ref[...]).astype(o_ref.dtype)


def _norm_mm_tap_kernel(x_ref, nw_ref, w_ref, o_ref, t0_ref, t1_ref, xn_ref, *, tap_tile):
    _norm_mm_kernel(x_ref, nw_ref, w_ref, o_ref, xn_ref)

    @pl.when(pl.program_id(1) == tap_tile)
    def _():
        heads, hd = t0_ref.shape[1], t0_ref.shape[2]
        for h in range(heads):
            t0_ref[:, h, :] = o_ref[:, h * hd:(h + 1) * hd]
            t1_ref[:, h, :] = o_ref[:, (heads + h) * hd:(heads + h + 1) * hd]


def _layer_spec(layer, block, index, **kw):
    return pl.BlockSpec((None,) + block, lambda *g: (layer,) + index(*g), **kw)


def norm_matmul(x, norm_w, w, layer, *, tm=1024, tn=1024, out_dtype=F32, tap_col0=None):
    m, k = x.shape
    n = w.shape[2]
    tm, tn = _tile(m, tm), _tile(n, tn)
    in_specs = [pl.BlockSpec((tm, k), lambda i, j: (i, 0)),
                pl.BlockSpec((1, k), lambda i, j: (0, 0)),
                _layer_spec(layer, (k, tn), lambda i, j: (0, j))]
    out_spec = pl.BlockSpec((tm, tn), lambda i, j: (i, j))
    out_shape = jax.ShapeDtypeStruct((m, n), out_dtype)
    if tap_col0 is None:
        body, out_specs, out_shapes = _norm_mm_kernel, out_spec, out_shape
    else:
        assert tap_col0 % tn == 0 and tn % (2 * LANES) == 0
        body = functools.partial(_norm_mm_tap_kernel, tap_tile=tap_col0 // tn)
        tap_heads = tn // 2 // HEAD_DIM
        tap_spec = pl.BlockSpec((tm, tap_heads, HEAD_DIM), lambda i, j: (i, 0, 0))
        tap_shape = jax.ShapeDtypeStruct((m, tap_heads, HEAD_DIM), out_dtype)
        out_specs, out_shapes = (out_spec, tap_spec, tap_spec), (out_shape, tap_shape, tap_shape)
    return pl.pallas_call(
        body,
        out_shape=out_shapes,
        grid=(m // tm, n // tn),
        in_specs=in_specs,
        out_specs=out_specs,
        scratch_shapes=[pltpu.VMEM((tm, k), BF16)],
        compiler_params=_cparams("parallel", "arbitrary"),
        name="norm_matmul",
    )(x, norm_w.reshape(1, k), w)


def _mm_res_kernel(*refs, n_pairs):
    res_ref, o_ref = refs[2 * n_pairs], refs[2 * n_pairs + 1]
    acc = res_ref[...]
    for p in range(n_pairs):
        acc = acc + _dot(refs[2 * p][...], refs[2 * p + 1][...])
    o_ref[...] = acc


def matmul_residual(pairs, res, layer, *, tm=512):
    m, n = res.shape
    tm = _tile(m, tm)
    in_specs, args = [], []
    for a, w, rb in pairs:
        k = a.shape[1]
        assert w.shape[1] % k == 0 and w.shape[2] == n
        in_specs += [pl.BlockSpec((tm, k), lambda i: (i, 0)),
                     _layer_spec(layer, (k, n), lambda i, rb=rb: (rb, 0), pipeline_mode=pl.Buffered(1))]
        args += [a, w]
    in_specs.append(pl.BlockSpec((tm, n), lambda i: (i, 0)))
    return pl.pallas_call(
        functools.partial(_mm_res_kernel, n_pairs=len(pairs)),
        out_shape=jax.ShapeDtypeStruct((m, n), F32),
        grid=(m // tm,),
        in_specs=in_specs,
        out_specs=pl.BlockSpec((tm, n), lambda i: (i, 0)),
        compiler_params=_cparams("parallel"),
        name="matmul_residual",
    )(*args, res)


def _mm_kernel(a_ref, w_ref, o_ref):
    o_ref[...] = _dot(a_ref[...], w_ref[...]).astype(o_ref.dtype)


def matmul(a, w, layer, *, col0=0, n=None, tm=1024, tn=1024, out_dtype=F32):
    m, k = a.shape
    n = w.shape[2] - col0 if n is None else n
    tm, tn = _tile(m, tm), _tile(n, tn)
    assert col0 % tn == 0 and col0 + n <= w.shape[2]
    cb0 = col0 // tn
    return pl.pallas_call(
        _mm_kernel,
        out_shape=jax.ShapeDtypeStruct((m, n), out_dtype),
        grid=(m // tm, n // tn),
        in_specs=[pl.BlockSpec((tm, k), lambda i, j: (i, 0)),
                  _layer_spec(layer, (k, tn), lambda i, j: (0, cb0 + j))],
        out_specs=pl.BlockSpec((tm, tn), lambda i, j: (i, j)),
        compiler_params=_cparams("parallel", "parallel"),
        name="matmul",
    )(a, w)


def _rmsnorm_kernel(x_ref, nw_ref, o_ref):
    x = x_ref[...]
    ms = jnp.mean(x * x, axis=-1, keepdims=True)
    o_ref[...] = ((x * lax.rsqrt(ms + NORM_EPS)) * nw_ref[...]).astype(o_ref.dtype)


def rmsnorm(x, norm_w, *, tm=512, out_dtype=F32):
    m, k = x.shape
    tm = _tile(m, tm)
    return pl.pallas_call(
        _rmsnorm_kernel,
        out_shape=jax.ShapeDtypeStruct((m, k), out_dtype),
        grid=(m // tm,),
        in_specs=[pl.BlockSpec((tm, k), lambda i: (i, 0)),
                  pl.BlockSpec((1, k), lambda i: (0, 0))],
        out_specs=pl.BlockSpec((tm, k), lambda i: (i, 0)),
        compiler_params=_cparams("parallel"),
        name="rmsnorm",
    )(x, norm_w.reshape(1, k))


def _rows_padded(ref, q):
    v = ref[...]
    if v.shape[0] < q:
        v = jnp.concatenate([v, jnp.zeros((q - v.shape[0], v.shape[1]), v.dtype)], axis=0)
    return v


def _causal_conv_chunk(x_ref, buf_ref, cw_ref, cb_ref, q):
    x = _rows_padded(x_ref, q)
    buf_ref[SUBLANES:SUBLANES + q, :] = x
    y = cw_ref[SSD_CONV - 1:SSD_CONV, :] * x
    for k in range(1, SSD_CONV):
        y = y + cw_ref[SSD_CONV - 1 - k:SSD_CONV - k, :] * buf_ref[SUBLANES - k:SUBLANES - k + q, :]
    if cb_ref is not None:
        y = y + cb_ref[...]
    buf_ref[0:SUBLANES, :] = x[q - SUBLANES:q, :]
    return y


def _ssd_kernel(z_ref, xs_ref, b_ref, c_ref, dt_ref, cs_xs_ref, cs_b_ref, cs_c_ref, h0_ref,
                cw_xs_ref, cw_b_ref, cw_c_ref, cb_xs_ref, cb_b_ref, cb_c_ref,
                dtb_ref, alog_ref, dskip_ref, nw_ref, e_ref,
                y_ref, hfin_ref,
                ht_ref, buf_xs, buf_b, buf_c, *, q, n_valid, n_chunks):
    c = pl.program_id(1)
    n_heads = e_ref.shape[1] // SSD_HEADDIM
    gw = (n_heads // SSD_GROUPS) * SSD_HEADDIM

    @pl.when(c == 0)
    def _():
        ht_ref[...] = h0_ref[...].T
        buf_xs[0:SUBLANES, :] = cs_xs_ref[...]
        buf_b[0:SUBLANES, :] = cs_b_ref[...]
        buf_c[0:SUBLANES, :] = cs_c_ref[...]

    xs = _silu(_causal_conv_chunk(xs_ref, buf_xs, cw_xs_ref, cb_xs_ref, q))
    bm = _silu(_causal_conv_chunk(b_ref, buf_b, cw_b_ref, cb_b_ref, q))
    cm = _silu(_causal_conv_chunk(c_ref, buf_c, cw_c_ref, cb_c_ref, q))

    row = lax.broadcasted_iota(jnp.int32, (q, q), 0)
    col = lax.broadcasted_iota(jnp.int32, (q, q), 1)
    tril = row >= col

    dtv = _rows_padded(dt_ref, q) + dtb_ref[...]
    dt = jnp.maximum(dtv, 0.0) + jnp.log(1.0 + jnp.exp(-jnp.abs(dtv)))
    if n_valid < q:
        dt = jnp.where(lax.broadcasted_iota(jnp.int32, dt.shape, 0) < n_valid, dt, 0.0)
    a = dt * (-jnp.exp(alog_ref[...]))
    a_cs = _dot_exact_right(jnp.where(tril, 1.0, 0.0).astype(BF16), a)
    a_cs_t = a_cs.T
    expand = e_ref[...]
    dt_full = _dot_exact_left(dt, expand)
    acs_full = _dot_exact_left(a_cs, expand)
    tot_full = acs_full[q - 1:q, :]
    xr = xs * dt_full
    xr_dec = (xr * jnp.exp(tot_full - acs_full)).astype(BF16)
    exp_acs = jnp.exp(acs_full)
    exp_tot = jnp.exp(tot_full)

    lane = lax.broadcasted_iota(jnp.int32, (q, LANES), 1)
    lo_half = lane < SSD_HEADDIM
    heads_per_tile = LANES // SSD_HEADDIM

    y = dskip_ref[...] * xs
    y_parts = []
    for g in range(SSD_GROUPS):
        bg = bm[:, g * SSD_STATE:(g + 1) * SSD_STATE]
        cg = cm[:, g * SSD_STATE:(g + 1) * SSD_STATE].astype(BF16)
        bg_t = bg.T.astype(BF16)
        cb = _dot(cg, bg_t)
        ht_g = ht_ref[:, g * gw:(g + 1) * gw]
        y_g = _dot(cg, ht_g.astype(BF16)) * exp_acs[:, g * gw:(g + 1) * gw]
        tiles = []
        for t in range(gw // LANES):
            base = g * gw + t * LANES
            xr_t = xr[:, base:base + LANES]
            acc = None
            for e in range(heads_per_tile):
                h = base // SSD_HEADDIM + e
                seg = a_cs[:, h:h + 1] - a_cs_t[h:h + 1, :]
                lmat = jnp.where(tril, jnp.exp(seg), 0.0)
                in_head = lo_half if e == 0 else jnp.logical_not(lo_half)
                part = _dot((cb * lmat).astype(BF16), jnp.where(in_head, xr_t, 0.0).astype(BF16))
                acc = part if acc is None else acc + part
            tiles.append(acc)
        y_g = y_g + jnp.concatenate(tiles, axis=1)
        states = _dot(bg_t, xr_dec[:, g * gw:(g + 1) * gw])
        ht_ref[:, g * gw:(g + 1) * gw] = exp_tot[:, g * gw:(g + 1) * gw] * ht_g + states
        y_parts.append(y_g)
    y = y + jnp.concatenate(y_parts, axis=1)

    y = y * _silu(_rows_padded(z_ref, q))
    for g in range(SSD_GROUPS):
        yg = y[:, g * gw:(g + 1) * gw]
        ms = jnp.mean(yg * yg, axis=-1, keepdims=True)
        y_ref[:, g * gw:(g + 1) * gw] = ((yg * lax.rsqrt(ms + NORM_EPS)) * nw_ref[:, g * gw:(g + 1) * gw]).astype(y_ref.dtype)

    @pl.when(c == n_chunks - 1)
    def _():
        hfin_ref[...] = ht_ref[...].T


def ssd_mixer(proj, dt_raw, conv_state8, h0, wts, *, batch, seq):
    q = SSD_CHUNK
    n_valid = min(seq, q)
    assert seq % n_valid == 0 and n_valid % SUBLANES == 0
    nc = seq // n_valid
    assert n_valid == q or nc == 1
    inner = h0.shape[1]
    gn = SSD_GROUPS * SSD_STATE
    assert inner % gn == 0 and (inner // gn) * gn == inner
    kb = inner // gn
    rows = lambda b, c: b * nc + c
    full = lambda b, c: (0, 0)
    in_specs = [
        pl.BlockSpec((n_valid, inner), lambda b, c: (rows(b, c), 0)),
        pl.BlockSpec((n_valid, inner), lambda b, c: (rows(b, c), 1)),
        pl.BlockSpec((n_valid, gn), lambda b, c: (rows(b, c), 2 * kb)),
        pl.BlockSpec((n_valid, gn), lambda b, c: (rows(b, c), 2 * kb + 1)),
        pl.BlockSpec((n_valid, LANES), lambda b, c: (rows(b, c), 0)),
        pl.BlockSpec((None, SUBLANES, inner), lambda b, c: (b, 0, 0)),
        pl.BlockSpec((None, SUBLANES, gn), lambda b, c: (b, 0, kb)),
        pl.BlockSpec((None, SUBLANES, gn), lambda b, c: (b, 0, kb + 1)),
        pl.BlockSpec((None, inner, SSD_STATE), lambda b, c: (b, 0, 0)),
        pl.BlockSpec((SSD_CONV, inner), full),
        pl.BlockSpec((SSD_CONV, gn), lambda b, c: (0, kb)),
        pl.BlockSpec((SSD_CONV, gn), lambda b, c: (0, kb + 1)),
        pl.BlockSpec((1, inner), full),
        pl.BlockSpec((1, gn), lambda b, c: (0, kb)),
        pl.BlockSpec((1, gn), lambda b, c: (0, kb + 1)),
        pl.BlockSpec((1, LANES), full),
        pl.BlockSpec((1, LANES), full),
        pl.BlockSpec((1, inner), full),
        pl.BlockSpec((1, inner), full),
        pl.BlockSpec((LANES, inner), full),
    ]
    cw, cb = wts["conv_w"], wts["conv_b"]
    y, hfin = pl.pallas_call(
        functools.partial(_ssd_kernel, q=q, n_valid=n_valid, n_chunks=nc),
        out_shape=(jax.ShapeDtypeStruct((batch * nc * q, inner), BF16),
                   jax.ShapeDtypeStruct((batch, inner, SSD_STATE), F32)),
        grid=(batch, nc),
        in_specs=in_specs,
        out_specs=(pl.BlockSpec((q, inner), lambda b, c: (rows(b, c), 0)),
                   pl.BlockSpec((None, inner, SSD_STATE), lambda b, c: (b, 0, 0))),
        scratch_shapes=[pltpu.VMEM((SSD_STATE, inner), F32),
                        pltpu.VMEM((q + SUBLANES, inner), F32),
                        pltpu.VMEM((q + SUBLANES, gn), F32),
                        pltpu.VMEM((q + SUBLANES, gn), F32)],
        compiler_params=_cparams("parallel", "arbitrary"),
        name="ssd_mixer",
    )(proj, proj, proj, proj, dt_raw, conv_state8, conv_state8, conv_state8, h0,
      cw, cw, cw, cb, cb, cb, wts["dt_bias"], wts["a_log"], wts["d_skip"], wts["norm_w"], wts["expand"])
    return y, hfin


def _sc_mixer_kernel(xn_ref, wb_ref, wc_ref, wx_ref, wg_ref, st_ref, w_ref, y_ref, last_ref, buf_ref, *, tq, slabs):
    if slabs == 1:
        @pl.when(pl.program_id(2) == 0)
        def _():
            buf_ref[0:SUBLANES, :] = st_ref[0]

    xn = xn_ref[...]
    prod = _dot(xn, wc_ref[...]) * _dot(xn, wx_ref[...])
    us = []
    for s in range(slabs):
        if slabs > 1:
            buf_ref[0:SUBLANES, :] = st_ref[s]
        ps = prod[s * tq:(s + 1) * tq, :]
        buf_ref[SUBLANES:SUBLANES + tq, :] = ps
        u = w_ref[SC_WIDTH - 1:SC_WIDTH, :] * ps
        for k in range(1, SC_WIDTH):
            u = u + w_ref[SC_WIDTH - 1 - k:SC_WIDTH - k, :] * buf_ref[SUBLANES - k:SUBLANES - k + tq, :]
        tail = ps[tq - SUBLANES:tq, :]
        buf_ref[0:SUBLANES, :] = tail
        last_ref[s] = tail
        us.append(u)
    u = us[0] if slabs == 1 else jnp.concatenate(us, axis=0)
    y_ref[...] = (_dot(xn, wb_ref[...]) * u * _silu(_dot(xn, wg_ref[...]))).astype(y_ref.dtype)


def sc_mixer(xn, w_sc, layer, state8, w, *, batch, seq, tq=1024, tc=512):
    k = xn.shape[1]
    dim = w.shape[1]
    slabs = batch if batch * seq <= tq else 1
    tq = seq if slabs > 1 else _tile(seq, tq)
    nt = seq // tq
    nb = batch // slabs
    ncb = dim // tc
    assert dim % tc == 0 and tq % SUBLANES == 0 and w_sc.shape[2] == 4 * dim

    def part(p):
        return _layer_spec(layer, (k, tc), lambda j, b, t: (0, p * ncb + j))

    return pl.pallas_call(
        functools.partial(_sc_mixer_kernel, tq=tq, slabs=slabs),
        out_shape=(jax.ShapeDtypeStruct((batch * seq, dim), BF16),
                   jax.ShapeDtypeStruct((batch, SUBLANES, dim), F32)),
        grid=(ncb, nb, nt),
        in_specs=[pl.BlockSpec((slabs * tq, k), lambda j, b, t: (b * nt + t, 0)),
                  part(0), part(1), part(2), part(3),
                  pl.BlockSpec((slabs, SUBLANES, tc), lambda j, b, t: (b, 0, j)),
                  pl.BlockSpec((SC_WIDTH, tc), lambda j, b, t: (0, j))],
        out_specs=(pl.BlockSpec((slabs * tq, tc), lambda j, b, t: (b * nt + t, j)),
                   pl.BlockSpec((slabs, SUBLANES, tc), lambda j, b, t: (b, 0, j))),
        scratch_shapes=[pltpu.VMEM((tq + SUBLANES, tc), F32)],
        compiler_params=_cparams("parallel", "parallel", "arbitrary"),
        name="sc_mixer",
    )(xn, w_sc, w_sc, w_sc, w_sc, state8, w)


def _top_blocks(gate, n_valid, axis=1):
    idx = lax.broadcasted_iota(jnp.int32, gate.shape, axis).astype(F32)
    g = jnp.where(idx < jnp.asarray(n_valid, F32), gate, NEG_INF)
    sel = jnp.zeros(gate.shape, F32)
    for _ in range(MOBA_TOPK):
        m = jnp.max(g, axis=axis, keepdims=True)
        first = jnp.min(jnp.where(g == m, idx, float(gate.shape[axis])), axis=axis, keepdims=True)
        pick = idx == first
        sel = jnp.where(pick & (m > 0.5 * NEG_INF), 1.0, sel)
        g = jnp.where(pick, -jnp.inf, g)
    return sel


def _moba_prompt_kernel(slopes_ref, q_ref, k_ref, v_ref, g_ref, o_ref, kmean_ref, kb_ref, vt_ref, mval_ref,
                        bias_ref, diag_ref, *, n_blocks):
    kv = pl.program_id(1)
    i = pl.program_id(2)
    blk = MOBA_BLOCK
    heads = range(Q_PER_KV)

    @pl.when(i == 0)
    def _():
        kmean_ref[...] = jnp.zeros(kmean_ref.shape, F32)
        for j in range(n_blocks):
            kj = k_ref[j * blk:(j + 1) * blk, :]
            kmean_ref[j:j + 1, :] = jnp.sum(kj, axis=0, keepdims=True) * (1.0 / blk)
            kb_ref[j] = kj.astype(BF16)
            vt_ref[j] = v_ref[j * blk:(j + 1) * blk, :].T.astype(BF16)
        rel = (lax.broadcasted_iota(jnp.int32, (blk, blk), 1)
               - lax.broadcasted_iota(jnp.int32, (blk, blk), 0)).astype(F32)
        for g in heads:
            slope = slopes_ref[kv * Q_PER_KV + g]
            bias_ref[:, g * blk:(g + 1) * blk] = -slope * rel
            diag_ref[:, g * blk:(g + 1) * blk] = jnp.where(rel >= 0, -slope * rel, NEG_INF)

    blk_idx = lax.broadcasted_iota(jnp.int32, (kmean_ref.shape[0], blk), 0)
    back = (i - blk_idx).astype(F32) * float(blk)
    kmean = kmean_ref[...]

    qt = []
    for g in heads:
        slope = slopes_ref[kv * Q_PER_KV + g]
        q_t = (q_ref[:, g * HEAD_DIM:(g + 1) * HEAD_DIM] * (HEAD_DIM ** -0.5)).T
        gate = _dot(kmean, q_t, precision=HIGHEST)
        sel = _top_blocks(gate, i, axis=0)
        mval_ref[:, g * blk:(g + 1) * blk] = jnp.where(sel > 0.0, -slope * back, NEG_INF)
        qt.append(q_t.astype(BF16))
    qt = jnp.concatenate(qt, axis=1)

    s = _dot(kb_ref[i], qt) + diag_ref[...]
    m0 = jnp.max(s, axis=0, keepdims=True)
    p = jnp.exp(s - m0)
    l0 = jnp.sum(p, axis=0, keepdims=True)
    acc0 = _dot(vt_ref[i], p.astype(BF16))

    def past_pair(t, carry):
        m, l, acc = carry
        js = (2 * t, 2 * t + 1)
        ss = [_dot(kb_ref[j], qt) + bias_ref[...] for j in js]
        masks = [mval_ref[pl.ds(j, 1), :] for j in js]
        m_new = m
        for s, mask in zip(ss, masks):
            m_new = jnp.maximum(m_new, jnp.max(s, axis=0, keepdims=True) + mask)
        l = jnp.exp(m - m_new) * l
        acc = jnp.exp(m - m_new) * acc
        for j, s, mask in zip(js, ss, masks):
            p = jnp.exp(s - (m_new - mask))
            l = l + jnp.sum(p, axis=0, keepdims=True)
            acc = acc + _dot(vt_ref[j], p.astype(BF16))
        return m_new, l, acc

    _, l, acc = lax.fori_loop(0, (i + 1) // 2, past_pair, (m0, l0, acc0))
    o_t = acc / l
    for g in heads:
        c = slice(g * HEAD_DIM, (g + 1) * HEAD_DIM)
        o_ref[:, c] = (o_t[:, g * blk:(g + 1) * blk].T * _silu(g_ref[:, c])).astype(o_ref.dtype)


def moba_prompt(proj, slopes, *, batch, seq):
    assert seq % MOBA_BLOCK == 0
    nb = seq // MOBA_BLOCK
    nbp = -(-nb // SUBLANES) * SUBLANES
    gw = Q_PER_KV * HEAD_DIM
    kcol = ATT_HEADS
    vcol = ATT_HEADS + KV_HEADS
    gcol = (ATT_HEADS + 2 * KV_HEADS) * HEAD_DIM // gw
    assert gcol * gw == (ATT_HEADS + 2 * KV_HEADS) * HEAD_DIM
    return pl.pallas_call(
        functools.partial(_moba_prompt_kernel, n_blocks=nb),
        out_shape=jax.ShapeDtypeStruct((batch * seq, ATT_HEADS * HEAD_DIM), BF16),
        grid=(batch, KV_HEADS, nb),
        in_specs=[pl.BlockSpec(memory_space=pltpu.SMEM),
                  pl.BlockSpec((MOBA_BLOCK, gw), lambda b, kv, i: (b * nb + i, kv)),
                  pl.BlockSpec((seq, HEAD_DIM), lambda b, kv, i: (b, kcol + kv)),
                  pl.BlockSpec((seq, HEAD_DIM), lambda b, kv, i: (b, vcol + kv)),
                  pl.BlockSpec((MOBA_BLOCK, gw), lambda b, kv, i: (b * nb + i, gcol + kv))],
        out_specs=pl.BlockSpec((MOBA_BLOCK, gw), lambda b, kv, i: (b * nb + i, kv)),
        scratch_shapes=[pltpu.VMEM((nbp, HEAD_DIM), F32),
                        pltpu.VMEM((nb, MOBA_BLOCK, HEAD_DIM), BF16),
                        pltpu.VMEM((nb, HEAD_DIM, MOBA_BLOCK), BF16),
                        pltpu.VMEM((nbp, Q_PER_KV * MOBA_BLOCK), F32),
                        pltpu.VMEM((MOBA_BLOCK, Q_PER_KV * MOBA_BLOCK), F32),
                        pltpu.VMEM((MOBA_BLOCK, Q_PER_KV * MOBA_BLOCK), F32)],
        compiler_params=_cparams("parallel", "parallel", "arbitrary"),
        name="moba_prompt",
    )(slopes, proj, proj, proj, proj)


def _page_specs(layer):
    return [pl.BlockSpec((None, None, PAGE_ROWS, HEAD_DIM),
                         lambda b, s, pt, p=p: (layer, pt[b, s * PAGES_PER_STEP + p], 0, 0))
            for p in range(PAGES_PER_STEP)]


def _kmean_kernel(pt_ref, *refs):
    del pt_ref
    pages, o_ref = refs[:PAGES_PER_STEP], refs[PAGES_PER_STEP]
    for bb in range(BLOCKS_PER_STEP):
        tot = None
        for pg in range(PAGES_PER_BLOCK):
            page = pages[bb * PAGES_PER_BLOCK + pg][...]
            part = jnp.sum(page.reshape(PAGE_ROWS // SUBLANES, SUBLANES, HEAD_DIM), axis=0)
            tot = part if tot is None else tot + part
        per_kv = tot[0:KV_HEADS]
        for par in range(1, SUBLANES // KV_HEADS):
            per_kv = per_kv + tot[par * KV_HEADS:(par + 1) * KV_HEADS]
        o_ref[bb] = per_kv * (1.0 / MOBA_BLOCK)


def paged_block_means(k_pools, page_table, *, layer, n_blocks):
    db = page_table.shape[0]
    assert n_blocks % BLOCKS_PER_STEP == 0 and SUBLANES % KV_HEADS == 0
    return pl.pallas_call(
        _kmean_kernel,
        out_shape=jax.ShapeDtypeStruct((db, n_blocks, KV_HEADS, HEAD_DIM), F32),
        grid_spec=pltpu.PrefetchScalarGridSpec(
            num_scalar_prefetch=1,
            grid=(db, n_blocks // BLOCKS_PER_STEP),
            in_specs=_page_specs(layer),
            out_specs=pl.BlockSpec((None, BLOCKS_PER_STEP, KV_HEADS, HEAD_DIM), lambda b, s, pt: (b, s, 0, 0))),
        compiler_params=_cparams("parallel", "arbitrary"),
        name="paged_block_means",
    )(page_table, *([k_pools] * PAGES_PER_STEP))


def _moba_sample_kernel(pt_ref, q_ref, kmean_ref, slope_ref, knew_ref, vnew_ref, *refs,
                        n_blocks, past_len, dec_seq):
    del pt_ref
    k_pages, v_pages = refs[:PAGES_PER_STEP], refs[PAGES_PER_STEP:2 * PAGES_PER_STEP]
    o_ref, m_ref, l_ref, acc_ref, sel_ref, base_ref = refs[2 * PAGES_PER_STEP:]
    step = pl.program_id(1)
    rows = q_ref.shape[0]
    rpk = rows // KV_HEADS
    q = q_ref[...] * (HEAD_DIM ** -0.5)
    qb = q.astype(BF16)
    slope = slope_ref[:, 0:1]
    lane = lax.broadcasted_iota(jnp.int32, (rows, LANES), 1)
    t_row = jnp.bitwise_and(lax.broadcasted_iota(jnp.int32, (rows, LANES), 0), dec_seq - 1)

    def per_head(fn):
        return jnp.concatenate([fn(h, slice(h * rpk, (h + 1) * rpk)) for h in range(KV_HEADS)], axis=0)

    def page_head(page_ref, h):
        return page_ref[pl.ds(h, PAGE_SIZE, stride=KV_HEADS), :].astype(BF16)

    @pl.when(step == 0)
    def _():
        gate = per_head(lambda h, r: _dot_nt(q[r, :], kmean_ref[h], precision=HIGHEST))
        sel_ref[...] = _top_blocks(gate, n_blocks)
        base_ref[...] = -slope * (past_len + t_row - lane).astype(F32)
        dist = (t_row - lane).astype(F32)
        s = (per_head(lambda h, r: _dot_nt(qb[r, :], knew_ref[h].astype(BF16)))
             + jnp.where(dist >= 0, -slope * dist, NEG_INF))
        m = jnp.max(s, axis=1, keepdims=True)
        p = jnp.exp(s - m)
        pb = p.astype(BF16)
        m_ref[...] = jnp.broadcast_to(m, m_ref.shape)
        l_ref[...] = jnp.broadcast_to(jnp.sum(p, axis=1, keepdims=True), l_ref.shape)
        acc_ref[...] = per_head(lambda h, r: _dot(pb[r, :], vnew_ref[h].astype(BF16)))

    sel = sel_ref[...]
    base = base_ref[...]
    m, l, acc = m_ref[:, 0:1], l_ref[:, 0:1], acc_ref[...]
    scores, offsets = [], []
    m_new = m
    for bb in range(BLOCKS_PER_STEP):
        blk_id = step * BLOCKS_PER_STEP + bb
        chosen = jnp.sum(jnp.where(lane == blk_id, sel, 0.0), axis=1, keepdims=True) > 0.0
        for pg in range(PAGES_PER_BLOCK):
            page_pos = (blk_id * MOBA_BLOCK + pg * PAGE_SIZE).astype(F32)
            mcol = jnp.where(chosen, slope * page_pos, NEG_INF)
            k_page = k_pages[bb * PAGES_PER_BLOCK + pg]
            s = per_head(lambda h, r: _dot_nt(qb[r, :], page_head(k_page, h))) + base
            m_new = jnp.maximum(m_new, jnp.max(s, axis=1, keepdims=True) + mcol)
            scores.append(s)
            offsets.append(mcol)
    alpha = jnp.exp(m - m_new)
    l, acc, m = alpha * l, alpha * acc, m_new
    for pg, (s, mcol) in enumerate(zip(scores, offsets)):
        p = jnp.exp(s - (m - mcol))
        pb = p.astype(BF16)
        l = l + jnp.sum(p, axis=1, keepdims=True)
        acc = acc + per_head(lambda h, r: _dot(pb[r, :], page_head(v_pages[pg], h)))
    m_ref[...] = jnp.broadcast_to(m, m_ref.shape)
    l_ref[...] = jnp.broadcast_to(l, l_ref.shape)
    acc_ref[...] = acc

    @pl.when(step == n_blocks // BLOCKS_PER_STEP - 1)
    def _():
        o_ref[...] = acc / l


def moba_sample(q_rows, kmean_t, slope_rows, k_new_pad, v_new_pad, k_pools, v_pools, page_table,
                *, layer, n_blocks, past_len, dec_seq):
    db, rows, _ = q_rows.shape
    rpk = rows // KV_HEADS
    for n in (KV_HEADS, dec_seq, rpk):
        assert n & (n - 1) == 0
    assert n_blocks % BLOCKS_PER_STEP == 0 and dec_seq * KV_HEADS <= LANES
    per_b = lambda *shape: pl.BlockSpec((None,) + shape, lambda b, s, pt: (b,) + (0,) * len(shape))
    return pl.pallas_call(
        functools.partial(_moba_sample_kernel, n_blocks=n_blocks, past_len=past_len, dec_seq=dec_seq),
        out_shape=jax.ShapeDtypeStruct((db, rows, HEAD_DIM), F32),
        grid_spec=pltpu.PrefetchScalarGridSpec(
            num_scalar_prefetch=1,
            grid=(db, n_blocks // BLOCKS_PER_STEP),
            in_specs=[per_b(rows, HEAD_DIM),
                      per_b(KV_HEADS, LANES, HEAD_DIM),
                      pl.BlockSpec((rows, LANES), lambda b, s, pt: (0, 0)),
                      per_b(KV_HEADS, LANES, HEAD_DIM),
                      per_b(KV_HEADS, LANES, HEAD_DIM)] + _page_specs(layer) + _page_specs(layer),
            out_specs=per_b(rows, HEAD_DIM),
            scratch_shapes=[pltpu.VMEM((rows, LANES), F32), pltpu.VMEM((rows, LANES), F32),
                            pltpu.VMEM((rows, HEAD_DIM), F32), pltpu.VMEM((rows, LANES), F32),
                            pltpu.VMEM((rows, PAGE_SIZE), F32)]),
        compiler_params=_cparams("parallel", "arbitrary"),
        name="moba_sample",
    )(page_table, q_rows, kmean_t, slope_rows, k_new_pad, v_new_pad,
      *([k_pools] * PAGES_PER_STEP), *([v_pools] * PAGES_PER_STEP))


def _gate_kernel(o_ref, g_ref, y_ref):
    y_ref[...] = (o_ref[...] * _silu(g_ref[...])).astype(y_ref.dtype)


def silu_gate(o, proj, *, col0, tn=1024):
    m, n = o.shape
    assert col0 % tn == 0 and n % tn == 0
    cb = col0 // tn
    return pl.pallas_call(
        _gate_kernel,
        out_shape=jax.ShapeDtypeStruct((m, n), BF16),
        grid=(n // tn,),
        in_specs=[pl.BlockSpec((m, tn), lambda j: (0, j)),
                  pl.BlockSpec((m, tn), lambda j: (0, cb + j))],
        out_specs=pl.BlockSpec((m, tn), lambda j: (0, j)),
        compiler_params=_cparams("parallel"),
        name="silu_gate",
    )(o, proj)


def _cross_layer_kernel(x_ref, nw_ref, wq_ref, k_ref, v_ref, wo_ref, *refs, slabs, emit_x, emit_norm):
    refs = list(refs)
    nn_ref = refs.pop(0) if emit_norm else None
    o_ref = refs.pop(0) if emit_x else None
    n_ref = refs.pop(0) if emit_norm else None
    width = CA_HEADS * CA_HEAD_DIM
    x = x_ref[...]
    rows = x.shape[0] // slabs
    ms = jnp.mean(x * x, axis=-1, keepdims=True)
    xn = ((x * lax.rsqrt(ms + NORM_EPS)) * nw_ref[...]).astype(BF16)
    qg = _dot(xn, wq_ref[...])
    gated_rows = []
    for s_idx in range(slabs):
        r = slice(s_idx * rows, (s_idx + 1) * rows)
        gated = []
        for h in range(CA_HEADS):
            c = slice(h * CA_HEAD_DIM, (h + 1) * CA_HEAD_DIM)
            q = (qg[r, c] * (CA_HEAD_DIM ** -0.5)).astype(BF16)
            s = _dot_nt(q, k_ref[s_idx, :, c].astype(BF16))
            m = jnp.max(s, axis=1, keepdims=True)
            p = jnp.exp(s - m)
            l = jnp.sum(p, axis=1, keepdims=True)
            o = _dot(p.astype(BF16), v_ref[s_idx, :, c].astype(BF16)) / l
            g = qg[r, width + h * CA_HEAD_DIM:width + (h + 1) * CA_HEAD_DIM]
            gated.append(o * _silu(g))
        gated_rows.append(jnp.concatenate(gated, axis=1))
    og = gated_rows[0] if slabs == 1 else jnp.concatenate(gated_rows, axis=0)
    x_new = x + _dot(og.astype(BF16), wo_ref[...])
    if emit_x:
        o_ref[...] = x_new
    if emit_norm:
        ms_new = jnp.mean(x_new * x_new, axis=-1, keepdims=True)
        n_ref[...] = ((x_new * lax.rsqrt(ms_new + NORM_EPS)) * nn_ref[...]).astype(n_ref.dtype)


def cross_layer(x, norm_w, w_q, mem_k, mem_v, w_out, layer, *, batch, seq, tm=512,
                next_norm_w=None, next_dtype=F32, emit_x=True):
    width = CA_HEADS * CA_HEAD_DIM
    d = x.shape[1]
    n_mem = mem_k.shape[1]
    slabs = batch if batch * seq <= tm else 1
    tm = batch * seq if slabs > 1 else _tile(seq, tm)
    nt = batch * seq // tm // (batch // slabs)
    const = lambda i: (0, 0)
    emit_norm = next_norm_w is not None
    assert emit_x or emit_norm
    row_spec = pl.BlockSpec((tm, d), lambda i: (i, 0))
    in_specs = [row_spec,
                pl.BlockSpec((1, d), const),
                _layer_spec(layer, (d, 2 * width), const),
                pl.BlockSpec((slabs, n_mem, width), lambda i: (i // nt, 0, 0)),
                pl.BlockSpec((slabs, n_mem, width), lambda i: (i // nt, 0, 0)),
                _layer_spec(layer, (width, d), const)]
    args = [x, norm_w.reshape(1, d), w_q, mem_k, mem_v, w_out]
    out_specs, out_shapes = [], []
    if emit_norm:
        in_specs.append(pl.BlockSpec((1, d), const))
        args.append(next_norm_w.reshape(1, d))
    if emit_x:
        out_specs.append(row_spec)
        out_shapes.append(jax.ShapeDtypeStruct((batch * seq, d), F32))
    if emit_norm:
        out_specs.append(row_spec)
        out_shapes.append(jax.ShapeDtypeStruct((batch * seq, d), next_dtype))
    outs = pl.pallas_call(
        functools.partial(_cross_layer_kernel, slabs=slabs, emit_x=emit_x, emit_norm=emit_norm),
        out_shape=tuple(out_shapes),
        grid=(batch * seq // tm,),
        in_specs=in_specs,
        out_specs=tuple(out_specs),
        compiler_params=_cparams("parallel"),
        name="cross_layer",
    )(*args)
    return outs[0] if len(outs) == 1 else outs


def _alibi_slopes():
    return np.array([2.0 ** (-8.0 * (h + 1) / ATT_HEADS) for h in range(ATT_HEADS)], dtype=np.float32)


def _pad_rows(a, n, front=False):
    extra = n - a.shape[-2]
    pad = [(0, 0)] * a.ndim
    pad[-2] = (extra, 0) if front else (0, extra)
    return jnp.pad(a, pad)


def _pad_lanes(a):
    pad = [(0, 0)] * a.ndim
    pad[-1] = (0, LANES - a.shape[-1])
    return jnp.pad(a, pad)


def _even_layer(x, w, layer, conv_state, ssm_state, sc_state, *, batch, seq, xn=None):
    d = x.shape[1]
    inner = d
    conv_ch = inner + 2 * SSD_GROUPS * SSD_STATE
    if xn is None:
        xn = rmsnorm(x, w["norm_mix"], out_dtype=BF16)
    proj = matmul(xn, w["w_in"], layer, n=inner + conv_ch)
    dt_raw = matmul(xn, w["w_in"], layer, col0=inner + conv_ch, n=LANES)
    new_conv = proj.reshape(batch, seq, -1)[:, seq - (SSD_CONV - 1):, inner:inner + conv_ch]

    y, h_fin = ssd_mixer(proj, dt_raw, _pad_rows(conv_state, SUBLANES, front=True),
                         ssm_state.reshape(batch, inner, SSD_STATE), w, batch=batch, seq=seq)
    if y.shape[0] != batch * seq:
        y = y.reshape(batch, -1, inner)[:, :seq].reshape(batch * seq, inner)

    y_sc, sc_tail = sc_mixer(xn, w["w_sc"], layer, _pad_rows(sc_state, SUBLANES, front=True), w["sc_w"],
                             batch=batch, seq=seq)
    x = matmul_residual([(y, w["w_out"], 0), (y_sc, w["w_out"], 1)], x, layer)
    return (x, new_conv, h_fin.reshape(batch, inner // SSD_HEADDIM, SSD_HEADDIM, SSD_STATE),
            sc_tail[:, SUBLANES - (SC_WIDTH - 1):, :])


def _odd_prompt(x, w, slopes, *, layer, batch, seq):
    att_q = ATT_HEADS * HEAD_DIM
    proj, k, v = norm_matmul(x, w["norm_mix"], w["w_in"], layer, tap_col0=att_q)
    og = moba_prompt(proj, slopes, batch=batch, seq=seq)
    return (matmul_residual([(og, w["w_out"], 0)], x, layer),
            k.reshape(batch, seq, KV_HEADS, HEAD_DIM), v.reshape(batch, seq, KV_HEADS, HEAD_DIM))


def _odd_sample(x, w, k_pools, v_pools, page_table, slope_rows, *, layer, batch, seq, past_len):
    assert past_len % MOBA_BLOCK == 0 and seq <= MOBA_BLOCK and seq <= LANES
    n_blocks = past_len // MOBA_BLOCK
    assert MOBA_TOPK <= n_blocks <= LANES
    att_q, att_kv = ATT_HEADS * HEAD_DIM, KV_HEADS * HEAD_DIM
    proj, k_new, v_new = norm_matmul(x, w["norm_mix"], w["w_in"], layer, tap_col0=att_q)
    q_rows = proj[:, :att_q].reshape(batch, seq, KV_HEADS, Q_PER_KV, HEAD_DIM).transpose(0, 2, 3, 1, 4)
    q_rows = q_rows.reshape(batch, ATT_HEADS * seq, HEAD_DIM)
    kmean = paged_block_means(k_pools, page_table, layer=layer, n_blocks=n_blocks)
    new_keys = lambda a: _pad_rows(a.reshape(batch, seq, KV_HEADS, HEAD_DIM).transpose(0, 2, 1, 3), LANES)
    o = moba_sample(q_rows, _pad_rows(kmean.transpose(0, 2, 1, 3), LANES), slope_rows,
                    new_keys(k_new), new_keys(v_new),
                    k_pools, v_pools, page_table, layer=layer, n_blocks=n_blocks, past_len=past_len, dec_seq=seq)
    o = o.reshape(batch, KV_HEADS, Q_PER_KV, seq, HEAD_DIM).transpose(0, 3, 1, 2, 4).reshape(batch * seq, att_q)
    og = silu_gate(o, proj, col0=att_q + 2 * att_kv)
    return (matmul_residual([(og, w["w_out"], 0)], x, layer),
            k_new.reshape(batch, seq, KV_HEADS, HEAD_DIM), v_new.reshape(batch, seq, KV_HEADS, HEAD_DIM))


def _cross_layer(x, w, layer, mem_k, mem_v, *, batch, seq, **follow):
    return cross_layer(x, w["norm_cross"], w["ca_w_q"], mem_k, mem_v, w["ca_w_out"], layer, batch=batch, seq=seq,
                       **follow)


def kernel(x_prompt, x_sample, mem_prompt, cache_k, cache_v, page_table, state_conv, state_ssm, state_sc,
           cache_mem_k, cache_mem_v, norm_mix_w, norm_cross_w, norm_mem_w, final_norm_w,
           ev_w_in, ev_conv_w, ev_conv_b, ev_dt_bias, ev_a_log, ev_d_skip, ev_norm_w, ev_sc_w, ev_w_out,
           od_w_in, od_w_out, ca_w_q, ca_w_kv, ca_w_out):
    bp, sp, d = x_prompt.shape
    bs, ss, _ = x_sample.shape
    depth = norm_mix_w.shape[0]
    n_mem = mem_prompt.shape[1]
    n_heads = ev_dt_bias.shape[1]
    inner = n_heads * SSD_HEADDIM
    conv_ch = inner + 2 * SSD_GROUPS * SSD_STATE
    past_len = page_table.shape[1] * PAGE_SIZE
    ca_w = CA_HEADS * CA_HEAD_DIM
    assert inner == d and n_heads <= LANES

    slopes = jnp.asarray(_alibi_slopes())
    slope_rows = jnp.asarray(np.repeat(_alibi_slopes(), ss)[:, None] * np.ones((1, LANES), np.float32))
    expand_np = np.zeros((LANES, inner), np.float32)
    for hd in range(n_heads):
        expand_np[hd, hd * SSD_HEADDIM:(hd + 1) * SSD_HEADDIM] = 1.0
    expand = jnp.asarray(expand_np, dtype=BF16)

    xp = x_prompt.reshape(bp * sp, d)
    xs = x_sample.reshape(bs * ss, d)
    mem = mem_prompt.reshape(bp * n_mem, d)
    k_pools = cache_k.reshape(cache_k.shape[0], cache_k.shape[1], PAGE_ROWS, HEAD_DIM)
    v_pools = cache_v.reshape(cache_v.shape[0], cache_v.shape[1], PAGE_ROWS, HEAD_DIM)

    dt0 = inner + conv_ch
    ev_w_in_b = ev_w_in.astype(BF16)
    ev_w_sc_b = ev_w_in_b[:, :, dt0 + n_heads:]
    ev_w_out_b = ev_w_out.astype(BF16)
    od_w_in_b, od_w_out_b = od_w_in.astype(BF16), od_w_out.astype(BF16)
    ca_w_q_b, ca_w_kv_b, ca_w_out_b = ca_w_q.astype(BF16), ca_w_kv.astype(BF16), ca_w_out.astype(BF16)

    pk, pv, sk, sv = [], [], [], []
    pconv, pssm, psc, sconv, sssm, ssc = [], [], [], [], [], []
    pmk, pmv = [], []
    xn_p = xn_s = None
    for l in range(depth):
        i = l // 2
        if l % 2 == 0:
            w = {
                "norm_mix": norm_mix_w[l],
                "w_in": ev_w_in_b,
                "w_sc": ev_w_sc_b,
                "w_out": ev_w_out_b,
                "conv_w": ev_conv_w[i],
                "conv_b": ev_conv_b[i].reshape(1, conv_ch),
                "dt_bias": _pad_lanes(ev_dt_bias[i].reshape(1, n_heads)),
                "a_log": _pad_lanes(ev_a_log[i].reshape(1, n_heads)),
                "d_skip": jnp.repeat(ev_d_skip[i], SSD_HEADDIM).reshape(1, inner),
                "norm_w": ev_norm_w[i].reshape(1, inner),
                "expand": expand,
                "sc_w": ev_sc_w[i],
            }
            xp, c1, s1, q1 = _even_layer(xp, w, i, jnp.zeros((bp, SSD_CONV - 1, conv_ch), F32),
                                         jnp.zeros((bp, n_heads, SSD_HEADDIM, SSD_STATE), F32),
                                         jnp.zeros((bp, SC_WIDTH - 1, d), F32), batch=bp, seq=sp, xn=xn_p)
            xs, c2, s2, q2 = _even_layer(xs, w, i, state_conv[i], state_ssm[i], state_sc[i], batch=bs, seq=ss,
                                         xn=xn_s)
            pconv.append(c1); pssm.append(s1); psc.append(q1)
            sconv.append(c2); sssm.append(s2); ssc.append(q2)
        else:
            w = {"norm_mix": norm_mix_w[l], "w_in": od_w_in_b, "w_out": od_w_out_b}
            xp, k1, v1 = _odd_prompt(xp, w, slopes, layer=i, batch=bp, seq=sp)
            xs, k2, v2 = _odd_sample(xs, w, k_pools, v_pools, page_table, slope_rows,
                                     layer=i, batch=bs, seq=ss, past_len=past_len)
            pk.append(k1); pv.append(v1); sk.append(k2); sv.append(v2)
        wc = {"norm_cross": norm_cross_w[l], "ca_w_q": ca_w_q_b, "ca_w_out": ca_w_out_b}
        mkv = norm_matmul(mem, norm_mem_w[l], ca_w_kv_b, l)
        mk = mkv[:, :ca_w].reshape(bp, n_mem, ca_w)
        mv = mkv[:, ca_w:].reshape(bp, n_mem, ca_w)
        pmk.append(mk.reshape(bp, n_mem, CA_HEADS, CA_HEAD_DIM))
        pmv.append(mv.reshape(bp, n_mem, CA_HEADS, CA_HEAD_DIM))
        if l == depth - 1:
            follow = dict(next_norm_w=final_norm_w, next_dtype=F32, emit_x=False)
        elif (l + 1) % 2 == 0:
            follow = dict(next_norm_w=norm_mix_w[l + 1], next_dtype=BF16)
        else:
            follow = {}
        mem_ks, mem_vs = cache_mem_k[l].reshape(bs, n_mem, ca_w), cache_mem_v[l].reshape(bs, n_mem, ca_w)
        out_p = _cross_layer(xp, wc, l, mk, mv, batch=bp, seq=sp, **follow)
        out_s = _cross_layer(xs, wc, l, mem_ks, mem_vs, batch=bs, seq=ss, **follow)
        if l == depth - 1:
            y_prompt, y_sample = out_p.reshape(bp, sp, d), out_s.reshape(bs, ss, d)
        elif follow:
            (xp, xn_p), (xs, xn_s) = out_p, out_s
        else:
            xp, xs, xn_p, xn_s = out_p, out_s, None, None
    return (y_prompt, y_sample,
            jnp.stack(pk), jnp.stack(pv), jnp.stack(pconv), jnp.stack(pssm), jnp.stack(psc),
            jnp.stack(pmk), jnp.stack(pmv),
            jnp.stack(sk), jnp.stack(sv), jnp.stack(sconv), jnp.stack(sssm), jnp.stack(ssc))
```

```python
import functools
import math

import numpy as np
import jax
import jax.numpy as jnp
from jax import lax
from jax.experimental import pallas as pl
from jax.experimental.pallas import tpu as pltpu

F32 = jnp.float32
BF16 = jnp.bfloat16
HIGHEST = lax.Precision.HIGHEST

NORM_EPS = 1e-5
NEG_INF = -1e30

LANES = 128
SUBLANES = 8
VMEM_LIMIT = 48 * 1024 * 1024

SSD_HEADDIM = 64
SSD_GROUPS = 4
SSD_STATE = 128
SSD_CONV = 4
SSD_CHUNK = 128
SC_WIDTH = 3
ATT_HEADS = 16
KV_HEADS = 4
HEAD_DIM = 128
Q_PER_KV = ATT_HEADS // KV_HEADS
MOBA_BLOCK = 256
MOBA_TOPK = 3
PAGE_SIZE = 128
PAGES_PER_BLOCK = MOBA_BLOCK // PAGE_SIZE
PAGE_ROWS = PAGE_SIZE * KV_HEADS
BLOCKS_PER_STEP = 8
PAGES_PER_STEP = BLOCKS_PER_STEP * PAGES_PER_BLOCK
CA_HEADS = 4
CA_HEAD_DIM = 128


def _cparams(*sem):
    return pltpu.CompilerParams(dimension_semantics=sem, vmem_limit_bytes=VMEM_LIMIT)


def _tile(n, pref):
    if n <= pref:
        return n
    t = pref
    while n % t:
        t //= 2
    return t


def _silu(x):
    return x / (1.0 + jnp.exp(-x))


def _dot(a, b, **kw):
    return jnp.dot(a, b, preferred_element_type=F32, **kw)


def _bf16_pieces(x):
    hi = x.astype(BF16)
    rest = x - hi.astype(F32)
    mid = rest.astype(BF16)
    lo = (rest - mid.astype(F32)).astype(BF16)
    return hi, mid, lo


def _dot_exact_left(x, sel):
    return sum(_dot(p, sel) for p in _bf16_pieces(x))


def _dot_exact_right(sel, x):
    return sum(_dot(sel, p) for p in _bf16_pieces(x))


def _dot_nt(a, b, **kw):
    return lax.dot_general(a, b, (((1,), (1,)), ((), ())), preferred_element_type=F32, **kw)


def _norm_mm_kernel(x_ref, nw_ref, w_ref, o_ref, xn_ref):
    @pl.when(pl.program_id(1) == 0)
    def _():
        x = x_ref[...]
        ms = jnp.mean(x * x, axis=-1, keepdims=True)
        xn_ref[...] = ((x * lax.rsqrt(ms + NORM_EPS)) * nw_ref[...]).astype(BF16)

    o_ref[...] = _dot(xn_ref[...], w_ref[...]).astype(o_ref.dtype)


def _norm_mm_tap_kernel(x_ref, nw_ref, w_ref, o_ref, t0_ref, t1_ref, xn_ref, *, tap_tile):
    _norm_mm_kernel(x_ref, nw_ref, w_ref, o_ref, xn_ref)

    @pl.when(pl.program_id(1) == tap_tile)
    def _():
        heads, hd = t0_ref.shape[1], t0_ref.shape[2]
        for h in range(heads):
            t0_ref[:, h, :] = o_ref[:, h * hd:(h + 1) * hd]
            t1_ref[:, h, :] = o_ref[:, (heads + h) * hd:(heads + h + 1) * hd]


def _layer_spec(layer, block, index, **kw):
    return pl.BlockSpec((None,) + block, lambda *g: (layer,) + index(*g), **kw)


def norm_matmul(x, norm_w, w, layer, *, tm=1024, tn=1024, out_dtype=F32, tap_col0=None):
    m, k = x.shape
    n = w.shape[2]
    tm, tn = _tile(m, tm), _tile(n, tn)
    in_specs = [pl.BlockSpec((tm, k), lambda i, j: (i, 0)),
                pl.BlockSpec((1, k), lambda i, j: (0, 0)),
                _layer_spec(layer, (k, tn), lambda i, j: (0, j))]
    out_spec = pl.BlockSpec((tm, tn), lambda i, j: (i, j))
    out_shape = jax.ShapeDtypeStruct((m, n), out_dtype)
    if tap_col0 is None:
        body, out_specs, out_shapes = _norm_mm_kernel, out_spec, out_shape
    else:
        assert tap_col0 % tn == 0 and tn % (2 * LANES) == 0
        body = functools.partial(_norm_mm_tap_kernel, tap_tile=tap_col0 // tn)
        tap_heads = tn // 2 // HEAD_DIM
        tap_spec = pl.BlockSpec((tm, tap_heads, HEAD_DIM), lambda i, j: (i, 0, 0))
        tap_shape = jax.ShapeDtypeStruct((m, tap_heads, HEAD_DIM), out_dtype)
        out_specs, out_shapes = (out_spec, tap_spec, tap_spec), (out_shape, tap_shape, tap_shape)
    return pl.pallas_call(
        body,
        out_shape=out_shapes,
        grid=(m // tm, n // tn),
        in_specs=in_specs,
        out_specs=out_specs,
        scratch_shapes=[pltpu.VMEM((tm, k), BF16)],
        compiler_params=_cparams("parallel", "arbitrary"),
        name="norm_matmul",
    )(x, norm_w.reshape(1, k), w)


def _mm_res_kernel(*refs, n_pairs):
    res_ref, o_ref = refs[2 * n_pairs], refs[2 * n_pairs + 1]
    acc = res_ref[...]
    for p in range(n_pairs):
        acc = acc + _dot(refs[2 * p][...], refs[2 * p + 1][...])
    o_ref[...] = acc


def matmul_residual(pairs, res, layer, *, tm=512):
    m, n = res.shape
    tm = _tile(m, tm)
    in_specs, args = [], []
    for a, w, rb in pairs:
        k = a.shape[1]
        assert w.shape[1] % k == 0 and w.shape[2] == n
        in_specs += [pl.BlockSpec((tm, k), lambda i: (i, 0)),
                     _layer_spec(layer, (k, n), lambda i, rb=rb: (rb, 0), pipeline_mode=pl.Buffered(1))]
        args += [a, w]
    in_specs.append(pl.BlockSpec((tm, n), lambda i: (i, 0)))
    return pl.pallas_call(
        functools.partial(_mm_res_kernel, n_pairs=len(pairs)),
        out_shape=jax.ShapeDtypeStruct((m, n), F32),
        grid=(m // tm,),
        in_specs=in_specs,
        out_specs=pl.BlockSpec((tm, n), lambda i: (i, 0)),
        compiler_params=_cparams("parallel"),
        name="matmul_residual",
    )(*args, res)


def _mm_kernel(a_ref, w_ref, o_ref):
    o_ref[...] = _dot(a_ref[...], w_ref[...]).astype(o_ref.dtype)


def matmul(a, w, layer, *, col0=0, n=None, tm=1024, tn=1024, out_dtype=F32):
    m, k = a.shape
    n = w.shape[2] - col0 if n is None else n
    tm, tn = _tile(m, tm), _tile(n, tn)
    assert col0 % tn == 0 and col0 + n <= w.shape[2]
    cb0 = col0 // tn
    return pl.pallas_call(
        _mm_kernel,
        out_shape=jax.ShapeDtypeStruct((m, n), out_dtype),
        grid=(m // tm, n // tn),
        in_specs=[pl.BlockSpec((tm, k), lambda i, j: (i, 0)),
                  _layer_spec(layer, (k, tn), lambda i, j: (0, cb0 + j))],
        out_specs=pl.BlockSpec((tm, tn), lambda i, j: (i, j)),
        compiler_params=_cparams("parallel", "parallel"),
        name="matmul",
    )(a, w)


def _rmsnorm_kernel(x_ref, nw_ref, o_ref):
    x = x_ref[...]
    ms = jnp.mean(x * x, axis=-1, keepdims=True)
    o_ref[...] = ((x * lax.rsqrt(ms + NORM_EPS)) * nw_ref[...]).astype(o_ref.dtype)


def rmsnorm(x, norm_w, *, tm=512, out_dtype=F32):
    m, k = x.shape
    tm = _tile(m, tm)
    return pl.pallas_call(
        _rmsnorm_kernel,
        out_shape=jax.ShapeDtypeStruct((m, k), out_dtype),
        grid=(m // tm,),
        in_specs=[pl.BlockSpec((tm, k), lambda i: (i, 0)),
                  pl.BlockSpec((1, k), lambda i: (0, 0))],
        out_specs=pl.BlockSpec((tm, k), lambda i: (i, 0)),
        compiler_params=_cparams("parallel"),
        name="rmsnorm",
    )(x, norm_w.reshape(1, k))


def _rows_padded(ref, q):
    v = ref[...]
    if v.shape[0] < q:
        v = jnp.concatenate([v, jnp.zeros((q - v.shape[0], v.shape[1]), v.dtype)], axis=0)
    return v


def _causal_conv_chunk(x_ref, buf_ref, cw_ref, cb_ref, q):
    x = _rows_padded(x_ref, q)
    buf_ref[SUBLANES:SUBLANES + q, :] = x
    y = cw_ref[SSD_CONV - 1:SSD_CONV, :] * x
    for k in range(1, SSD_CONV):
        y = y + cw_ref[SSD_CONV - 1 - k:SSD_CONV - k, :] * buf_ref[SUBLANES - k:SUBLANES - k + q, :]
    if cb_ref is not None:
        y = y + cb_ref[...]
    buf_ref[0:SUBLANES, :] = x[q - SUBLANES:q, :]
    return y


def _ssd_kernel(z_ref, xs_ref, b_ref, c_ref, dt_ref, cs_xs_ref, cs_b_ref, cs_c_ref, h0_ref,
                cw_xs_ref, cw_b_ref, cw_c_ref, cb_xs_ref, cb_b_ref, cb_c_ref,
                dtb_ref, alog_ref, dskip_ref, nw_ref, e_ref,
                y_ref, hfin_ref,
                ht_ref, buf_xs, buf_b, buf_c, *, q, n_valid, n_chunks):
    c = pl.program_id(1)
    n_heads = e_ref.shape[1] // SSD_HEADDIM
    gw = (n_heads // SSD_GROUPS) * SSD_HEADDIM

    @pl.when(c == 0)
    def _():
        ht_ref[...] = h0_ref[...].T
        buf_xs[0:SUBLANES, :] = cs_xs_ref[...]
        buf_b[0:SUBLANES, :] = cs_b_ref[...]
        buf_c[0:SUBLANES, :] = cs_c_ref[...]

    xs = _silu(_causal_conv_chunk(xs_ref, buf_xs, cw_xs_ref, cb_xs_ref, q))
    bm = _silu(_causal_conv_chunk(b_ref, buf_b, cw_b_ref, cb_b_ref, q))
    cm = _silu(_causal_conv_chunk(c_ref, buf_c, cw_c_ref, cb_c_ref, q))

    row = lax.broadcasted_iota(jnp.int32, (q, q), 0)
    col = lax.broadcasted_iota(jnp.int32, (q, q), 1)
    tril = row >= col

    dtv = _rows_padded(dt_ref, q) + dtb_ref[...]
    dt = jnp.maximum(dtv, 0.0) + jnp.log(1.0 + jnp.exp(-jnp.abs(dtv)))
    if n_valid < q:
        dt = jnp.where(lax.broadcasted_iota(jnp.int32, dt.shape, 0) < n_valid, dt, 0.0)
    a = dt * (-jnp.exp(alog_ref[...]))
    a_cs = _dot_exact_right(jnp.where(tril, 1.0, 0.0).astype(BF16), a)
    a_cs_t = a_cs.T
    expand = e_ref[...]
    dt_full = _dot_exact_left(dt, expand)
    acs_full = _dot_exact_left(a_cs, expand)
    tot_full = acs_full[q - 1:q, :]
    xr = xs * dt_full
    xr_dec = (xr * jnp.exp(tot_full - acs_full)).astype(BF16)
    exp_acs = jnp.exp(acs_full)
    exp_tot = jnp.exp(tot_full)

    lane = lax.broadcasted_iota(jnp.int32, (q, LANES), 1)
    lo_half = lane < SSD_HEADDIM
    heads_per_tile = LANES // SSD_HEADDIM

    y = dskip_ref[...] * xs
    y_parts = []
    for g in range(SSD_GROUPS):
        bg = bm[:, g * SSD_STATE:(g + 1) * SSD_STATE]
        cg = cm[:, g * SSD_STATE:(g + 1) * SSD_STATE].astype(BF16)
        bg_t = bg.T.astype(BF16)
        cb = _dot(cg, bg_t)
        ht_g = ht_ref[:, g * gw:(g + 1) * gw]
        y_g = _dot(cg, ht_g.astype(BF16)) * exp_acs[:, g * gw:(g + 1) * gw]
        tiles = []
        for t in range(gw // LANES):
            base = g * gw + t * LANES
            xr_t = xr[:, base:base + LANES]
            acc = None
            for e in range(heads_per_tile):
                h = base // SSD_HEADDIM + e
                seg = a_cs[:, h:h + 1] - a_cs_t[h:h + 1, :]
                lmat = jnp.where(tril, jnp.exp(seg), 0.0)
                in_head = lo_half if e == 0 else jnp.logical_not(lo_half)
                part = _dot((cb * lmat).astype(BF16), jnp.where(in_head, xr_t, 0.0).astype(BF16))
                acc = part if acc is None else acc + part
            tiles.append(acc)
        y_g = y_g + jnp.concatenate(tiles, axis=1)
        states = _dot(bg_t, xr_dec[:, g * gw:(g + 1) * gw])
        ht_ref[:, g * gw:(g + 1) * gw] = exp_tot[:, g * gw:(g + 1) * gw] * ht_g + states
        y_parts.append(y_g)
    y = y + jnp.concatenate(y_parts, axis=1)

    y = y * _silu(_rows_padded(z_ref, q))
    for g in range(SSD_GROUPS):
        yg = y[:, g * gw:(g + 1) * gw]
        ms = jnp.mean(yg * yg, axis=-1, keepdims=True)
        y_ref[:, g * gw:(g + 1) * gw] = ((yg * lax.rsqrt(ms + NORM_EPS)) * nw_ref[:, g * gw:(g + 1) * gw]).astype(y_ref.dtype)

    @pl.when(c == n_chunks - 1)
    def _():
        hfin_ref[...] = ht_ref[...].T


def ssd_mixer(proj, dt_raw, conv_state8, h0, wts, *, batch, seq):
    q = SSD_CHUNK
    n_valid = min(seq, q)
    assert seq % n_valid == 0 and n_valid % SUBLANES == 0
    nc = seq // n_valid
    assert n_valid == q or nc == 1
    inner = h0.shape[1]
    gn = SSD_GROUPS * SSD_STATE
    assert inner % gn == 0 and (inner // gn) * gn == inner
    kb = inner // gn
    rows = lambda b, c: b * nc + c
    full = lambda b, c: (0, 0)
    in_specs = [
        pl.BlockSpec((n_valid, inner), lambda b, c: (rows(b, c), 0)),
        pl.BlockSpec((n_valid, inner), lambda b, c: (rows(b, c), 1)),
        pl.BlockSpec((n_valid, gn), lambda b, c: (rows(b, c), 2 * kb)),
        pl.BlockSpec((n_valid, gn), lambda b, c: (rows(b, c), 2 * kb + 1)),
        pl.BlockSpec((n_valid, LANES), lambda b, c: (rows(b, c), 0)),
        pl.BlockSpec((None, SUBLANES, inner), lambda b, c: (b, 0, 0)),
        pl.BlockSpec((None, SUBLANES, gn), lambda b, c: (b, 0, kb)),
        pl.BlockSpec((None, SUBLANES, gn), lambda b, c: (b, 0, kb + 1)),
        pl.BlockSpec((None, inner, SSD_STATE), lambda b, c: (b, 0, 0)),
        pl.BlockSpec((SSD_CONV, inner), full),
        pl.BlockSpec((SSD_CONV, gn), lambda b, c: (0, kb)),
        pl.BlockSpec((SSD_CONV, gn), lambda b, c: (0, kb + 1)),
        pl.BlockSpec((1, inner), full),
        pl.BlockSpec((1, gn), lambda b, c: (0, kb)),
        pl.BlockSpec((1, gn), lambda b, c: (0, kb + 1)),
        pl.BlockSpec((1, LANES), full),
        pl.BlockSpec((1, LANES), full),
        pl.BlockSpec((1, inner), full),
        pl.BlockSpec((1, inner), full),
        pl.BlockSpec((LANES, inner), full),
    ]
    cw, cb = wts["conv_w"], wts["conv_b"]
    y, hfin = pl.pallas_call(
        functools.partial(_ssd_kernel, q=q, n_valid=n_valid, n_chunks=nc),
        out_shape=(jax.ShapeDtypeStruct((batch * nc * q, inner), BF16),
                   jax.ShapeDtypeStruct((batch, inner, SSD_STATE), F32)),
        grid=(batch, nc),
        in_specs=in_specs,
        out_specs=(pl.BlockSpec((q, inner), lambda b, c: (rows(b, c), 0)),
                   pl.BlockSpec((None, inner, SSD_STATE), lambda b, c: (b, 0, 0))),
        scratch_shapes=[pltpu.VMEM((SSD_STATE, inner), F32),
                        pltpu.VMEM((q + SUBLANES, inner), F32),
                        pltpu.VMEM((q + SUBLANES, gn), F32),
                        pltpu.VMEM((q + SUBLANES, gn), F32)],
        compiler_params=_cparams("parallel", "arbitrary"),
        name="ssd_mixer",
    )(proj, proj, proj, proj, dt_raw, conv_state8, conv_state8, conv_state8, h0,
      cw, cw, cw, cb, cb, cb, wts["dt_bias"], wts["a_log"], wts["d_skip"], wts["norm_w"], wts["expand"])
    return y, hfin


def _sc_mixer_kernel(xn_ref, wb_ref, wc_ref, wx_ref, wg_ref, st_ref, w_ref, y_ref, last_ref, buf_ref, *, tq, slabs):
    if slabs == 1:
        @pl.when(pl.program_id(2) == 0)
        def _():
            buf_ref[0:SUBLANES, :] = st_ref[0]

    xn = xn_ref[...]
    prod = _dot(xn, wc_ref[...]) * _dot(xn, wx_ref[...])
    us = []
    for s in range(slabs):
        if slabs > 1:
            buf_ref[0:SUBLANES, :] = st_ref[s]
        ps = prod[s * tq:(s + 1) * tq, :]
        buf_ref[SUBLANES:SUBLANES + tq, :] = ps
        u = w_ref[SC_WIDTH - 1:SC_WIDTH, :] * ps
        for k in range(1, SC_WIDTH):
            u = u + w_ref[SC_WIDTH - 1 - k:SC_WIDTH - k, :] * buf_ref[SUBLANES - k:SUBLANES - k + tq, :]
        tail = ps[tq - SUBLANES:tq, :]
        buf_ref[0:SUBLANES, :] = tail
        last_ref[s] = tail
        us.append(u)
    u = us[0] if slabs == 1 else jnp.concatenate(us, axis=0)
    y_ref[...] = (_dot(xn, wb_ref[...]) * u * _silu(_dot(xn, wg_ref[...]))).astype(y_ref.dtype)


def sc_mixer(xn, w_sc, layer, state8, w, *, batch, seq, tq=1024, tc=512):
    k = xn.shape[1]
    dim = w.shape[1]
    slabs = batch if batch * seq <= tq else 1
    tq = seq if slabs > 1 else _tile(seq, tq)
    nt = seq // tq
    nb = batch // slabs
    ncb = dim // tc
    assert dim % tc == 0 and tq % SUBLANES == 0 and w_sc.shape[2] == 4 * dim

    def part(p):
        return _layer_spec(layer, (k, tc), lambda j, b, t: (0, p * ncb + j))

    return pl.pallas_call(
        functools.partial(_sc_mixer_kernel, tq=tq, slabs=slabs),
        out_shape=(jax.ShapeDtypeStruct((batch * seq, dim), BF16),
                   jax.ShapeDtypeStruct((batch, SUBLANES, dim), F32)),
        grid=(ncb, nb, nt),
        in_specs=[pl.BlockSpec((slabs * tq, k), lambda j, b, t: (b * nt + t, 0)),
                  part(0), part(1), part(2), part(3),
                  pl.BlockSpec((slabs, SUBLANES, tc), lambda j, b, t: (b, 0, j)),
                  pl.BlockSpec((SC_WIDTH, tc), lambda j, b, t: (0, j))],
        out_specs=(pl.BlockSpec((slabs * tq, tc), lambda j, b, t: (b * nt + t, j)),
                   pl.BlockSpec((slabs, SUBLANES, tc), lambda j, b, t: (b, 0, j))),
        scratch_shapes=[pltpu.VMEM((tq + SUBLANES, tc), F32)],
        compiler_params=_cparams("parallel", "parallel", "arbitrary"),
        name="sc_mixer",
    )(xn, w_sc, w_sc, w_sc, w_sc, state8, w)


def _top_blocks(gate, n_valid, axis=1):
    idx = lax.broadcasted_iota(jnp.int32, gate.shape, axis).astype(F32)
    g = jnp.where(idx < jnp.asarray(n_valid, F32), gate, NEG_INF)
    sel = jnp.zeros(gate.shape, F32)
    for _ in range(MOBA_TOPK):
        m = jnp.max(g, axis=axis, keepdims=True)
        first = jnp.min(jnp.where(g == m, idx, float(gate.shape[axis])), axis=axis, keepdims=True)
        pick = idx == first
        sel = jnp.where(pick & (m > 0.5 * NEG_INF), 1.0, sel)
        g = jnp.where(pick, -jnp.inf, g)
    return sel


def _moba_prompt_kernel(slopes_ref, q_ref, k_ref, v_ref, g_ref, o_ref, kmean_ref, kb_ref, vt_ref, mval_ref,
                        bias_ref, diag_ref, *, n_blocks):
    kv = pl.program_id(1)
    i = pl.program_id(2)
    blk = MOBA_BLOCK
    heads = range(Q_PER_KV)

    @pl.when(i == 0)
    def _():
        kmean_ref[...] = jnp.zeros(kmean_ref.shape, F32)
        for j in range(n_blocks):
            kj = k_ref[j * blk:(j + 1) * blk, :]
            kmean_ref[j:j + 1, :] = jnp.sum(kj, axis=0, keepdims=True) * (1.0 / blk)
            kb_ref[j] = kj.astype(BF16)
            vt_ref[j] = v_ref[j * blk:(j + 1) * blk, :].T.astype(BF16)
        rel = (lax.broadcasted_iota(jnp.int32, (blk, blk), 1)
               - lax.broadcasted_iota(jnp.int32, (blk, blk), 0)).astype(F32)
        for g in heads:
            slope = slopes_ref[kv * Q_PER_KV + g]
            bias_ref[:, g * blk:(g + 1) * blk] = -slope * rel
            diag_ref[:, g * blk:(g + 1) * blk] = jnp.where(rel >= 0, -slope * rel, NEG_INF)

    blk_idx = lax.broadcasted_iota(jnp.int32, (kmean_ref.shape[0], blk), 0)
    back = (i - blk_idx).astype(F32) * float(blk)
    kmean = kmean_ref[...]

    qt = []
    for g in heads:
        slope = slopes_ref[kv * Q_PER_KV + g]
        q_t = (q_ref[:, g * HEAD_DIM:(g + 1) * HEAD_DIM] * (HEAD_DIM ** -0.5)).T
        gate = _dot(kmean, q_t, precision=HIGHEST)
        sel = _top_blocks(gate, i, axis=0)
        mval_ref[:, g * blk:(g + 1) * blk] = jnp.where(sel > 0.0, -slope * back, NEG_INF)
        qt.append(q_t.astype(BF16))
    qt = jnp.concatenate(qt, axis=1)

    s = _dot(kb_ref[i], qt) + diag_ref[...]
    m0 = jnp.max(s, axis=0, keepdims=True)
    p = jnp.exp(s - m0)
    l0 = jnp.sum(p, axis=0, keepdims=True)
    acc0 = _dot(vt_ref[i], p.astype(BF16))

    def past_pair(t, carry):
        m, l, acc = carry
        js = (2 * t, 2 * t + 1)
        ss = [_dot(kb_ref[j], qt) + bias_ref[...] for j in js]
        masks = [mval_ref[pl.ds(j, 1), :] for j in js]
        m_new = m
        for s, mask in zip(ss, masks):
            m_new = jnp.maximum(m_new, jnp.max(s, axis=0, keepdims=True) + mask)
        l = jnp.exp(m - m_new) * l
        acc = jnp.exp(m - m_new) * acc
        for j, s, mask in zip(js, ss, masks):
            p = jnp.exp(s - (m_new - mask))
            l = l + jnp.sum(p, axis=0, keepdims=True)
            acc = acc + _dot(vt_ref[j], p.astype(BF16))
        return m_new, l, acc

    _, l, acc = lax.fori_loop(0, (i + 1) // 2, past_pair, (m0, l0, acc0))
    o_t = acc / l
    for g in heads:
        c = slice(g * HEAD_DIM, (g + 1) * HEAD_DIM)
        o_ref[:, c] = (o_t[:, g * blk:(g + 1) * blk].T * _silu(g_ref[:, c])).astype(o_ref.dtype)


def moba_prompt(proj, slopes, *, batch, seq):
    assert seq % MOBA_BLOCK == 0
    nb = seq // MOBA_BLOCK
    nbp = -(-nb // SUBLANES) * SUBLANES
    gw = Q_PER_KV * HEAD_DIM
    kcol = ATT_HEADS
    vcol = ATT_HEADS + KV_HEADS
    gcol = (ATT_HEADS + 2 * KV_HEADS) * HEAD_DIM // gw
    assert gcol * gw == (ATT_HEADS + 2 * KV_HEADS) * HEAD_DIM
    return pl.pallas_call(
        functools.partial(_moba_prompt_kernel, n_blocks=nb),
        out_shape=jax.ShapeDtypeStruct((batch * seq, ATT_HEADS * HEAD_DIM), BF16),
        grid=(batch, KV_HEADS, nb),
        in_specs=[pl.BlockSpec(memory_space=pltpu.SMEM),
                  pl.BlockSpec((MOBA_BLOCK, gw), lambda b, kv, i: (b * nb + i, kv)),
                  pl.BlockSpec((seq, HEAD_DIM), lambda b, kv, i: (b, kcol + kv)),
                  pl.BlockSpec((seq, HEAD_DIM), lambda b, kv, i: (b, vcol + kv)),
                  pl.BlockSpec((MOBA_BLOCK, gw), lambda b, kv, i: (b * nb + i, gcol + kv))],
        out_specs=pl.BlockSpec((MOBA_BLOCK, gw), lambda b, kv, i: (b * nb + i, kv)),
        scratch_shapes=[pltpu.VMEM((nbp, HEAD_DIM), F32),
                        pltpu.VMEM((nb, MOBA_BLOCK, HEAD_DIM), BF16),
                        pltpu.VMEM((nb, HEAD_DIM, MOBA_BLOCK), BF16),
                        pltpu.VMEM((nbp, Q_PER_KV * MOBA_BLOCK), F32),
                        pltpu.VMEM((MOBA_BLOCK, Q_PER_KV * MOBA_BLOCK), F32),
                        pltpu.VMEM((MOBA_BLOCK, Q_PER_KV * MOBA_BLOCK), F32)],
        compiler_params=_cparams("parallel", "parallel", "arbitrary"),
        name="moba_prompt",
    )(slopes, proj, proj, proj, proj)


def _page_specs(layer):
    return [pl.BlockSpec((None, None, PAGE_ROWS, HEAD_DIM),
                         lambda b, s, pt, p=p: (layer, pt[b, s * PAGES_PER_STEP + p], 0, 0))
            for p in range(PAGES_PER_STEP)]


def _kmean_kernel(pt_ref, *refs):
    del pt_ref
    pages, o_ref = refs[:PAGES_PER_STEP], refs[PAGES_PER_STEP]
    for bb in range(BLOCKS_PER_STEP):
        tot = None
        for pg in range(PAGES_PER_BLOCK):
            page = pages[bb * PAGES_PER_BLOCK + pg][...]
            part = jnp.sum(page.reshape(PAGE_ROWS // SUBLANES, SUBLANES, HEAD_DIM), axis=0)
            tot = part if tot is None else tot + part
        per_kv = tot[0:KV_HEADS]
        for par in range(1, SUBLANES // KV_HEADS):
            per_kv = per_kv + tot[par * KV_HEADS:(par + 1) * KV_HEADS]
        o_ref[bb] = per_kv * (1.0 / MOBA_BLOCK)


def paged_block_means(k_pools, page_table, *, layer, n_blocks):
    db = page_table.shape[0]
    assert n_blocks % BLOCKS_PER_STEP == 0 and SUBLANES % KV_HEADS == 0
    return pl.pallas_call(
        _kmean_kernel,
        out_shape=jax.ShapeDtypeStruct((db, n_blocks, KV_HEADS, HEAD_DIM), F32),
        grid_spec=pltpu.PrefetchScalarGridSpec(
            num_scalar_prefetch=1,
            grid=(db, n_blocks // BLOCKS_PER_STEP),
            in_specs=_page_specs(layer),
            out_specs=pl.BlockSpec((None, BLOCKS_PER_STEP, KV_HEADS, HEAD_DIM), lambda b, s, pt: (b, s, 0, 0))),
        compiler_params=_cparams("parallel", "arbitrary"),
        name="paged_block_means",
    )(page_table, *([k_pools] * PAGES_PER_STEP))


def _moba_sample_kernel(pt_ref, q_ref, kmean_ref, slope_ref, knew_ref, vnew_ref, *refs,
                        n_blocks, past_len, dec_seq):
    del pt_ref
    k_pages, v_pages = refs[:PAGES_PER_STEP], refs[PAGES_PER_STEP:2 * PAGES_PER_STEP]
    o_ref, m_ref, l_ref, acc_ref, sel_ref, base_ref = refs[2 * PAGES_PER_STEP:]
    step = pl.program_id(1)
    rows = q_ref.shape[0]
    rpk = rows // KV_HEADS
    q = q_ref[...] * (HEAD_DIM ** -0.5)
    qb = q.astype(BF16)
    slope = slope_ref[:, 0:1]
    lane = lax.broadcasted_iota(jnp.int32, (rows, LANES), 1)
    t_row = jnp.bitwise_and(lax.broadcasted_iota(jnp.int32, (rows, LANES), 0), dec_seq - 1)

    def per_head(fn):
        return jnp.concatenate([fn(h, slice(h * rpk, (h + 1) * rpk)) for h in range(KV_HEADS)], axis=0)

    def page_head(page_ref, h):
        return page_ref[pl.ds(h, PAGE_SIZE, stride=KV_HEADS), :].astype(BF16)

    @pl.when(step == 0)
    def _():
        gate = per_head(lambda h, r: _dot_nt(q[r, :], kmean_ref[h], precision=HIGHEST))
        sel_ref[...] = _top_blocks(gate, n_blocks)
        base_ref[...] = -slope * (past_len + t_row - lane).astype(F32)
        dist = (t_row - lane).astype(F32)
        s = (per_head(lambda h, r: _dot_nt(qb[r, :], knew_ref[h].astype(BF16)))
             + jnp.where(dist >= 0, -slope * dist, NEG_INF))
        m = jnp.max(s, axis=1, keepdims=True)
        p = jnp.exp(s - m)
        pb = p.astype(BF16)
        m_ref[...] = jnp.broadcast_to(m, m_ref.shape)
        l_ref[...] = jnp.broadcast_to(jnp.sum(p, axis=1, keepdims=True), l_ref.shape)
        acc_ref[...] = per_head(lambda h, r: _dot(pb[r, :], vnew_ref[h].astype(BF16)))

    sel = sel_ref[...]
    base = base_ref[...]
    m, l, acc = m_ref[:, 0:1], l_ref[:, 0:1], acc_ref[...]
    scores, offsets = [], []
    m_new = m
    for bb in range(BLOCKS_PER_STEP):
        blk_id = step * BLOCKS_PER_STEP + bb
        chosen = jnp.sum(jnp.where(lane == blk_id, sel, 0.0), axis=1, keepdims=True) > 0.0
        for pg in range(PAGES_PER_BLOCK):
            page_pos = (blk_id * MOBA_BLOCK + pg * PAGE_SIZE).astype(F32)
            mcol = jnp.where(chosen, slope * page_pos, NEG_INF)
            k_page = k_pages[bb * PAGES_PER_BLOCK + pg]
            s = per_head(lambda h, r: _dot_nt(qb[r, :], page_head(k_page, h))) + base
            m_new = jnp.maximum(m_new, jnp.max(s, axis=1, keepdims=True) + mcol)
            scores.append(s)
            offsets.append(mcol)
    alpha = jnp.exp(m - m_new)
    l, acc, m = alpha * l, alpha * acc, m_new
    for pg, (s, mcol) in enumerate(zip(scores, offsets)):
        p = jnp.exp(s - (m - mcol))
        pb = p.astype(BF16)
        l = l + jnp.sum(p, axis=1, keepdims=True)
        acc = acc + per_head(lambda h, r: _dot(pb[r, :], page_head(v_pages[pg], h)))
    m_ref[...] = jnp.broadcast_to(m, m_ref.shape)
    l_ref[...] = jnp.broadcast_to(l, l_ref.shape)
    acc_ref[...] = acc

    @pl.when(step == n_blocks // BLOCKS_PER_STEP - 1)
    def _():
        o_ref[...] = acc / l


def moba_sample(q_rows, kmean_t, slope_rows, k_new_pad, v_new_pad, k_pools, v_pools, page_table,
                *, layer, n_blocks, past_len, dec_seq):
    db, rows, _ = q_rows.shape
    rpk = rows // KV_HEADS
    for n in (KV_HEADS, dec_seq, rpk):
        assert n & (n - 1) == 0
    assert n_blocks % BLOCKS_PER_STEP == 0 and dec_seq * KV_HEADS <= LANES
    per_b = lambda *shape: pl.BlockSpec((None,) + shape, lambda b, s, pt: (b,) + (0,) * len(shape))
    return pl.pallas_call(
        functools.partial(_moba_sample_kernel, n_blocks=n_blocks, past_len=past_len, dec_seq=dec_seq),
        out_shape=jax.ShapeDtypeStruct((db, rows, HEAD_DIM), F32),
        grid_spec=pltpu.PrefetchScalarGridSpec(
            num_scalar_prefetch=1,
            grid=(db, n_blocks // BLOCKS_PER_STEP),
            in_specs=[per_b(rows, HEAD_DIM),
                      per_b(KV_HEADS, LANES, HEAD_DIM),
                      pl.BlockSpec((rows, LANES), lambda b, s, pt: (0, 0)),
                      per_b(KV_HEADS, LANES, HEAD_DIM),
                      per_b(KV_HEADS, LANES, HEAD_DIM)] + _page_specs(layer) + _page_specs(layer),
            out_specs=per_b(rows, HEAD_DIM),
            scratch_shapes=[pltpu.VMEM((rows, LANES), F32), pltpu.VMEM((rows, LANES), F32),
                            pltpu.VMEM((rows, HEAD_DIM), F32), pltpu.VMEM((rows, LANES), F32),
                            pltpu.VMEM((rows, PAGE_SIZE), F32)]),
        compiler_params=_cparams("parallel", "arbitrary"),
        name="moba_sample",
    )(page_table, q_rows, kmean_t, slope_rows, k_new_pad, v_new_pad,
      *([k_pools] * PAGES_PER_STEP), *([v_pools] * PAGES_PER_STEP))


def _gate_kernel(o_ref, g_ref, y_ref):
    y_ref[...] = (o_ref[...] * _silu(g_ref[...])).astype(y_ref.dtype)


def silu_gate(o, proj, *, col0, tn=1024):
    m, n = o.shape
    assert col0 % tn == 0 and n % tn == 0
    cb = col0 // tn
    return pl.pallas_call(
        _gate_kernel,
        out_shape=jax.ShapeDtypeStruct((m, n), BF16),
        grid=(n // tn,),
        in_specs=[pl.BlockSpec((m, tn), lambda j: (0, j)),
                  pl.BlockSpec((m, tn), lambda j: (0, cb + j))],
        out_specs=pl.BlockSpec((m, tn), lambda j: (0, j)),
        compiler_params=_cparams("parallel"),
        name="silu_gate",
    )(o, proj)


def _cross_layer_kernel(x_ref, nw_ref, wq_ref, k_ref, v_ref, wo_ref, *refs, slabs, emit_x, emit_norm):
    refs = list(refs)
    nn_ref = refs.pop(0) if emit_norm else None
    o_ref = refs.pop(0) if emit_x else None
    n_ref = refs.pop(0) if emit_norm else None
    width = CA_HEADS * CA_HEAD_DIM
    n_mem = k_ref.shape[1] // CA_HEADS
    x = x_ref[...]
    rows = x.shape[0] // slabs
    ms = jnp.mean(x * x, axis=-1, keepdims=True)
    xn = ((x * lax.rsqrt(ms + NORM_EPS)) * nw_ref[...]).astype(BF16)
    qg = _dot(xn, wq_ref[...])
    gated_rows = []
    for s_idx in range(slabs):
        r = slice(s_idx * rows, (s_idx + 1) * rows)
        gated = []
        for h in range(CA_HEADS):
            c = slice(h * CA_HEAD_DIM, (h + 1) * CA_HEAD_DIM)
            q = (qg[r, c] * (CA_HEAD_DIM ** -0.5)).astype(BF16)
            head_rows = pl.ds(h, n_mem, stride=CA_HEADS)
            s = _dot_nt(q, k_ref[s_idx, head_rows, :].astype(BF16))
            m = jnp.max(s, axis=1, keepdims=True)
            p = jnp.exp(s - m)
            l = jnp.sum(p, axis=1, keepdims=True)
            o = _dot(p.astype(BF16), v_ref[s_idx, head_rows, :].astype(BF16)) / l
            g = qg[r, width + h * CA_HEAD_DIM:width + (h + 1) * CA_HEAD_DIM]
            gated.append(o * _silu(g))
        gated_rows.append(jnp.concatenate(gated, axis=1))
    og = gated_rows[0] if slabs == 1 else jnp.concatenate(gated_rows, axis=0)
    x_new = x + _dot(og.astype(BF16), wo_ref[...])
    if emit_x:
        o_ref[...] = x_new
    if emit_norm:
        ms_new = jnp.mean(x_new * x_new, axis=-1, keepdims=True)
        n_ref[...] = ((x_new * lax.rsqrt(ms_new + NORM_EPS)) * nn_ref[...]).astype(n_ref.dtype)


def cross_layer(x, norm_w, w_q, mem_k, mem_v, w_out, layer, *, batch, seq, mem_layer=0, tm=512,
                next_norm_w=None, next_dtype=F32, emit_x=True):
    width = CA_HEADS * CA_HEAD_DIM
    d = x.shape[1]
    mem_rows = mem_k.shape[2]
    slabs = batch if batch * seq <= tm else 1
    tm = batch * seq if slabs > 1 else _tile(seq, tm)
    nt = batch * seq // tm // (batch // slabs)
    const = lambda i: (0, 0)
    emit_norm = next_norm_w is not None
    assert emit_x or emit_norm
    row_spec = pl.BlockSpec((tm, d), lambda i: (i, 0))
    in_specs = [row_spec,
                pl.BlockSpec((1, d), const),
                _layer_spec(layer, (d, 2 * width), const),
                pl.BlockSpec((None, slabs, mem_rows, CA_HEAD_DIM), lambda i: (mem_layer, i // nt, 0, 0)),
                pl.BlockSpec((None, slabs, mem_rows, CA_HEAD_DIM), lambda i: (mem_layer, i // nt, 0, 0)),
                _layer_spec(layer, (width, d), const)]
    args = [x, norm_w.reshape(1, d), w_q, mem_k, mem_v, w_out]
    out_specs, out_shapes = [], []
    if emit_norm:
        in_specs.append(pl.BlockSpec((1, d), const))
        args.append(next_norm_w.reshape(1, d))
    if emit_x:
        out_specs.append(row_spec)
        out_shapes.append(jax.ShapeDtypeStruct((batch * seq, d), F32))
    if emit_norm:
        out_specs.append(row_spec)
        out_shapes.append(jax.ShapeDtypeStruct((batch * seq, d), next_dtype))
    outs = pl.pallas_call(
        functools.partial(_cross_layer_kernel, slabs=slabs, emit_x=emit_x, emit_norm=emit_norm),
        out_shape=tuple(out_shapes),
        grid=(batch * seq // tm,),
        in_specs=in_specs,
        out_specs=tuple(out_specs),
        compiler_params=_cparams("parallel"),
        name="cross_layer",
    )(*args)
    return outs[0] if len(outs) == 1 else outs


def _alibi_slopes():
    return np.array([2.0 ** (-8.0 * (h + 1) / ATT_HEADS) for h in range(ATT_HEADS)], dtype=np.float32)


def _pad_rows(a, n, front=False):
    extra = n - a.shape[-2]
    pad = [(0, 0)] * a.ndim
    pad[-2] = (extra, 0) if front else (0, extra)
    return jnp.pad(a, pad)


def _pad_lanes(a):
    pad = [(0, 0)] * a.ndim
    pad[-1] = (0, LANES - a.shape[-1])
    return jnp.pad(a, pad)


def _even_layer(x, w, layer, conv_state, ssm_state, sc_state, *, batch, seq, xn=None):
    d = x.shape[1]
    inner = d
    conv_ch = inner + 2 * SSD_GROUPS * SSD_STATE
    if xn is None:
        xn = rmsnorm(x, w["norm_mix"], out_dtype=BF16)
    proj = matmul(xn, w["w_in"], layer, n=inner + conv_ch)
    dt_raw = matmul(xn, w["w_in"], layer, col0=inner + conv_ch, n=LANES)
    new_conv = proj.reshape(batch, seq, -1)[:, seq - (SSD_CONV - 1):, inner:inner + conv_ch]

    y, h_fin = ssd_mixer(proj, dt_raw, _pad_rows(conv_state, SUBLANES, front=True),
                         ssm_state.reshape(batch, inner, SSD_STATE), w, batch=batch, seq=seq)
    if y.shape[0] != batch * seq:
        y = y.reshape(batch, -1, inner)[:, :seq].reshape(batch * seq, inner)

    y_sc, sc_tail = sc_mixer(xn, w["w_sc"], layer, _pad_rows(sc_state, SUBLANES, front=True), w["sc_w"],
                             batch=batch, seq=seq)
    x = matmul_residual([(y, w["w_out"], 0), (y_sc, w["w_out"], 1)], x, layer)
    return (x, new_conv, h_fin.reshape(batch, inner // SSD_HEADDIM, SSD_HEADDIM, SSD_STATE),
            sc_tail[:, SUBLANES - (SC_WIDTH - 1):, :])


def _odd_prompt(x, w, slopes, *, layer, batch, seq):
    att_q = ATT_HEADS * HEAD_DIM
    proj, k, v = norm_matmul(x, w["norm_mix"], w["w_in"], layer, tap_col0=att_q)
    og = moba_prompt(proj, slopes, batch=batch, seq=seq)
    return (matmul_residual([(og, w["w_out"], 0)], x, layer),
            k.reshape(batch, seq, KV_HEADS, HEAD_DIM), v.reshape(batch, seq, KV_HEADS, HEAD_DIM))


def _odd_sample(x, w, k_pools, v_pools, page_table, slope_rows, *, layer, batch, seq, past_len):
    assert past_len % MOBA_BLOCK == 0 and seq <= MOBA_BLOCK and seq <= LANES
    n_blocks = past_len // MOBA_BLOCK
    assert MOBA_TOPK <= n_blocks <= LANES
    att_q, att_kv = ATT_HEADS * HEAD_DIM, KV_HEADS * HEAD_DIM
    proj, k_new, v_new = norm_matmul(x, w["norm_mix"], w["w_in"], layer, tap_col0=att_q)
    q_rows = proj[:, :att_q].reshape(batch, seq, KV_HEADS, Q_PER_KV, HEAD_DIM).transpose(0, 2, 3, 1, 4)
    q_rows = q_rows.reshape(batch, ATT_HEADS * seq, HEAD_DIM)
    kmean = paged_block_means(k_pools, page_table, layer=layer, n_blocks=n_blocks)
    new_keys = lambda a: _pad_rows(a.reshape(batch, seq, KV_HEADS, HEAD_DIM).transpose(0, 2, 1, 3), LANES)
    o = moba_sample(q_rows, _pad_rows(kmean.transpose(0, 2, 1, 3), LANES), slope_rows,
                    new_keys(k_new), new_keys(v_new),
                    k_pools, v_pools, page_table, layer=layer, n_blocks=n_blocks, past_len=past_len, dec_seq=seq)
    o = o.reshape(batch, KV_HEADS, Q_PER_KV, seq, HEAD_DIM).transpose(0, 3, 1, 2, 4).reshape(batch * seq, att_q)
    og = silu_gate(o, proj, col0=att_q + 2 * att_kv)
    return (matmul_residual([(og, w["w_out"], 0)], x, layer),
            k_new.reshape(batch, seq, KV_HEADS, HEAD_DIM), v_new.reshape(batch, seq, KV_HEADS, HEAD_DIM))


def _cross_layer(x, w, layer, mem_k, mem_v, *, batch, seq, **follow):
    return cross_layer(x, w["norm_cross"], w["ca_w_q"], mem_k, mem_v, w["ca_w_out"], layer, batch=batch, seq=seq,
                       **follow)


def kernel(x_prompt, x_sample, mem_prompt, cache_k, cache_v, page_table, state_conv, state_ssm, state_sc,
           cache_mem_k, cache_mem_v, norm_mix_w, norm_cross_w, norm_mem_w, final_norm_w,
           ev_w_in, ev_conv_w, ev_conv_b, ev_dt_bias, ev_a_log, ev_d_skip, ev_norm_w, ev_sc_w, ev_w_out,
           od_w_in, od_w_out, ca_w_q, ca_w_kv, ca_w_out):
    bp, sp, d = x_prompt.shape
    bs, ss, _ = x_sample.shape
    depth = norm_mix_w.shape[0]
    n_mem = mem_prompt.shape[1]
    n_heads = ev_dt_bias.shape[1]
    inner = n_heads * SSD_HEADDIM
    conv_ch = inner + 2 * SSD_GROUPS * SSD_STATE
    past_len = page_table.shape[1] * PAGE_SIZE
    ca_w = CA_HEADS * CA_HEAD_DIM
    assert inner == d and n_heads <= LANES

    slopes = jnp.asarray(_alibi_slopes())
    slope_rows = jnp.asarray(np.repeat(_alibi_slopes(), ss)[:, None] * np.ones((1, LANES), np.float32))
    expand_np = np.zeros((LANES, inner), np.float32)
    for hd in range(n_heads):
        expand_np[hd, hd * SSD_HEADDIM:(hd + 1) * SSD_HEADDIM] = 1.0
    expand = jnp.asarray(expand_np, dtype=BF16)

    xp = x_prompt.reshape(bp * sp, d)
    xs = x_sample.reshape(bs * ss, d)
    mem = mem_prompt.reshape(bp * n_mem, d)
    k_pools = cache_k.reshape(cache_k.shape[0], cache_k.shape[1], PAGE_ROWS, HEAD_DIM)
    v_pools = cache_v.reshape(cache_v.shape[0], cache_v.shape[1], PAGE_ROWS, HEAD_DIM)

    dt0 = inner + conv_ch
    ev_w_in_b = ev_w_in.astype(BF16)
    ev_w_sc_b = ev_w_in_b[:, :, dt0 + n_heads:]
    ev_w_out_b = ev_w_out.astype(BF16)
    od_w_in_b, od_w_out_b = od_w_in.astype(BF16), od_w_out.astype(BF16)
    ca_w_q_b, ca_w_kv_b, ca_w_out_b = ca_w_q.astype(BF16), ca_w_kv.astype(BF16), ca_w_out.astype(BF16)

    pk, pv, sk, sv = [], [], [], []
    pconv, pssm, psc, sconv, sssm, ssc = [], [], [], [], [], []
    pmk, pmv = [], []
    xn_p = xn_s = None
    mem_ks = cache_mem_k.reshape(depth, bs, n_mem * CA_HEADS, CA_HEAD_DIM)
    mem_vs = cache_mem_v.reshape(depth, bs, n_mem * CA_HEADS, CA_HEAD_DIM)
    for l in range(depth):
        i = l // 2
        if l % 2 == 0:
            w = {
                "norm_mix": norm_mix_w[l],
                "w_in": ev_w_in_b,
                "w_sc": ev_w_sc_b,
                "w_out": ev_w_out_b,
                "conv_w": ev_conv_w[i],
                "conv_b": ev_conv_b[i].reshape(1, conv_ch),
                "dt_bias": _pad_lanes(ev_dt_bias[i].reshape(1, n_heads)),
                "a_log": _pad_lanes(ev_a_log[i].reshape(1, n_heads)),
                "d_skip": jnp.repeat(ev_d_skip[i], SSD_HEADDIM).reshape(1, inner),
                "norm_w": ev_norm_w[i].reshape(1, inner),
                "expand": expand,
                "sc_w": ev_sc_w[i],
            }
            xp, c1, s1, q1 = _even_layer(xp, w, i, jnp.zeros((bp, SSD_CONV - 1, conv_ch), F32),
                                         jnp.zeros((bp, n_heads, SSD_HEADDIM, SSD_STATE), F32),
                                         jnp.zeros((bp, SC_WIDTH - 1, d), F32), batch=bp, seq=sp, xn=xn_p)
            xs, c2, s2, q2 = _even_layer(xs, w, i, state_conv[i], state_ssm[i], state_sc[i], batch=bs, seq=ss,
                                         xn=xn_s)
            pconv.append(c1); pssm.append(s1); psc.append(q1)
            sconv.append(c2); sssm.append(s2); ssc.append(q2)
        else:
            w = {"norm_mix": norm_mix_w[l], "w_in": od_w_in_b, "w_out": od_w_out_b}
            xp, k1, v1 = _odd_prompt(xp, w, slopes, layer=i, batch=bp, seq=sp)
            xs, k2, v2 = _odd_sample(xs, w, k_pools, v_pools, page_table, slope_rows,
                                     layer=i, batch=bs, seq=ss, past_len=past_len)
            pk.append(k1); pv.append(v1); sk.append(k2); sv.append(v2)
        wc = {"norm_cross": norm_cross_w[l], "ca_w_q": ca_w_q_b, "ca_w_out": ca_w_out_b}
        _, mk, mv = norm_matmul(mem, norm_mem_w[l], ca_w_kv_b, l, tap_col0=0)
        assert mk.shape[1:] == (CA_HEADS, CA_HEAD_DIM)
        pmk.append(mk.reshape(bp, n_mem, CA_HEADS, CA_HEAD_DIM))
        pmv.append(mv.reshape(bp, n_mem, CA_HEADS, CA_HEAD_DIM))
        mk = mk.reshape(1, bp, n_mem * CA_HEADS, CA_HEAD_DIM)
        mv = mv.reshape(1, bp, n_mem * CA_HEADS, CA_HEAD_DIM)
        if l == depth - 1:
            follow = dict(next_norm_w=final_norm_w, next_dtype=F32, emit_x=False)
        elif (l + 1) % 2 == 0:
            follow = dict(next_norm_w=norm_mix_w[l + 1], next_dtype=BF16)
        else:
            follow = {}
        out_p = _cross_layer(xp, wc, l, mk, mv, batch=bp, seq=sp, **follow)
        out_s = _cross_layer(xs, wc, l, mem_ks, mem_vs, batch=bs, seq=ss, mem_layer=l, **follow)
        if l == depth - 1:
            y_prompt, y_sample = out_p.reshape(bp, sp, d), out_s.reshape(bs, ss, d)
        elif follow:
            (xp, xn_p), (xs, xn_s) = out_p, out_s
        else:
            xp, xs, xn_p, xn_s = out_p, out_s, None, None
    return (y_prompt, y_sample,
            jnp.stack(pk), jnp.stack(pv), jnp.stack(pconv), jnp.stack(pssm), jnp.stack(psc),
            jnp.stack(pmk), jnp.stack(pmv),
            jnp.stack(sk), jnp.stack(sv), jnp.stack(sconv), jnp.stack(sssm), jnp.stack(ssc))
```
